```python
import math
import jax, jax.numpy as jnp
from jax import lax
import numpy as np

D_MODEL = 1024
BATCH = 8
SEQ = 2048
DEPTH = 1

EPS = 1e-6
D_FF = 2816
FFN_RES_WEIGHT = 0.5
N_MOD = 9

SWA_HEADS = 8
SWA_KV_HEADS = 2
SWA_HEAD_DIM = 64
SWA_GROUP = SWA_HEADS // SWA_KV_HEADS
WINDOW = 128

MLA_HEADS = 4
MLA_Q_RANK = 256
MLA_KV_RANK = 128
MLA_NOPE = 128
MLA_ROPE = 64
MLA_V = 128
ROPE_THETA = 10000.0
Q_BLOCK = 128

NUM_BUCKETS = 32
MAX_DISTANCE = 128

IN_SPLITS = (SWA_HEADS * SWA_HEAD_DIM, SWA_KV_HEADS * SWA_HEAD_DIM, SWA_KV_HEADS * SWA_HEAD_DIM,
             MLA_Q_RANK, MLA_KV_RANK, MLA_ROPE)
D_IN = sum(IN_SPLITS)
MIX_OUT = SWA_HEADS * SWA_HEAD_DIM + MLA_HEADS * MLA_V

kernel_name = "hybrid_swa_sink_mla_macaron_adaln"


def rms_norm(x, g):
    xf = x.astype(jnp.float32)
    y = xf * lax.rsqrt(jnp.mean(xf * xf, axis=-1, keepdims=True) + EPS)
    return (y * g.astype(jnp.float32)).astype(x.dtype)


def modulate(h, shift, scale):
    return h * (1 + scale[:, None, :]) + shift[:, None, :]


def swiglu(h, w_gate, w_up, w_down):
    return (jax.nn.silu(h @ w_gate) * (h @ w_up)) @ w_down


def t5_bucket(dist):
    max_exact = NUM_BUCKETS // 2
    n = jnp.maximum(dist, 0)
    nf = jnp.maximum(n, 1).astype(jnp.float32)
    large = max_exact + (jnp.log(nf / max_exact) / math.log(MAX_DISTANCE / max_exact)
                         * (NUM_BUCKETS - max_exact)).astype(jnp.int32)
    large = jnp.minimum(large, NUM_BUCKETS - 1)
    return jnp.where(n < max_exact, n, large)


def rope_tables(seq, dim):
    inv = ROPE_THETA ** (-jnp.arange(0, dim, 2, dtype=jnp.float32) / dim)
    ang = jnp.arange(seq, dtype=jnp.float32)[:, None] * inv[None, :]
    return jnp.cos(ang), jnp.sin(ang)


def apply_rope(x, cos, sin):
    c = cos[None, :, None, :].astype(x.dtype)
    s = sin[None, :, None, :].astype(x.dtype)
    x1, x2 = jnp.split(x, 2, axis=-1)
    return jnp.concatenate([x1 * c - x2 * s, x2 * c + x1 * s], axis=-1)


def sliding_window_gqa(q, k, v, sinks, rel_bias):
    B, S = q.shape[0], q.shape[1]
    nb = S // WINDOW
    qb = q.reshape(B, nb, WINDOW, SWA_KV_HEADS, SWA_GROUP, SWA_HEAD_DIM)
    kb = k.reshape(B, nb, WINDOW, SWA_KV_HEADS, SWA_HEAD_DIM)
    vb = v.reshape(B, nb, WINDOW, SWA_KV_HEADS, SWA_HEAD_DIM)
    pad_k = jnp.zeros_like(kb[:, :1])
    kk = jnp.concatenate([jnp.concatenate([pad_k, kb[:, :-1]], axis=1), kb], axis=2)
    vv = jnp.concatenate([jnp.concatenate([jnp.zeros_like(vb[:, :1]), vb[:, :-1]], axis=1), vb], axis=2)

    s = jnp.einsum('bnqhgd,bnkhd->bnhgqk', qb, kk).astype(jnp.float32) * (SWA_HEAD_DIM ** -0.5)

    qi = jnp.arange(WINDOW)[:, None]
    kj = jnp.arange(2 * WINDOW)[None, :]
    dist = qi + WINDOW - kj
    band = (dist >= 0) & (dist < WINDOW)
    blk = jnp.arange(nb)[:, None]
    key_ok = (blk > 0) | (jnp.arange(2 * WINDOW)[None, :] >= WINDOW)
    valid = band[None, :, :] & key_ok[:, None, :]

    bias = rel_bias.astype(jnp.float32)[t5_bucket(dist)]
    bias = bias.transpose(2, 0, 1).reshape(SWA_KV_HEADS, SWA_GROUP, WINDOW, 2 * WINDOW)
    s = jnp.where(valid[None, :, None, None], s + bias[None, None], -jnp.inf)

    sink = jnp.broadcast_to(sinks.astype(jnp.float32).reshape(1, 1, SWA_KV_HEADS, SWA_GROUP, 1, 1),
                            s.shape[:-1] + (1,))
    p = jax.nn.softmax(jnp.concatenate([s, sink], axis=-1), axis=-1)[..., :-1]
    o = jnp.einsum('bnhgqk,bnkhd->bnqhgd', p.astype(vv.dtype), vv)
    return o.reshape(B, S, SWA_HEADS * SWA_HEAD_DIM)


def mla_attention(q_lat, kv_lat, k_rope, q_norm, kv_norm, w_uq, w_ukv):
    B, S = q_lat.shape[0], q_lat.shape[1]
    cos, sin = rope_tables(S, MLA_ROPE)
    q = (rms_norm(q_lat, q_norm) @ w_uq).reshape(B, S, MLA_HEADS, MLA_NOPE + MLA_ROPE)
    q_nope, q_rope = q[..., :MLA_NOPE], apply_rope(q[..., MLA_NOPE:], cos, sin)
    kv = (rms_norm(kv_lat, kv_norm) @ w_ukv).reshape(B, S, MLA_HEADS, MLA_NOPE + MLA_V)
    k_nope, v = kv[..., :MLA_NOPE], kv[..., MLA_NOPE:]
    k_r = apply_rope(k_rope[:, :, None, :], cos, sin)[:, :, 0, :]
    scale = (MLA_NOPE + MLA_ROPE) ** -0.5

    nb = S // Q_BLOCK
    qn = q_nope.reshape(B, nb, Q_BLOCK, MLA_HEADS, MLA_NOPE).transpose(1, 0, 2, 3, 4)
    qr = q_rope.reshape(B, nb, Q_BLOCK, MLA_HEADS, MLA_ROPE).transpose(1, 0, 2, 3, 4)
    kpos = jnp.arange(S)

    def one_block(args):
        qn_b, qr_b, i = args
        s = (jnp.einsum('bqhd,bkhd->bhqk', qn_b, k_nope)
             + jnp.einsum('bqhd,bkd->bhqk', qr_b, k_r)).astype(jnp.float32) * scale
        qpos = i * Q_BLOCK + jnp.arange(Q_BLOCK)
        s = jnp.where(kpos[None, :] <= qpos[:, None], s, -jnp.inf)
        p = jax.nn.softmax(s, axis=-1).astype(v.dtype)
        return jnp.einsum('bhqk,bkhd->bqhd', p, v)

    o = lax.map(one_block, (qn, qr, jnp.arange(nb)))
    return o.transpose(1, 0, 2, 3, 4).reshape(B, S, MLA_HEADS * MLA_V)


def token_mixing(h, w_in, q_norm, kv_norm, w_uq, w_ukv, sinks, w_o, rel_bias):
    B, S = h.shape[0], h.shape[1]
    proj = h @ w_in
    idx = np.cumsum(IN_SPLITS)[:-1].tolist()
    q_a, k_a, v_a, q_lat, kv_lat, k_rope = jnp.split(proj, idx, axis=-1)
    out_a = sliding_window_gqa(q_a.reshape(B, S, SWA_HEADS, SWA_HEAD_DIM),
                               k_a.reshape(B, S, SWA_KV_HEADS, SWA_HEAD_DIM),
                               v_a.reshape(B, S, SWA_KV_HEADS, SWA_HEAD_DIM),
                               sinks, rel_bias)
    out_b = mla_attention(q_lat, kv_lat, k_rope, q_norm, kv_norm, w_uq, w_ukv)
    return jnp.concatenate([out_a, out_b], axis=-1) @ w_o


def setup_inputs(seed: int = 0) -> dict:
    key = jax.random.key(seed)
    ks = jax.random.split(key, 24)
    L, D = DEPTH, D_MODEL

    def w(k, shape, fan_in):
        return jax.random.normal(k, shape, jnp.float32) * fan_in ** -0.5

    def gain(k, shape):
        return 1.0 + 0.05 * jax.random.normal(k, shape, jnp.float32)

    return {
        "x": jax.random.normal(ks[0], (BATCH, SEQ, D), jnp.float32),
        "c": jax.random.normal(ks[1], (BATCH, D), jnp.float32),
        "w_mod": w(ks[2], (L, D, N_MOD * D), D) * 0.5,
        "b_mod": 0.02 * jax.random.normal(ks[3], (L, N_MOD * D), jnp.float32),
        "norm_ffn1": gain(ks[4], (L, D)),
        "ffn1_gate": w(ks[5], (L, D, D_FF), D),
        "ffn1_up": w(ks[6], (L, D, D_FF), D),
        "ffn1_down": w(ks[7], (L, D_FF, D), D_FF),
        "norm_mix": gain(ks[8], (L, D)),
        "w_in": w(ks[9], (L, D, D_IN), D),
        "q_norm": gain(ks[10], (L, MLA_Q_RANK)),
        "kv_norm": gain(ks[11], (L, MLA_KV_RANK)),
        "w_uq": w(ks[12], (L, MLA_Q_RANK, MLA_HEADS * (MLA_NOPE + MLA_ROPE)), MLA_Q_RANK),
        "w_ukv": w(ks[13], (L, MLA_KV_RANK, MLA_HEADS * (MLA_NOPE + MLA_V)), MLA_KV_RANK),
        "sinks": 0.5 * jax.random.normal(ks[14], (L, SWA_HEADS), jnp.float32),
        "w_o": w(ks[15], (L, MIX_OUT, D), MIX_OUT),
        "norm_ffn2": gain(ks[16], (L, D)),
        "ffn2_gate": w(ks[17], (L, D, D_FF), D),
        "ffn2_up": w(ks[18], (L, D, D_FF), D),
        "ffn2_down": w(ks[19], (L, D_FF, D), D_FF),
        "rel_bias": 0.5 * jax.random.normal(ks[20], (NUM_BUCKETS, SWA_HEADS), jnp.float32),
        "norm_final": gain(ks[21], (D,)),
    }


def reference(x, c, w_mod, b_mod, norm_ffn1, ffn1_gate, ffn1_up, ffn1_down, norm_mix, w_in,
              q_norm, kv_norm, w_uq, w_ukv, sinks, w_o, norm_ffn2, ffn2_gate, ffn2_up, ffn2_down,
              rel_bias, norm_final):
    c_act = jax.nn.silu(c)
    for l in range(DEPTH):
        mod = c_act @ w_mod[l] + b_mod[l]
        sh1, sc1, g1, sh2, sc2, g2, sh3, sc3, g3 = jnp.split(mod, N_MOD, axis=-1)

        h = modulate(rms_norm(x, norm_ffn1[l]), sh1, sc1)
        x = x + FFN_RES_WEIGHT * g1[:, None, :] * swiglu(h, ffn1_gate[l], ffn1_up[l], ffn1_down[l])

        h = modulate(rms_norm(x, norm_mix[l]), sh2, sc2)
        x = x + g2[:, None, :] * token_mixing(h, w_in[l], q_norm[l], kv_norm[l], w_uq[l], w_ukv[l],
                                              sinks[l], w_o[l], rel_bias)

        h = modulate(rms_norm(x, norm_ffn2[l]), sh3, sc3)
        x = x + FFN_RES_WEIGHT * g3[:, None, :] * swiglu(h, ffn2_gate[l], ffn2_up[l], ffn2_down[l])
    return rms_norm(x, norm_final)
```

```python
import functools
import math

import jax
import jax.numpy as jnp
import numpy as np
from jax import lax
from jax.experimental import pallas as pl
from jax.experimental.pallas import tpu as pltpu

EPS = 1e-6
FFN_RES_WEIGHT = 0.5
N_MOD = 9

SWA_HEADS = 8
SWA_KV_HEADS = 2
SWA_HEAD_DIM = 64
SWA_GROUP = SWA_HEADS // SWA_KV_HEADS
WINDOW = 128

MLA_HEADS = 4
MLA_Q_RANK = 256
MLA_KV_RANK = 128
MLA_NOPE = 128
MLA_ROPE = 64
MLA_V = 128
MLA_QK = MLA_NOPE + MLA_ROPE
ROPE_THETA = 10000.0

NUM_BUCKETS = 32
MAX_DISTANCE = 128

MASK_VALUE = -1e30
V7X_VMEM_LIMIT_BYTES = 56 * 1024 * 1024

F32 = jnp.float32
BF16 = jnp.bfloat16
NT_DIMS = (((1,), (1,)), ((), ()))


def _params(n_grid_dims):
    return pltpu.CompilerParams(dimension_semantics=("arbitrary",) * n_grid_dims,
                                vmem_limit_bytes=V7X_VMEM_LIMIT_BYTES)


def _resident(shape):
    return pl.BlockSpec(shape, lambda *_: (0,) * len(shape), pipeline_mode=pl.Buffered(1))


def _rms_norm(x, gain):
    ms = jnp.mean(x * x, axis=-1, keepdims=True)
    return x * lax.rsqrt(ms + EPS) * gain


def _silu(x):
    return x * (1.0 / (1.0 + jnp.exp(-x)))


def _dot(a, b):
    return jnp.dot(a, b, preferred_element_type=F32)


def _mod_kernel(c_ref, w_ref, b_ref, o_ref):
    c_act = _silu(c_ref[...]).astype(BF16)
    o_ref[...] = _dot(c_act, w_ref[...].astype(BF16)) + b_ref[...]


def _mod(c, w_mod, b_mod, *, tn=1024):
    batch, d = c.shape
    n = w_mod.shape[1]
    return pl.pallas_call(
        _mod_kernel,
        grid=(n // tn,),
        in_specs=[pl.BlockSpec((batch, d), lambda j: (0, 0)),
                  pl.BlockSpec((d, tn), lambda j: (0, j)),
                  pl.BlockSpec((1, tn), lambda j: (0, j))],
        out_specs=pl.BlockSpec((batch, tn), lambda j: (0, j)),
        out_shape=jax.ShapeDtypeStruct((batch, n), F32),
        compiler_params=_params(1),
        name="mod",
    )(c, w_mod, b_mod.reshape(1, n))


def _ffn_kernel(*refs, mod_row, f_chunks, final_norm):
    if final_norm:
        x_ref, mod_ref, g_ref, wg_ref, wu_ref, wd_ref, gf_ref, o_ref, h_ref, a_ref = refs
    else:
        x_ref, mod_ref, g_ref, wg_ref, wu_ref, wd_ref, o_ref, h_ref, a_ref = refs
    x = x_ref[...]
    shift = mod_ref[0, mod_row:mod_row + 1, :]
    scale = mod_ref[0, mod_row + 1:mod_row + 2, :]
    gate = mod_ref[0, mod_row + 2:mod_row + 3, :]
    h_ref[...] = (_rms_norm(x, g_ref[...]) * (1.0 + scale) + shift).astype(BF16)
    f0 = 0
    for fc in f_chunks:
        g = _dot(h_ref[...], wg_ref[:, f0:f0 + fc])
        u = _dot(h_ref[...], wu_ref[:, f0:f0 + fc])
        a_ref[:, f0:f0 + fc] = (_silu(g) * u).astype(BF16)
        f0 += fc
    y = _dot(a_ref[...], wd_ref[...])
    out = x + FFN_RES_WEIGHT * gate * y
    if final_norm:
        out = _rms_norm(out, gf_ref[...])
    o_ref[...] = out


def _ffn(x2d, mod3, gain, wg, wu, wd, *, mod_row, seq, final_gain=None, tm=512,
         f_chunks=(1024, 1024, 768)):
    m, d = x2d.shape
    f = wg.shape[1]
    assert sum(f_chunks) == f and seq % tm == 0
    tiles_per_seq = seq // tm
    final_norm = final_gain is not None
    in_specs = [pl.BlockSpec((tm, d), lambda i: (i, 0)),
                pl.BlockSpec((1, N_MOD, d), lambda i: (i // tiles_per_seq, 0, 0)),
                _resident((1, d)),
                _resident((d, f)), _resident((d, f)), _resident((f, d))]
    args = [x2d, mod3, gain.reshape(1, d), wg, wu, wd]
    if final_norm:
        in_specs.append(_resident((1, d)))
        args.append(final_gain.reshape(1, d))
    return pl.pallas_call(
        functools.partial(_ffn_kernel, mod_row=mod_row, f_chunks=f_chunks, final_norm=final_norm),
        grid=(m // tm,),
        in_specs=in_specs,
        out_specs=pl.BlockSpec((tm, d), lambda i: (i, 0)),
        out_shape=jax.ShapeDtypeStruct((m, d), F32),
        scratch_shapes=[pltpu.VMEM((tm, d), BF16), pltpu.VMEM((tm, f), BF16)],
        compiler_params=_params(1),
        name="ffn_final" if final_norm else "ffn",
    )(*args)


_QA0, _KVA0, _QLAT0, _KVLAT0, _KR0, _KRS0, _PROJ_W = 0, 512, 768, 1024, 1152, 1216, 1280
_UQ_ROPE0 = MLA_HEADS * MLA_NOPE
_UQ_ROPES0 = _UQ_ROPE0 + MLA_HEADS * MLA_ROPE


def _mix_proj_kernel(x_ref, mod_ref, g_ref, win_ref, qn_ref, kvn_ref, wuq_ref, wukv_ref, cos_ref, sin_ref,
                     qa_ref, kva_ref, qm_ref, km_ref, vm_ref):
    x = x_ref[...]
    shift = mod_ref[0, 3:4, :]
    scale = mod_ref[0, 4:5, :]
    h = (_rms_norm(x, g_ref[...]) * (1.0 + scale) + shift).astype(BF16)
    proj = _dot(h, win_ref[...])
    qa_ref[...] = (proj[:, _QA0:_KVA0] * (SWA_HEAD_DIM ** -0.5)).astype(BF16)
    kva_ref[...] = proj[:, _KVA0:_QLAT0].astype(BF16)

    q_lat = _rms_norm(proj[:, _QLAT0:_KVLAT0], qn_ref[...]).astype(BF16)
    kv_lat = _rms_norm(proj[:, _KVLAT0:_KR0], kvn_ref[...]).astype(BF16)
    q_all = _dot(q_lat, wuq_ref[...])
    kv_all = _dot(kv_lat, wukv_ref[...])

    cos = cos_ref[...]
    sin = sin_ref[...]
    k_rope = (proj[:, _KR0:_KRS0] * cos + proj[:, _KRS0:_PROJ_W] * sin).astype(BF16)
    q_scale = MLA_QK ** -0.5
    for hd in range(MLA_HEADS):
        q_nope = q_all[:, hd * MLA_NOPE:(hd + 1) * MLA_NOPE]
        q_rope = (q_all[:, _UQ_ROPE0 + hd * MLA_ROPE:_UQ_ROPE0 + (hd + 1) * MLA_ROPE] * cos
                  + q_all[:, _UQ_ROPES0 + hd * MLA_ROPE:_UQ_ROPES0 + (hd + 1) * MLA_ROPE] * sin)
        qm_ref[0, hd, :, 0:MLA_NOPE] = (q_nope * q_scale).astype(BF16)
        qm_ref[0, hd, :, MLA_NOPE:MLA_QK] = (q_rope * q_scale).astype(BF16)
        kv0 = hd * (MLA_NOPE + MLA_V)
        km_ref[0, hd, :, 0:MLA_NOPE] = kv_all[:, kv0:kv0 + MLA_NOPE].astype(BF16)
        km_ref[0, hd, :, MLA_NOPE:MLA_QK] = k_rope
        vm_ref[0, hd] = kv_all[:, kv0 + MLA_NOPE:kv0 + MLA_NOPE + MLA_V].astype(BF16)


def _swap_halves(w):
    half = w.shape[-1] // 2
    return jnp.concatenate([w[..., half:], w[..., :half]], axis=-1)


def _mix_proj(x2d, mod3, gain, w_in, q_norm, kv_norm, w_uq, w_ukv, *, batch, seq, tm=512):
    m, d = x2d.shape
    tiles_per_seq = seq // tm
    w_in_ext = jnp.concatenate([w_in, _swap_halves(w_in[:, _KR0:_KRS0])], axis=1).astype(BF16)
    w_uq_h = w_uq.reshape(MLA_Q_RANK, MLA_HEADS, MLA_QK)
    uq_rope = w_uq_h[:, :, MLA_NOPE:]
    w_uq_r = jnp.concatenate([w_uq_h[:, :, :MLA_NOPE].reshape(MLA_Q_RANK, -1),
                              uq_rope.reshape(MLA_Q_RANK, -1),
                              _swap_halves(uq_rope).reshape(MLA_Q_RANK, -1)], axis=1).astype(BF16)
    w_ukv_b = w_ukv.astype(BF16)
    inv = ROPE_THETA ** (-jnp.arange(0, MLA_ROPE, 2, dtype=F32) / MLA_ROPE)
    ang = jnp.arange(seq, dtype=F32)[:, None] * inv[None, :]
    cos2 = jnp.concatenate([jnp.cos(ang), jnp.cos(ang)], axis=-1)
    sin2 = jnp.concatenate([-jnp.sin(ang), jnp.sin(ang)], axis=-1)

    def head_spec(width):
        return pl.BlockSpec((1, MLA_HEADS, tm, width),
                            lambda i: (i // tiles_per_seq, 0, i % tiles_per_seq, 0))

    def head_shape(width):
        return jax.ShapeDtypeStruct((batch, MLA_HEADS, seq, width), BF16)

    return pl.pallas_call(
        _mix_proj_kernel,
        grid=(m // tm,),
        in_specs=[pl.BlockSpec((tm, d), lambda i: (i, 0)),
                  pl.BlockSpec((1, N_MOD, d), lambda i: (i // tiles_per_seq, 0, 0)),
                  _resident((1, d)),
                  _resident((d, _PROJ_W)),
                  _resident((1, MLA_Q_RANK)), _resident((1, MLA_KV_RANK)),
                  _resident(w_uq_r.shape), _resident(w_ukv_b.shape),
                  pl.BlockSpec((tm, MLA_ROPE), lambda i: (i % tiles_per_seq, 0)),
                  pl.BlockSpec((tm, MLA_ROPE), lambda i: (i % tiles_per_seq, 0))],
        out_specs=[pl.BlockSpec((tm, SWA_HEADS * SWA_HEAD_DIM), lambda i: (i, 0)),
                   pl.BlockSpec((tm, 2 * SWA_KV_HEADS * SWA_HEAD_DIM), lambda i: (i, 0)),
                   head_spec(MLA_QK), head_spec(MLA_QK), head_spec(MLA_V)],
        out_shape=[jax.ShapeDtypeStruct((m, SWA_HEADS * SWA_HEAD_DIM), BF16),
                   jax.ShapeDtypeStruct((m, 2 * SWA_KV_HEADS * SWA_HEAD_DIM), BF16),
                   head_shape(MLA_QK), head_shape(MLA_QK), head_shape(MLA_V)],
        compiler_params=_params(1),
        name="mix_proj",
    )(x2d, mod3, gain.reshape(1, d), w_in_ext, q_norm.reshape(1, -1), kv_norm.reshape(1, -1),
      w_uq_r, w_ukv_b, cos2, sin2)


def _t5_bucket_table():
    qi = np.arange(WINDOW)[:, None]
    kj = np.arange(2 * WINDOW)[None, :]
    dist = qi + WINDOW - kj
    max_exact = NUM_BUCKETS // 2
    n = np.maximum(dist, 0)
    nf = np.maximum(n, 1).astype(np.float32)
    large = max_exact + (np.log(nf / np.float32(max_exact)) / np.float32(math.log(MAX_DISTANCE / max_exact))
                         * np.float32(NUM_BUCKETS - max_exact)).astype(np.int32)
    large = np.minimum(large, NUM_BUCKETS - 1)
    bucket = np.where(n < max_exact, n, large)
    band = (dist >= 0) & (dist < WINDOW)
    return np.where(band, bucket, -1).astype(np.int32)


def _swa_bias_kernel(rel_ref, bucket_ref, o_ref):
    bucket = bucket_ref[...]
    for hd in range(SWA_HEADS):
        acc = jnp.full(bucket.shape, MASK_VALUE, F32)
        for b in range(NUM_BUCKETS):
            acc = jnp.where(bucket == b, rel_ref[b, hd], acc)
        o_ref[hd] = acc


def _swa_bias(rel_bias):
    shape = (SWA_HEADS, WINDOW, 2 * WINDOW)
    return pl.pallas_call(
        _swa_bias_kernel,
        in_specs=[pl.BlockSpec(memory_space=pltpu.SMEM),
                  pl.BlockSpec((WINDOW, 2 * WINDOW), lambda: (0, 0))],
        out_specs=pl.BlockSpec(shape, lambda: (0, 0, 0)),
        out_shape=jax.ShapeDtypeStruct(shape, F32),
        name="swa_bias",
    )(rel_bias, jnp.asarray(_t5_bucket_table()))


def _swa_kernel(sink_ref, q_ref, kv_ref, bias_ref, o_ref):
    seq = q_ref.shape[0]
    kdim = SWA_KV_HEADS * SWA_HEAD_DIM

    def block(q0, k0, n_keys):
        outs = []
        for g in range(SWA_KV_HEADS):
            k = kv_ref[pl.ds(k0, n_keys), g * SWA_HEAD_DIM:(g + 1) * SWA_HEAD_DIM]
            v = kv_ref[pl.ds(k0, n_keys), kdim + g * SWA_HEAD_DIM:kdim + (g + 1) * SWA_HEAD_DIM]
            for j in range(SWA_GROUP):
                hd = g * SWA_GROUP + j
                q = q_ref[pl.ds(q0, WINDOW), hd * SWA_HEAD_DIM:(hd + 1) * SWA_HEAD_DIM]
                s = lax.dot_general(q, k, NT_DIMS, preferred_element_type=F32)
                s = s + bias_ref[hd, :, 2 * WINDOW - n_keys:2 * WINDOW]
                sink = sink_ref[hd]
                mx = jnp.maximum(jnp.max(s, axis=-1, keepdims=True), sink)
                p = jnp.exp(s - mx)
                denom = jnp.sum(p, axis=-1, keepdims=True) + jnp.exp(sink - mx)
                o = _dot(p.astype(BF16), v)
                outs.append(o * (1.0 / denom))
        o_ref[pl.ds(q0, WINDOW), :] = jnp.concatenate(outs, axis=-1).astype(BF16)

    block(0, 0, WINDOW)

    def body(n, carry):
        q0 = pl.multiple_of(n * WINDOW, WINDOW)
        block(q0, pl.multiple_of(q0 - WINDOW, WINDOW), 2 * WINDOW)
        return carry

    lax.fori_loop(1, seq // WINDOW, body, 0)


def _swa(qa, kva, bias, sinks, *, batch, seq):
    m, qw = qa.shape
    kvw = kva.shape[1]
    return pl.pallas_call(
        _swa_kernel,
        grid=(batch,),
        in_specs=[pl.BlockSpec(memory_space=pltpu.SMEM),
                  pl.BlockSpec((seq, qw), lambda b: (b, 0)),
                  pl.BlockSpec((seq, kvw), lambda b: (b, 0)),
                  _resident(bias.shape)],
        out_specs=pl.BlockSpec((seq, qw), lambda b: (b, 0)),
        out_shape=jax.ShapeDtypeStruct((m, qw), BF16),
        compiler_params=_params(1),
        name="swa",
    )(sinks, qa, kva, bias)


def _mla_kernel(q_ref, k_ref, v_ref, o_ref, *, tq):
    seq = q_ref.shape[2]
    row = lax.broadcasted_iota(jnp.int32, (tq, tq), 0)
    col = lax.broadcasted_iota(jnp.int32, (tq, tq), 1)
    causal = col <= row
    for i in range(seq // tq):
        lo, hi = i * tq, (i + 1) * tq
        q = q_ref[0, 0, lo:hi, :]
        s_diag = lax.dot_general(q, k_ref[0, 0, lo:hi, :], NT_DIMS, preferred_element_type=F32)
        s_diag = jnp.where(causal, s_diag, MASK_VALUE)
        mx = jnp.max(s_diag, axis=-1, keepdims=True)
        if i > 0:
            s_prev = lax.dot_general(q, k_ref[0, 0, 0:lo, :], NT_DIMS, preferred_element_type=F32)
            mx = jnp.maximum(mx, jnp.max(s_prev, axis=-1, keepdims=True))
        p_diag = jnp.exp(s_diag - mx)
        denom = jnp.sum(p_diag, axis=-1, keepdims=True)
        acc = _dot(p_diag.astype(BF16), v_ref[0, 0, lo:hi, :])
        if i > 0:
            p_prev = jnp.exp(s_prev - mx)
            denom = denom + jnp.sum(p_prev, axis=-1, keepdims=True)
            acc = acc + _dot(p_prev.astype(BF16), v_ref[0, 0, 0:lo, :])
        o_ref[lo:hi, :] = (acc * (1.0 / denom)).astype(BF16)


def _mla(qm, km, vm, *, tq=256):
    batch, heads, seq, qk = qm.shape
    dv = vm.shape[-1]
    return pl.pallas_call(
        functools.partial(_mla_kernel, tq=tq),
        grid=(batch, heads),
        in_specs=[pl.BlockSpec((1, 1, seq, qk), lambda b, h: (b, h, 0, 0)),
                  pl.BlockSpec((1, 1, seq, qk), lambda b, h: (b, h, 0, 0)),
                  pl.BlockSpec((1, 1, seq, dv), lambda b, h: (b, h, 0, 0))],
        out_specs=pl.BlockSpec((seq, dv), lambda b, h: (b, h)),
        out_shape=jax.ShapeDtypeStruct((batch * seq, heads * dv), BF16),
        compiler_params=_params(2),
        name="mla",
    )(qm, km, vm)


def _out_proj_kernel(x_ref, mod_ref, a_ref, b_ref, woa_ref, wob_ref, o_ref):
    gate = mod_ref[0, 5:6, :]
    y = _dot(a_ref[...], woa_ref[...]) + _dot(b_ref[...], wob_ref[...])
    o_ref[...] = x_ref[...] + gate * y


def _out_proj(x2d, mod3, out_a, out_b, w_o, *, seq, tm=1024):
    m, d = x2d.shape
    wa = out_a.shape[1]
    wb = out_b.shape[1]
    tiles_per_seq = seq // tm
    w_o_b = w_o.astype(BF16)
    return pl.pallas_call(
        _out_proj_kernel,
        grid=(m // tm,),
        in_specs=[pl.BlockSpec((tm, d), lambda i: (i, 0)),
                  pl.BlockSpec((1, N_MOD, d), lambda i: (i // tiles_per_seq, 0, 0)),
                  pl.BlockSpec((tm, wa), lambda i: (i, 0)),
                  pl.BlockSpec((tm, wb), lambda i: (i, 0)),
                  _resident((wa, d)), _resident((wb, d))],
        out_specs=pl.BlockSpec((tm, d), lambda i: (i, 0)),
        out_shape=jax.ShapeDtypeStruct((m, d), F32),
        compiler_params=_params(1),
        name="out_proj",
    )(x2d, mod3, out_a, out_b, w_o_b[:wa], w_o_b[wa:])


def kernel(x, c, w_mod, b_mod, norm_ffn1, ffn1_gate, ffn1_up, ffn1_down, norm_mix, w_in, q_norm, kv_norm, w_uq, w_ukv, sinks, w_o, norm_ffn2, ffn2_gate, ffn2_up, ffn2_down, rel_bias, norm_final):
    batch, seq, d = x.shape
    depth = w_mod.shape[0]
    x2d = x.reshape(batch * seq, d)
    bias = _swa_bias(rel_bias)
    for l in range(depth):
        mod3 = _mod(c, w_mod[l], b_mod[l]).reshape(batch, N_MOD, d)
        x2d = _ffn(x2d, mod3, norm_ffn1[l], ffn1_gate[l].astype(BF16), ffn1_up[l].astype(BF16),
                   ffn1_down[l].astype(BF16), mod_row=0, seq=seq)
        qa, kva, qm, km, vm = _mix_proj(x2d, mod3, norm_mix[l], w_in[l], q_norm[l], kv_norm[l],
                                        w_uq[l], w_ukv[l], batch=batch, seq=seq)
        out_a = _swa(qa, kva, bias, sinks[l], batch=batch, seq=seq)
        out_b = _mla(qm, km, vm)
        x2d = _out_proj(x2d, mod3, out_a, out_b, w_o[l], seq=seq)
        last = l == depth - 1
        x2d = _ffn(x2d, mod3, norm_ffn2[l], ffn2_gate[l].astype(BF16), ffn2_up[l].astype(BF16),
                   ffn2_down[l].astype(BF16), mod_row=6, seq=seq,
                   final_gain=norm_final if last else None)
    return x2d.reshape(batch, seq, d)
```

```python
import functools
import math

import jax
import jax.numpy as jnp
import numpy as np
from jax import lax
from jax.experimental import pallas as pl
from jax.experimental.pallas import tpu as pltpu

EPS = 1e-6
FFN_RES_WEIGHT = 0.5
N_MOD = 9

SWA_HEADS = 8
SWA_KV_HEADS = 2
SWA_HEAD_DIM = 64
SWA_GROUP = SWA_HEADS // SWA_KV_HEADS
WINDOW = 128

MLA_HEADS = 4
MLA_Q_RANK = 256
MLA_KV_RANK = 128
MLA_NOPE = 128
MLA_ROPE = 64
MLA_V = 128
MLA_QK = MLA_NOPE + MLA_ROPE
ROPE_THETA = 10000.0

NUM_BUCKETS = 32
MAX_DISTANCE = 128

SWA_ONES_ROWS = 16
LOG2E = math.log2(math.e)
MASK_VALUE = -1e30
V7X_VMEM_LIMIT_BYTES = 56 * 1024 * 1024

F32 = jnp.float32
BF16 = jnp.bfloat16
NT_DIMS = (((1,), (1,)), ((), ()))


def _params(n_grid_dims):
    return pltpu.CompilerParams(dimension_semantics=("arbitrary",) * n_grid_dims,
                                vmem_limit_bytes=V7X_VMEM_LIMIT_BYTES)


def _resident(shape):
    return pl.BlockSpec(shape, lambda *_: (0,) * len(shape), pipeline_mode=pl.Buffered(1))


def _rms_norm(x, gain):
    ms = jnp.mean(x * x, axis=-1, keepdims=True)
    return x * lax.rsqrt(ms + EPS) * gain


def _silu(x):
    return x * (1.0 / (1.0 + jnp.exp(-x)))


def _dot(a, b):
    return jnp.dot(a, b, preferred_element_type=F32)


def _mod_kernel(c_ref, w_ref, b_ref, o_ref):
    c_act = _silu(c_ref[...]).astype(BF16)
    o_ref[...] = _dot(c_act, w_ref[...].astype(BF16)) + b_ref[...]


def _mod(c, w_mod, b_mod, *, tn=1024):
    batch, d = c.shape
    n = w_mod.shape[1]
    return pl.pallas_call(
        _mod_kernel,
        grid=(n // tn,),
        in_specs=[pl.BlockSpec((batch, d), lambda j: (0, 0)),
                  pl.BlockSpec((d, tn), lambda j: (0, j)),
                  pl.BlockSpec((1, tn), lambda j: (0, j))],
        out_specs=pl.BlockSpec((batch, tn), lambda j: (0, j)),
        out_shape=jax.ShapeDtypeStruct((batch, n), F32),
        compiler_params=_params(1),
        name="mod",
    )(c, w_mod, b_mod.reshape(1, n))


def _ffn_kernel(*refs, mod_row, f_chunks, final_norm):
    if final_norm:
        x_ref, mod_ref, g_ref, wg_ref, wu_ref, wd_ref, gf_ref, o_ref, h_ref, a_ref = refs
    else:
        x_ref, mod_ref, g_ref, wg_ref, wu_ref, wd_ref, o_ref, h_ref, a_ref = refs
    x = x_ref[...]
    shift = mod_ref[0, mod_row:mod_row + 1, :]
    scale = mod_ref[0, mod_row + 1:mod_row + 2, :]
    gate = mod_ref[0, mod_row + 2:mod_row + 3, :]
    h_ref[...] = (_rms_norm(x, g_ref[...]) * (1.0 + scale) + shift).astype(BF16)
    f0 = 0
    for fc in f_chunks:
        g = _dot(h_ref[...], wg_ref[:, f0:f0 + fc])
        u = _dot(h_ref[...], wu_ref[:, f0:f0 + fc])
        a_ref[:, f0:f0 + fc] = (_silu(g) * u).astype(BF16)
        f0 += fc
    y = _dot(a_ref[...], wd_ref[...])
    out = x + FFN_RES_WEIGHT * gate * y
    if final_norm:
        out = _rms_norm(out, gf_ref[...])
    o_ref[...] = out


def _ffn(x2d, mod3, gain, wg, wu, wd, *, mod_row, seq, final_gain=None, tm=512,
         f_chunks=(1024, 1024, 768)):
    m, d = x2d.shape
    f = wg.shape[1]
    assert sum(f_chunks) == f and seq % tm == 0
    tiles_per_seq = seq // tm
    final_norm = final_gain is not None
    in_specs = [pl.BlockSpec((tm, d), lambda i: (i, 0)),
                pl.BlockSpec((1, N_MOD, d), lambda i: (i // tiles_per_seq, 0, 0)),
                _resident((1, d)),
                _resident((d, f)), _resident((d, f)), _resident((f, d))]
    args = [x2d, mod3, gain.reshape(1, d), wg, wu, wd]
    if final_norm:
        in_specs.append(_resident((1, d)))
        args.append(final_gain.reshape(1, d))
    return pl.pallas_call(
        functools.partial(_ffn_kernel, mod_row=mod_row, f_chunks=f_chunks, final_norm=final_norm),
        grid=(m // tm,),
        in_specs=in_specs,
        out_specs=pl.BlockSpec((tm, d), lambda i: (i, 0)),
        out_shape=jax.ShapeDtypeStruct((m, d), F32),
        scratch_shapes=[pltpu.VMEM((tm, d), BF16), pltpu.VMEM((tm, f), BF16)],
        compiler_params=_params(1),
        name="ffn_final" if final_norm else "ffn",
    )(*args)


_QA0, _KVA0, _QLAT0, _KVLAT0, _KR0, _KRS0, _PROJ_W = 0, 512, 768, 1024, 1152, 1216, 1280
_UQ_ROPE0 = MLA_HEADS * MLA_NOPE
_UQ_ROPES0 = _UQ_ROPE0 + MLA_HEADS * MLA_ROPE


def _mix_proj_kernel(x_ref, mod_ref, g_ref, win_ref, qn_ref, kvn_ref, wuq_ref, wukv_ref, cos_ref, sin_ref,
                     qa_ref, kva_ref, qm_ref, km_ref, vm_ref):
    x = x_ref[...]
    shift = mod_ref[0, 3:4, :]
    scale = mod_ref[0, 4:5, :]
    h = (_rms_norm(x, g_ref[...]) * (1.0 + scale) + shift).astype(BF16)
    proj = _dot(h, win_ref[...])
    for hd in range(SWA_HEADS):
        qa_ref[0, hd] = (proj[:, _QA0 + hd * SWA_HEAD_DIM:_QA0 + (hd + 1) * SWA_HEAD_DIM]
                         * (SWA_HEAD_DIM ** -0.5 * LOG2E)).astype(BF16)
    kva_ref[...] = proj[:, _KVA0:_QLAT0].astype(BF16)

    q_lat = _rms_norm(proj[:, _QLAT0:_KVLAT0], qn_ref[...]).astype(BF16)
    kv_lat = _rms_norm(proj[:, _KVLAT0:_KR0], kvn_ref[...]).astype(BF16)
    q_all = _dot(q_lat, wuq_ref[...])
    kv_all = _dot(kv_lat, wukv_ref[...])

    cos = cos_ref[...]
    sin = sin_ref[...]
    k_rope = (proj[:, _KR0:_KRS0] * cos + proj[:, _KRS0:_PROJ_W] * sin).astype(BF16)
    q_scale = MLA_QK ** -0.5 * LOG2E
    for hd in range(MLA_HEADS):
        q_nope = q_all[:, hd * MLA_NOPE:(hd + 1) * MLA_NOPE]
        q_rope = (q_all[:, _UQ_ROPE0 + hd * MLA_ROPE:_UQ_ROPE0 + (hd + 1) * MLA_ROPE] * cos
                  + q_all[:, _UQ_ROPES0 + hd * MLA_ROPE:_UQ_ROPES0 + (hd + 1) * MLA_ROPE] * sin)
        qm_ref[0, hd, :, 0:MLA_NOPE] = (q_nope * q_scale).astype(BF16)
        qm_ref[0, hd, :, MLA_NOPE:MLA_QK] = (q_rope * q_scale).astype(BF16)
        kv0 = hd * (MLA_NOPE + MLA_V)
        km_ref[0, hd, :, 0:MLA_NOPE] = kv_all[:, kv0:kv0 + MLA_NOPE].astype(BF16)
        km_ref[0, hd, :, MLA_NOPE:MLA_QK] = k_rope
        vm_ref[0, hd] = kv_all[:, kv0 + MLA_NOPE:kv0 + MLA_NOPE + MLA_V].astype(BF16)


def _swap_halves(w):
    half = w.shape[-1] // 2
    return jnp.concatenate([w[..., half:], w[..., :half]], axis=-1)


def _mix_proj(x2d, mod3, gain, w_in, q_norm, kv_norm, w_uq, w_ukv, *, batch, seq, tm=512):
    m, d = x2d.shape
    tiles_per_seq = seq // tm
    w_in_ext = jnp.concatenate([w_in, _swap_halves(w_in[:, _KR0:_KRS0])], axis=1).astype(BF16)
    w_uq_h = w_uq.reshape(MLA_Q_RANK, MLA_HEADS, MLA_QK)
    uq_rope = w_uq_h[:, :, MLA_NOPE:]
    w_uq_r = jnp.concatenate([w_uq_h[:, :, :MLA_NOPE].reshape(MLA_Q_RANK, -1),
                              uq_rope.reshape(MLA_Q_RANK, -1),
                              _swap_halves(uq_rope).reshape(MLA_Q_RANK, -1)], axis=1).astype(BF16)
    w_ukv_b = w_ukv.astype(BF16)
    inv = ROPE_THETA ** (-jnp.arange(0, MLA_ROPE, 2, dtype=F32) / MLA_ROPE)
    ang = jnp.arange(seq, dtype=F32)[:, None] * inv[None, :]
    cos2 = jnp.concatenate([jnp.cos(ang), jnp.cos(ang)], axis=-1)
    sin2 = jnp.concatenate([-jnp.sin(ang), jnp.sin(ang)], axis=-1)

    def head_spec(width):
        return pl.BlockSpec((1, MLA_HEADS, tm, width),
                            lambda i: (i // tiles_per_seq, 0, i % tiles_per_seq, 0))

    def head_shape(width):
        return jax.ShapeDtypeStruct((batch, MLA_HEADS, seq, width), BF16)

    return pl.pallas_call(
        _mix_proj_kernel,
        grid=(m // tm,),
        in_specs=[pl.BlockSpec((tm, d), lambda i: (i, 0)),
                  pl.BlockSpec((1, N_MOD, d), lambda i: (i // tiles_per_seq, 0, 0)),
                  _resident((1, d)),
                  _resident((d, _PROJ_W)),
                  _resident((1, MLA_Q_RANK)), _resident((1, MLA_KV_RANK)),
                  _resident(w_uq_r.shape), _resident(w_ukv_b.shape),
                  pl.BlockSpec((tm, MLA_ROPE), lambda i: (i % tiles_per_seq, 0)),
                  pl.BlockSpec((tm, MLA_ROPE), lambda i: (i % tiles_per_seq, 0))],
        out_specs=[pl.BlockSpec((1, SWA_HEADS, tm, SWA_HEAD_DIM),
                                lambda i: (i // tiles_per_seq, 0, i % tiles_per_seq, 0)),
                   pl.BlockSpec((tm, 2 * SWA_KV_HEADS * SWA_HEAD_DIM), lambda i: (i, 0)),
                   head_spec(MLA_QK), head_spec(MLA_QK), head_spec(MLA_V)],
        out_shape=[jax.ShapeDtypeStruct((batch, SWA_HEADS, seq, SWA_HEAD_DIM), BF16),
                   jax.ShapeDtypeStruct((m, 2 * SWA_KV_HEADS * SWA_HEAD_DIM), BF16),
                   head_shape(MLA_QK), head_shape(MLA_QK), head_shape(MLA_V)],
        compiler_params=_params(1),
        name="mix_proj",
    )(x2d, mod3, gain.reshape(1, d), w_in_ext, q_norm.reshape(1, -1), kv_norm.reshape(1, -1),
      w_uq_r, w_ukv_b, cos2, sin2)


def _t5_bucket_table():
    qi = np.arange(WINDOW)[:, None]
    kj = np.arange(2 * WINDOW)[None, :]
    dist = qi + WINDOW - kj
    max_exact = NUM_BUCKETS // 2
    n = np.maximum(dist, 0)
    nf = np.maximum(n, 1).astype(np.float32)
    large = max_exact + (np.log(nf / np.float32(max_exact)) / np.float32(math.log(MAX_DISTANCE / max_exact))
                         * np.float32(NUM_BUCKETS - max_exact)).astype(np.int32)
    large = np.minimum(large, NUM_BUCKETS - 1)
    bucket = np.where(n < max_exact, n, large)
    band = (dist >= 0) & (dist < WINDOW)
    return np.where(band, bucket, -1).astype(np.int32)


def _swa_bias_kernel(rel_ref, bucket_ref, o_ref):
    bucket = bucket_ref[...]
    first_block_ok = lax.broadcasted_iota(jnp.int32, bucket.shape, 0) >= WINDOW
    for hd in range(SWA_HEADS):
        acc = jnp.full(bucket.shape, MASK_VALUE, F32)
        for b in range(NUM_BUCKETS):
            acc = jnp.where(bucket == b, rel_ref[b, hd] * LOG2E, acc)
        g, j = divmod(hd, SWA_GROUP)
        o_ref[0, g, :, j * WINDOW:(j + 1) * WINDOW] = jnp.where(first_block_ok, acc, MASK_VALUE)
        o_ref[1, g, :, j * WINDOW:(j + 1) * WINDOW] = acc


def _swa_bias(rel_bias):
    shape = (2, SWA_KV_HEADS, 2 * WINDOW, SWA_GROUP * WINDOW)
    return pl.pallas_call(
        _swa_bias_kernel,
        in_specs=[pl.BlockSpec(memory_space=pltpu.SMEM),
                  pl.BlockSpec((2 * WINDOW, WINDOW), lambda: (0, 0))],
        out_specs=pl.BlockSpec(shape, lambda: (0, 0, 0, 0)),
        out_shape=jax.ShapeDtypeStruct(shape, F32),
        name="swa_bias",
    )(rel_bias, jnp.asarray(_t5_bucket_table().T))


def _swa_kernel(sink_ref, q_ref, kv_ref, bias_ref, o_ref, kp_ref, vt_ref, *, blocks_per_iter):
    seq = kv_ref.shape[0]
    n_blocks = seq // WINDOW
    kdim = SWA_KV_HEADS * SWA_HEAD_DIM
    kp_ref[0:WINDOW, :] = jnp.zeros((WINDOW, kdim), BF16)
    kp_ref[WINDOW:, :] = kv_ref[:, 0:kdim]
    ones_rows = jnp.ones((SWA_ONES_ROWS, WINDOW), BF16)
    for g in range(SWA_KV_HEADS):
        vt_ref[0, g] = jnp.concatenate([jnp.zeros((SWA_HEAD_DIM, WINDOW), BF16), ones_rows], axis=0)
    for n in range(n_blocks):
        for g in range(SWA_KV_HEADS):
            v_blk = kv_ref[n * WINDOW:(n + 1) * WINDOW, kdim + g * SWA_HEAD_DIM:kdim + (g + 1) * SWA_HEAD_DIM]
            vt_ref[n + 1, g] = jnp.concatenate([v_blk.astype(F32).T.astype(BF16), ones_rows], axis=0)
    sink_rows = [[jnp.full((1, WINDOW), sink_ref[g * SWA_GROUP + j] * LOG2E, F32) for j in range(SWA_GROUP)]
                 for g in range(SWA_KV_HEADS)]

    def body(it, carry):
        units = [(b, g) for b in range(blocks_per_iter) for g in range(SWA_KV_HEADS)]
        blocks = [it * blocks_per_iter + b for b in range(blocks_per_iter)]
        q_starts = [pl.multiple_of(n * WINDOW, WINDOW) for n in blocks]
        tables = [jnp.minimum(n, 1) for n in blocks]
        sts, pts = {}, {}

        def scores(u):
            b, g = units[u]
            k = kp_ref[pl.ds(q_starts[b], 2 * WINDOW), g * SWA_HEAD_DIM:(g + 1) * SWA_HEAD_DIM]
            sts[u] = []
            for j in range(0, SWA_GROUP, 2):
                hd = g * SWA_GROUP + j
                q = jnp.concatenate([q_ref[0, hd, pl.ds(q_starts[b], WINDOW), :],
                                     q_ref[0, hd + 1, pl.ds(q_starts[b], WINDOW), :]], axis=0)
                st = lax.dot_general(k, q, NT_DIMS, preferred_element_type=F32)
                sts[u] += [st[:, 0:WINDOW], st[:, WINDOW:2 * WINDOW]]

        def softmax(u):
            b, g = units[u]
            pts[u] = []
            for j in range(SWA_GROUP):
                st = sts[u][j] + bias_ref[tables[b], g, :, j * WINDOW:(j + 1) * WINDOW]
                mx = jnp.maximum(jnp.max(st, axis=0, keepdims=True), sink_rows[g][j])
                pts[u].append((jnp.exp2(st - mx).astype(BF16), jnp.exp2(sink_rows[g][j] - mx)))

        def values(u):
            b, g = units[u]
            vt = jnp.concatenate([vt_ref[blocks[b], g], vt_ref[blocks[b] + 1, g]], axis=1)
            for j in range(0, SWA_GROUP, 2):
                pt = jnp.concatenate([pts[u][j][0], pts[u][j + 1][0]], axis=1)
                ot = _dot(vt, pt)
                halves = []
                for jj in range(2):
                    lanes = slice(jj * WINDOW, (jj + 1) * WINDOW)
                    denom = ot[SWA_HEAD_DIM:SWA_HEAD_DIM + 1, lanes] + pts[u][j + jj][1]
                    halves.append(ot[0:SWA_HEAD_DIM, lanes] * (1.0 / denom))
                pair = jnp.concatenate(halves, axis=0).T
                lane0 = (g * SWA_GROUP + j) * SWA_HEAD_DIM
                o_ref[pl.ds(q_starts[b], WINDOW), lane0:lane0 + 2 * SWA_HEAD_DIM] = pair.astype(BF16)

        for t in range(len(units) + 2):
            if t < len(units):
                scores(t)
            if 0 <= t - 1 < len(units):
                softmax(t - 1)
            if 0 <= t - 2 < len(units):
                values(t - 2)
        return carry

    lax.fori_loop(0, n_blocks // blocks_per_iter, body, 0)


def _swa(qa, kva, bias, sinks, *, batch, seq, blocks_per_iter=8):
    kvw = kva.shape[1]
    qw = SWA_HEADS * SWA_HEAD_DIM
    assert (seq // WINDOW) % blocks_per_iter == 0
    return pl.pallas_call(
        functools.partial(_swa_kernel, blocks_per_iter=blocks_per_iter),
        grid=(batch,),
        in_specs=[pl.BlockSpec(memory_space=pltpu.SMEM),
                  pl.BlockSpec((1, SWA_HEADS, seq, SWA_HEAD_DIM), lambda b: (b, 0, 0, 0)),
                  pl.BlockSpec((seq, kvw), lambda b: (b, 0)),
                  _resident(bias.shape)],
        out_specs=pl.BlockSpec((seq, qw), lambda b: (b, 0)),
        out_shape=jax.ShapeDtypeStruct((batch * seq, qw), BF16),
        scratch_shapes=[pltpu.VMEM((WINDOW + seq, SWA_KV_HEADS * SWA_HEAD_DIM), BF16),
                        pltpu.VMEM((1 + seq // WINDOW, SWA_KV_HEADS, SWA_HEAD_DIM + SWA_ONES_ROWS, WINDOW), BF16)],
        compiler_params=_params(1),
        name="swa",
    )(sinks, qa, kva, bias)


def _mla_kernel(q_ref, k_ref, v_ref, o_ref, *, tq):
    seq = q_ref.shape[2]
    row = lax.broadcasted_iota(jnp.int32, (tq, tq), 0)
    col = lax.broadcasted_iota(jnp.int32, (tq, tq), 1)
    causal = col <= row
    for i in range(seq // tq):
        lo, hi = i * tq, (i + 1) * tq
        q = q_ref[0, 0, lo:hi, :]
        s_diag = lax.dot_general(q, k_ref[0, 0, lo:hi, :], NT_DIMS, preferred_element_type=F32)
        s_diag = jnp.where(causal, s_diag, MASK_VALUE)
        mx = jnp.max(s_diag, axis=-1, keepdims=True)
        if i > 0:
            s_prev = lax.dot_general(q, k_ref[0, 0, 0:lo, :], NT_DIMS, preferred_element_type=F32)
            mx = jnp.maximum(mx, jnp.max(s_prev, axis=-1, keepdims=True))
        p_diag = jnp.exp2(s_diag - mx)
        denom = jnp.sum(p_diag, axis=-1, keepdims=True)
        acc = _dot(p_diag.astype(BF16), v_ref[0, 0, lo:hi, :])
        if i > 0:
            p_prev = jnp.exp2(s_prev - mx)
            denom = denom + jnp.sum(p_prev, axis=-1, keepdims=True)
            acc = acc + _dot(p_prev.astype(BF16), v_ref[0, 0, 0:lo, :])
        o_ref[lo:hi, :] = (acc * (1.0 / denom)).astype(BF16)


def _mla(qm, km, vm, *, tq=256):
    batch, heads, seq, qk = qm.shape
    dv = vm.shape[-1]
    return pl.pallas_call(
        functools.partial(_mla_kernel, tq=tq),
        grid=(batch, heads),
        in_specs=[pl.BlockSpec((1, 1, seq, qk), lambda b, h: (b, h, 0, 0)),
                  pl.BlockSpec((1, 1, seq, qk), lambda b, h: (b, h, 0, 0)),
                  pl.BlockSpec((1, 1, seq, dv), lambda b, h: (b, h, 0, 0))],
        out_specs=pl.BlockSpec((seq, dv), lambda b, h: (b, h)),
        out_shape=jax.ShapeDtypeStruct((batch * seq, heads * dv), BF16),
        compiler_params=_params(2),
        name="mla",
    )(qm, km, vm)


def _out_proj_kernel(x_ref, mod_ref, a_ref, b_ref, woa_ref, wob_ref, o_ref):
    gate = mod_ref[0, 5:6, :]
    y = _dot(a_ref[...], woa_ref[...]) + _dot(b_ref[...], wob_ref[...])
    o_ref[...] = x_ref[...] + gate * y


def _out_proj(x2d, mod3, out_a, out_b, w_o, *, seq, tm=1024):
    m, d = x2d.shape
    wa = out_a.shape[1]
    wb = out_b.shape[1]
    tiles_per_seq = seq // tm
    w_o_b = w_o.astype(BF16)
    return pl.pallas_call(
        _out_proj_kernel,
        grid=(m // tm,),
        in_specs=[pl.BlockSpec((tm, d), lambda i: (i, 0)),
                  pl.BlockSpec((1, N_MOD, d), lambda i: (i // tiles_per_seq, 0, 0)),
                  pl.BlockSpec((tm, wa), lambda i: (i, 0)),
                  pl.BlockSpec((tm, wb), lambda i: (i, 0)),
                  _resident((wa, d)), _resident((wb, d))],
        out_specs=pl.BlockSpec((tm, d), lambda i: (i, 0)),
        out_shape=jax.ShapeDtypeStruct((m, d), F32),
        compiler_params=_params(1),
        name="out_proj",
    )(x2d, mod3, out_a, out_b, w_o_b[:wa], w_o_b[wa:])


def kernel(x, c, w_mod, b_mod, norm_ffn1, ffn1_gate, ffn1_up, ffn1_down, norm_mix, w_in, q_norm, kv_norm, w_uq, w_ukv, sinks, w_o, norm_ffn2, ffn2_gate, ffn2_up, ffn2_down, rel_bias, norm_final):
    batch, seq, d = x.shape
    depth = w_mod.shape[0]
    x2d = x.reshape(batch * seq, d)
    bias = _swa_bias(rel_bias)
    for l in range(depth):
        mod3 = _mod(c, w_mod[l], b_mod[l]).reshape(batch, N_MOD, d)
        x2d = _ffn(x2d, mod3, norm_ffn1[l], ffn1_gate[l].astype(BF16), ffn1_up[l].astype(BF16),
                   ffn1_down[l].astype(BF16), mod_row=0, seq=seq)
        qa, kva, qm, km, vm = _mix_proj(x2d, mod3, norm_mix[l], w_in[l], q_norm[l], kv_norm[l],
                                        w_uq[l], w_ukv[l], batch=batch, seq=seq)
        out_a = _swa(qa, kva, bias, sinks[l], batch=batch, seq=seq)
        out_b = _mla(qm, km, vm)
        x2d = _out_proj(x2d, mod3, out_a, out_b, w_o[l], seq=seq)
        last = l == depth - 1
        x2d = _ffn(x2d, mod3, norm_ffn2[l], ffn2_gate[l].astype(BF16), ffn2_up[l].astype(BF16),
                   ffn2_down[l].astype(BF16), mod_row=6, seq=seq,
                   final_gain=norm_final if last else None)
    return x2d.reshape(batch, seq, d)
```

```python
import functools
import math

import jax
import jax.numpy as jnp
import numpy as np
from jax import lax
from jax.experimental import pallas as pl
from jax.experimental.pallas import tpu as pltpu

EPS = 1e-6
FFN_RES_WEIGHT = 0.5
N_MOD = 9

SWA_HEADS = 8
SWA_KV_HEADS = 2
SWA_HEAD_DIM = 64
SWA_GROUP = SWA_HEADS // SWA_KV_HEADS
WINDOW = 128

MLA_HEADS = 4
MLA_Q_RANK = 256
MLA_KV_RANK = 128
MLA_NOPE = 128
MLA_ROPE = 64
MLA_V = 128
MLA_QK = MLA_NOPE + MLA_ROPE
ROPE_THETA = 10000.0

NUM_BUCKETS = 32
MAX_DISTANCE = 128

SWA_ONES_ROWS = 16
MLA_ONES_ROWS = 16
LOG2E = math.log2(math.e)
MASK_VALUE = -1e30
V7X_VMEM_LIMIT_BYTES = 56 * 1024 * 1024

F32 = jnp.float32
BF16 = jnp.bfloat16
NT_DIMS = (((1,), (1,)), ((), ()))


def _params(n_grid_dims):
    return pltpu.CompilerParams(dimension_semantics=("arbitrary",) * n_grid_dims,
                                vmem_limit_bytes=V7X_VMEM_LIMIT_BYTES)


def _resident(shape):
    return pl.BlockSpec(shape, lambda *_: (0,) * len(shape), pipeline_mode=pl.Buffered(1))


def _rms_norm(x, gain):
    ms = jnp.mean(x * x, axis=-1, keepdims=True)
    return x * lax.rsqrt(ms + EPS) * gain


def _silu(x):
    return x * (1.0 / (1.0 + jnp.exp(-x)))


def _dot(a, b):
    return jnp.dot(a, b, preferred_element_type=F32)


def _mod_kernel(c_ref, w_ref, b_ref, o_ref):
    c_act = _silu(c_ref[...]).astype(BF16)
    o_ref[...] = _dot(c_act, w_ref[...].astype(BF16)) + b_ref[...]


def _mod(c, w_mod, b_mod, *, tn=1024):
    batch, d = c.shape
    n = w_mod.shape[1]
    return pl.pallas_call(
        _mod_kernel,
        grid=(n // tn,),
        in_specs=[pl.BlockSpec((batch, d), lambda j: (0, 0)),
                  pl.BlockSpec((d, tn), lambda j: (0, j)),
                  pl.BlockSpec((1, tn), lambda j: (0, j))],
        out_specs=pl.BlockSpec((batch, tn), lambda j: (0, j)),
        out_shape=jax.ShapeDtypeStruct((batch, n), F32),
        compiler_params=_params(1),
        name="mod",
    )(c, w_mod, b_mod.reshape(1, n))


def _ffn_kernel(*refs, mod_row, f_chunks, mix_residual, final_norm):
    refs = list(refs)
    x_ref, mod_ref = refs[:2]
    del refs[:2]
    if mix_residual:
        oa_ref, ob_ref, woa_ref, wob_ref = refs[:4]
        del refs[:4]
    g_ref, wg_ref, wu_ref, wd_ref = refs[:4]
    del refs[:4]
    if final_norm:
        gf_ref = refs.pop(0)
    o_ref, h_ref, a_ref = refs
    x = x_ref[...]
    if mix_residual:
        mix = _dot(oa_ref[...], woa_ref[...]) + _dot(ob_ref[...], wob_ref[...])
        x = x + mod_ref[0, mod_row - 1:mod_row, :] * mix
    shift = mod_ref[0, mod_row:mod_row + 1, :]
    scale = mod_ref[0, mod_row + 1:mod_row + 2, :]
    gate = mod_ref[0, mod_row + 2:mod_row + 3, :]
    h_ref[...] = (_rms_norm(x, g_ref[...]) * (1.0 + scale) + shift).astype(BF16)
    f0 = 0
    for fc in f_chunks:
        g = _dot(h_ref[...], wg_ref[:, f0:f0 + fc])
        u = _dot(h_ref[...], wu_ref[:, f0:f0 + fc])
        a_ref[:, f0:f0 + fc] = (_silu(g) * u).astype(BF16)
        f0 += fc
    y = _dot(a_ref[...], wd_ref[...])
    out = x + FFN_RES_WEIGHT * gate * y
    if final_norm:
        out = _rms_norm(out, gf_ref[...])
    o_ref[...] = out


def _ffn(x2d, mod3, gain, wg, wu, wd, *, mod_row, seq, mix=None, final_gain=None, tm=512,
         f_chunks=(1024, 1024, 768)):
    m, d = x2d.shape
    f = wg.shape[1]
    assert sum(f_chunks) == f and seq % tm == 0
    tiles_per_seq = seq // tm
    final_norm = final_gain is not None
    in_specs = [pl.BlockSpec((tm, d), lambda i: (i, 0)),
                pl.BlockSpec((1, N_MOD, d), lambda i: (i // tiles_per_seq, 0, 0))]
    args = [x2d, mod3]
    if mix is not None:
        out_a, out_b, w_o = mix
        wa, wb = out_a.shape[1], out_b.shape[1]
        in_specs += [pl.BlockSpec((tm, wa), lambda i: (i, 0)), pl.BlockSpec((tm, wb), lambda i: (i, 0)),
                     _resident((wa, d)), _resident((wb, d))]
        args += [out_a, out_b, w_o[:wa], w_o[wa:]]
    in_specs += [_resident((1, d)), _resident((d, f)), _resident((d, f)), _resident((f, d))]
    args += [gain.reshape(1, d), wg, wu, wd]
    if final_norm:
        in_specs.append(_resident((1, d)))
        args.append(final_gain.reshape(1, d))
    return pl.pallas_call(
        functools.partial(_ffn_kernel, mod_row=mod_row, f_chunks=f_chunks, mix_residual=mix is not None,
                          final_norm=final_norm),
        grid=(m // tm,),
        in_specs=in_specs,
        out_specs=pl.BlockSpec((tm, d), lambda i: (i, 0)),
        out_shape=jax.ShapeDtypeStruct((m, d), F32),
        scratch_shapes=[pltpu.VMEM((tm, d), BF16), pltpu.VMEM((tm, f), BF16)],
        compiler_params=_params(1),
        name="ffn_final" if final_norm else "ffn",
    )(*args)


_QA0, _KVA0, _QLAT0, _KVLAT0, _KR0, _KRS0, _PROJ_W = 0, 512, 768, 1024, 1152, 1216, 1280
_UQ_ROPE0 = MLA_HEADS * MLA_NOPE
_UQ_ROPES0 = _UQ_ROPE0 + MLA_HEADS * MLA_ROPE


def _mix_proj_kernel(x_ref, mod_ref, g_ref, win_ref, qn_ref, kvn_ref, wuq_ref, wukv_ref, cos_ref, sin_ref,
                     qa_ref, kva_ref, qm_ref, km_ref, vm_ref):
    x = x_ref[...]
    shift = mod_ref[0, 3:4, :]
    scale = mod_ref[0, 4:5, :]
    h = (_rms_norm(x, g_ref[...]) * (1.0 + scale) + shift).astype(BF16)
    proj = _dot(h, win_ref[...])
    for hd in range(SWA_HEADS):
        qa_ref[0, hd] = (proj[:, _QA0 + hd * SWA_HEAD_DIM:_QA0 + (hd + 1) * SWA_HEAD_DIM]
                         * (SWA_HEAD_DIM ** -0.5 * LOG2E)).astype(BF16)
    kva_ref[...] = proj[:, _KVA0:_QLAT0].astype(BF16)

    q_lat = _rms_norm(proj[:, _QLAT0:_KVLAT0], qn_ref[...]).astype(BF16)
    kv_lat = _rms_norm(proj[:, _KVLAT0:_KR0], kvn_ref[...]).astype(BF16)
    q_all = _dot(q_lat, wuq_ref[...])
    kv_all = _dot(kv_lat, wukv_ref[...])

    cos = cos_ref[...]
    sin = sin_ref[...]
    k_rope = (proj[:, _KR0:_KRS0] * cos + proj[:, _KRS0:_PROJ_W] * sin).astype(BF16)
    q_scale = MLA_QK ** -0.5 * LOG2E
    for hd in range(MLA_HEADS):
        q_nope = q_all[:, hd * MLA_NOPE:(hd + 1) * MLA_NOPE]
        q_rope = (q_all[:, _UQ_ROPE0 + hd * MLA_ROPE:_UQ_ROPE0 + (hd + 1) * MLA_ROPE] * cos
                  + q_all[:, _UQ_ROPES0 + hd * MLA_ROPE:_UQ_ROPES0 + (hd + 1) * MLA_ROPE] * sin)
        qm_ref[0, hd, :, 0:MLA_NOPE] = (q_nope * q_scale).astype(BF16)
        qm_ref[0, hd, :, MLA_NOPE:MLA_QK] = (q_rope * q_scale).astype(BF16)
        kv0 = hd * (MLA_NOPE + MLA_V)
        km_ref[0, hd, :, 0:MLA_NOPE] = kv_all[:, kv0:kv0 + MLA_NOPE].astype(BF16)
        km_ref[0, hd, :, MLA_NOPE:MLA_QK] = k_rope
        vm_ref[0, hd] = kv_all[:, kv0 + MLA_NOPE:kv0 + MLA_NOPE + MLA_V].astype(BF16)


def _swap_halves(w):
    half = w.shape[-1] // 2
    return jnp.concatenate([w[..., half:], w[..., :half]], axis=-1)


def _mix_proj(x2d, mod3, gain, w_in, q_norm, kv_norm, w_uq, w_ukv, *, batch, seq, tm=512):
    m, d = x2d.shape
    tiles_per_seq = seq // tm
    w_in_ext = jnp.concatenate([w_in, _swap_halves(w_in[:, _KR0:_KRS0])], axis=1).astype(BF16)
    w_uq_h = w_uq.reshape(MLA_Q_RANK, MLA_HEADS, MLA_QK)
    uq_rope = w_uq_h[:, :, MLA_NOPE:]
    w_uq_r = jnp.concatenate([w_uq_h[:, :, :MLA_NOPE].reshape(MLA_Q_RANK, -1),
                              uq_rope.reshape(MLA_Q_RANK, -1),
                              _swap_halves(uq_rope).reshape(MLA_Q_RANK, -1)], axis=1).astype(BF16)
    w_ukv_b = w_ukv.astype(BF16)
    inv = ROPE_THETA ** (-jnp.arange(0, MLA_ROPE, 2, dtype=F32) / MLA_ROPE)
    ang = jnp.arange(seq, dtype=F32)[:, None] * inv[None, :]
    cos2 = jnp.concatenate([jnp.cos(ang), jnp.cos(ang)], axis=-1)
    sin2 = jnp.concatenate([-jnp.sin(ang), jnp.sin(ang)], axis=-1)

    def head_spec(width):
        return pl.BlockSpec((1, MLA_HEADS, tm, width),
                            lambda i: (i // tiles_per_seq, 0, i % tiles_per_seq, 0))

    def head_shape(width):
        return jax.ShapeDtypeStruct((batch, MLA_HEADS, seq, width), BF16)

    return pl.pallas_call(
        _mix_proj_kernel,
        grid=(m // tm,),
        in_specs=[pl.BlockSpec((tm, d), lambda i: (i, 0)),
                  pl.BlockSpec((1, N_MOD, d), lambda i: (i // tiles_per_seq, 0, 0)),
                  _resident((1, d)),
                  _resident((d, _PROJ_W)),
                  _resident((1, MLA_Q_RANK)), _resident((1, MLA_KV_RANK)),
                  _resident(w_uq_r.shape), _resident(w_ukv_b.shape),
                  pl.BlockSpec((tm, MLA_ROPE), lambda i: (i % tiles_per_seq, 0)),
                  pl.BlockSpec((tm, MLA_ROPE), lambda i: (i % tiles_per_seq, 0))],
        out_specs=[pl.BlockSpec((1, SWA_HEADS, tm, SWA_HEAD_DIM),
                                lambda i: (i // tiles_per_seq, 0, i % tiles_per_seq, 0)),
                   pl.BlockSpec((tm, 2 * SWA_KV_HEADS * SWA_HEAD_DIM), lambda i: (i, 0)),
                   head_spec(MLA_QK), head_spec(MLA_QK), head_spec(MLA_V)],
        out_shape=[jax.ShapeDtypeStruct((batch, SWA_HEADS, seq, SWA_HEAD_DIM), BF16),
                   jax.ShapeDtypeStruct((m, 2 * SWA_KV_HEADS * SWA_HEAD_DIM), BF16),
                   head_shape(MLA_QK), head_shape(MLA_QK), head_shape(MLA_V)],
        compiler_params=_params(1),
        name="mix_proj",
    )(x2d, mod3, gain.reshape(1, d), w_in_ext, q_norm.reshape(1, -1), kv_norm.reshape(1, -1),
      w_uq_r, w_ukv_b, cos2, sin2)


def _t5_bucket_table():
    qi = np.arange(WINDOW)[:, None]
    kj = np.arange(2 * WINDOW)[None, :]
    dist = qi + WINDOW - kj
    max_exact = NUM_BUCKETS // 2
    n = np.maximum(dist, 0)
    nf = np.maximum(n, 1).astype(np.float32)
    large = max_exact + (np.log(nf / np.float32(max_exact)) / np.float32(math.log(MAX_DISTANCE / max_exact))
                         * np.float32(NUM_BUCKETS - max_exact)).astype(np.int32)
    large = np.minimum(large, NUM_BUCKETS - 1)
    bucket = np.where(n < max_exact, n, large)
    band = (dist >= 0) & (dist < WINDOW)
    return np.where(band, bucket, -1).astype(np.int32)


def _swa_bias_kernel(rel_ref, bucket_ref, o_ref):
    bucket = bucket_ref[...]
    first_block_ok = lax.broadcasted_iota(jnp.int32, bucket.shape, 0) >= WINDOW
    for hd in range(SWA_HEADS):
        acc = jnp.full(bucket.shape, MASK_VALUE, F32)
        for b in range(NUM_BUCKETS):
            acc = jnp.where(bucket == b, rel_ref[b, hd] * LOG2E, acc)
        g, j = divmod(hd, SWA_GROUP)
        o_ref[0, g, :, j * WINDOW:(j + 1) * WINDOW] = jnp.where(first_block_ok, acc, MASK_VALUE)
        o_ref[1, g, :, j * WINDOW:(j + 1) * WINDOW] = acc


def _swa_bias(rel_bias):
    shape = (2, SWA_KV_HEADS, 2 * WINDOW, SWA_GROUP * WINDOW)
    return pl.pallas_call(
        _swa_bias_kernel,
        in_specs=[pl.BlockSpec(memory_space=pltpu.SMEM),
                  pl.BlockSpec((2 * WINDOW, WINDOW), lambda: (0, 0))],
        out_specs=pl.BlockSpec(shape, lambda: (0, 0, 0, 0)),
        out_shape=jax.ShapeDtypeStruct(shape, F32),
        name="swa_bias",
    )(rel_bias, jnp.asarray(_t5_bucket_table().T))


def _swa_kernel(sink_ref, q_ref, kv_ref, bias_ref, o_ref, kp_ref, vt_ref, *, blocks_per_iter):
    seq = kv_ref.shape[0]
    n_blocks = seq // WINDOW
    kdim = SWA_KV_HEADS * SWA_HEAD_DIM
    kp_ref[0:WINDOW, :] = jnp.zeros((WINDOW, kdim), BF16)
    kp_ref[WINDOW:, :] = kv_ref[:, 0:kdim]
    ones_rows = jnp.ones((SWA_ONES_ROWS, WINDOW), BF16)
    for g in range(SWA_KV_HEADS):
        vt_ref[0, g] = jnp.concatenate([jnp.zeros((SWA_HEAD_DIM, WINDOW), BF16), ones_rows], axis=0)
    for n in range(n_blocks):
        for g in range(SWA_KV_HEADS):
            v_blk = kv_ref[n * WINDOW:(n + 1) * WINDOW, kdim + g * SWA_HEAD_DIM:kdim + (g + 1) * SWA_HEAD_DIM]
            vt_ref[n + 1, g] = jnp.concatenate([v_blk.astype(F32).T.astype(BF16), ones_rows], axis=0)
    sink_rows = [[jnp.full((1, WINDOW), sink_ref[g * SWA_GROUP + j] * LOG2E, F32) for j in range(SWA_GROUP)]
                 for g in range(SWA_KV_HEADS)]

    def body(it, carry):
        units = [(b, g) for b in range(blocks_per_iter) for g in range(SWA_KV_HEADS)]
        blocks = [it * blocks_per_iter + b for b in range(blocks_per_iter)]
        q_starts = [pl.multiple_of(n * WINDOW, WINDOW) for n in blocks]
        tables = [jnp.minimum(n, 1) for n in blocks]
        sts, pts = {}, {}

        def scores(u):
            b, g = units[u]
            k = kp_ref[pl.ds(q_starts[b], 2 * WINDOW), g * SWA_HEAD_DIM:(g + 1) * SWA_HEAD_DIM]
            sts[u] = []
            for j in range(0, SWA_GROUP, 2):
                hd = g * SWA_GROUP + j
                q = jnp.concatenate([q_ref[0, hd, pl.ds(q_starts[b], WINDOW), :],
                                     q_ref[0, hd + 1, pl.ds(q_starts[b], WINDOW), :]], axis=0)
                st = lax.dot_general(k, q, NT_DIMS, preferred_element_type=F32)
                sts[u] += [st[:, 0:WINDOW], st[:, WINDOW:2 * WINDOW]]

        def softmax(u):
            b, g = units[u]
            pts[u] = []
            for j in range(SWA_GROUP):
                st = sts[u][j] + bias_ref[tables[b], g, :, j * WINDOW:(j + 1) * WINDOW]
                mx = jnp.maximum(jnp.max(st, axis=0, keepdims=True), sink_rows[g][j])
                pts[u].append((jnp.exp2(st - mx).astype(BF16), jnp.exp2(sink_rows[g][j] - mx)))

        def values(u):
            b, g = units[u]
            vt = jnp.concatenate([vt_ref[blocks[b], g], vt_ref[blocks[b] + 1, g]], axis=1)
            for j in range(0, SWA_GROUP, 2):
                pt = jnp.concatenate([pts[u][j][0], pts[u][j + 1][0]], axis=1)
                ot = _dot(vt, pt)
                halves = []
                for jj in range(2):
                    lanes = slice(jj * WINDOW, (jj + 1) * WINDOW)
                    denom = ot[SWA_HEAD_DIM:SWA_HEAD_DIM + 1, lanes] + pts[u][j + jj][1]
                    halves.append(ot[0:SWA_HEAD_DIM, lanes] * (1.0 / denom))
                pair = jnp.concatenate(halves, axis=0).T
                lane0 = (g * SWA_GROUP + j) * SWA_HEAD_DIM
                o_ref[pl.ds(q_starts[b], WINDOW), lane0:lane0 + 2 * SWA_HEAD_DIM] = pair.astype(BF16)

        for t in range(len(units) + 2):
            if t < len(units):
                scores(t)
            if 0 <= t - 1 < len(units):
                softmax(t - 1)
            if 0 <= t - 2 < len(units):
                values(t - 2)
        return carry

    lax.fori_loop(0, n_blocks // blocks_per_iter, body, 0)


def _swa(qa, kva, bias, sinks, *, batch, seq, blocks_per_iter=8):
    kvw = kva.shape[1]
    qw = SWA_HEADS * SWA_HEAD_DIM
    assert (seq // WINDOW) % blocks_per_iter == 0
    return pl.pallas_call(
        functools.partial(_swa_kernel, blocks_per_iter=blocks_per_iter),
        grid=(batch,),
        in_specs=[pl.BlockSpec(memory_space=pltpu.SMEM),
                  pl.BlockSpec((1, SWA_HEADS, seq, SWA_HEAD_DIM), lambda b: (b, 0, 0, 0)),
                  pl.BlockSpec((seq, kvw), lambda b: (b, 0)),
                  _resident(bias.shape)],
        out_specs=pl.BlockSpec((seq, qw), lambda b: (b, 0)),
        out_shape=jax.ShapeDtypeStruct((batch * seq, qw), BF16),
        scratch_shapes=[pltpu.VMEM((WINDOW + seq, SWA_KV_HEADS * SWA_HEAD_DIM), BF16),
                        pltpu.VMEM((1 + seq // WINDOW, SWA_KV_HEADS, SWA_HEAD_DIM + SWA_ONES_ROWS, WINDOW), BF16)],
        compiler_params=_params(1),
        name="swa",
    )(sinks, qa, kva, bias)


def _mla_kernel(q_ref, k_ref, v_ref, o_ref, vt_ref, *, tq):
    seq = q_ref.shape[2]
    dv = v_ref.shape[3]
    n_tiles = seq // tq
    for c in range(seq // WINDOW):
        v_blk = v_ref[0, 0, c * WINDOW:(c + 1) * WINDOW, :]
        vt_ref[0:dv, c * WINDOW:(c + 1) * WINDOW] = v_blk.astype(F32).T.astype(BF16)
    vt_ref[dv:, :] = jnp.ones((MLA_ONES_ROWS, seq), BF16)
    key = lax.broadcasted_iota(jnp.int32, (tq, tq), 0)
    qry = lax.broadcasted_iota(jnp.int32, (tq, tq), 1)
    causal = key <= qry
    sts, mxs, pts, accs = {}, {}, {}, {}

    def scores(i, j):
        q = q_ref[0, 0, i * tq:(i + 1) * tq, :]
        s = lax.dot_general(k_ref[0, 0, j * tq:(j + 1) * tq, :], q, NT_DIMS, preferred_element_type=F32)
        sts[i, j] = jnp.where(causal, s, MASK_VALUE) if j == i else s

    def col_max(i, j):
        m = jnp.max(sts[i, j], axis=0, keepdims=True)
        mxs[i] = m if j == 0 else jnp.maximum(mxs[i], m)

    def probs(i, j):
        pts[i, j] = jnp.exp2(sts.pop((i, j)) - mxs[i]).astype(BF16)

    def values(i, j):
        part = _dot(vt_ref[:, j * tq:(j + 1) * tq], pts.pop((i, j)))
        accs[i] = part if j == 0 else accs[i] + part
        if j == i:
            ot = accs.pop(i)
            o = ot[0:dv, :] * (1.0 / ot[dv:dv + 1, :])
            o_ref[i * tq:(i + 1) * tq, :] = o.T.astype(BF16)

    stages = (scores, col_max, probs, values)
    for t in range(n_tiles + len(stages) - 1):
        for j in range(n_tiles):
            for lag, stage in enumerate(stages):
                i = t - lag
                if 0 <= i < n_tiles and j <= i:
                    stage(i, j)


def _mla(qm, km, vm, *, tq=256):
    batch, heads, seq, qk = qm.shape
    dv = vm.shape[-1]
    return pl.pallas_call(
        functools.partial(_mla_kernel, tq=tq),
        grid=(batch, heads),
        in_specs=[pl.BlockSpec((1, 1, seq, qk), lambda b, h: (b, h, 0, 0)),
                  pl.BlockSpec((1, 1, seq, qk), lambda b, h: (b, h, 0, 0)),
                  pl.BlockSpec((1, 1, seq, dv), lambda b, h: (b, h, 0, 0))],
        out_specs=pl.BlockSpec((seq, dv), lambda b, h: (b, h)),
        out_shape=jax.ShapeDtypeStruct((batch * seq, heads * dv), BF16),
        scratch_shapes=[pltpu.VMEM((dv + MLA_ONES_ROWS, seq), BF16)],
        compiler_params=_params(2),
        name="mla",
    )(qm, km, vm)


def kernel(x, c, w_mod, b_mod, norm_ffn1, ffn1_gate, ffn1_up, ffn1_down, norm_mix, w_in, q_norm, kv_norm, w_uq, w_ukv, sinks, w_o, norm_ffn2, ffn2_gate, ffn2_up, ffn2_down, rel_bias, norm_final):
    batch, seq, d = x.shape
    depth = w_mod.shape[0]
    x2d = x.reshape(batch * seq, d)
    bias = _swa_bias(rel_bias)
    for l in range(depth):
        mod3 = _mod(c, w_mod[l], b_mod[l]).reshape(batch, N_MOD, d)
        x2d = _ffn(x2d, mod3, norm_ffn1[l], ffn1_gate[l].astype(BF16), ffn1_up[l].astype(BF16),
                   ffn1_down[l].astype(BF16), mod_row=0, seq=seq)
        qa, kva, qm, km, vm = _mix_proj(x2d, mod3, norm_mix[l], w_in[l], q_norm[l], kv_norm[l],
                                        w_uq[l], w_ukv[l], batch=batch, seq=seq)
        out_a = _swa(qa, kva, bias, sinks[l], batch=batch, seq=seq)
        out_b = _mla(qm, km, vm)
        last = l == depth - 1
        x2d = _ffn(x2d, mod3, norm_ffn2[l], ffn2_gate[l].astype(BF16), ffn2_up[l].astype(BF16),
                   ffn2_down[l].astype(BF16), mod_row=6, seq=seq, mix=(out_a, out_b, w_o[l].astype(BF16)),
                   final_gain=norm_final if last else None)
    return x2d.reshape(batch, seq, d)
```

```python
import functools
import math

import jax
import jax.numpy as jnp
import numpy as np
from jax import lax
from jax.experimental import pallas as pl
from jax.experimental.pallas import tpu as pltpu

EPS = 1e-6
FFN_RES_WEIGHT = 0.5
N_MOD = 9

SWA_HEADS = 8
SWA_KV_HEADS = 2
SWA_HEAD_DIM = 64
SWA_GROUP = SWA_HEADS // SWA_KV_HEADS
WINDOW = 128

MLA_HEADS = 4
MLA_Q_RANK = 256
MLA_KV_RANK = 128
MLA_NOPE = 128
MLA_ROPE = 64
MLA_V = 128
MLA_QK = MLA_NOPE + MLA_ROPE
ROPE_THETA = 10000.0

NUM_BUCKETS = 32
MAX_DISTANCE = 128

SWA_ONES_ROWS = 16
MLA_ONES_ROWS = 16
LOG2E = math.log2(math.e)
MASK_VALUE = -1e30
V7X_VMEM_LIMIT_BYTES = 56 * 1024 * 1024

F32 = jnp.float32
BF16 = jnp.bfloat16
NT_DIMS = (((1,), (1,)), ((), ()))


def _params(n_grid_dims):
    return pltpu.CompilerParams(dimension_semantics=("arbitrary",) * n_grid_dims,
                                vmem_limit_bytes=V7X_VMEM_LIMIT_BYTES)


def _resident(shape):
    return pl.BlockSpec(shape, lambda *_: (0,) * len(shape), pipeline_mode=pl.Buffered(1))


def _rms_norm(x, gain):
    ms = jnp.mean(x * x, axis=-1, keepdims=True)
    return x * lax.rsqrt(ms + EPS) * gain


def _silu(x):
    return x * (1.0 / (1.0 + jnp.exp(-x)))


def _dot(a, b):
    return jnp.dot(a, b, preferred_element_type=F32)


def _mod_kernel(c_ref, w_ref, b_ref, o_ref):
    c_act = _silu(c_ref[...]).astype(BF16)
    o_ref[...] = _dot(c_act, w_ref[...].astype(BF16)) + b_ref[...]


def _mod(c, w_mod, b_mod, *, tn=1024):
    batch, d = c.shape
    n = w_mod.shape[1]
    return pl.pallas_call(
        _mod_kernel,
        grid=(n // tn,),
        in_specs=[pl.BlockSpec((batch, d), lambda j: (0, 0)),
                  pl.BlockSpec((d, tn), lambda j: (0, j)),
                  pl.BlockSpec((1, tn), lambda j: (0, j))],
        out_specs=pl.BlockSpec((batch, tn), lambda j: (0, j)),
        out_shape=jax.ShapeDtypeStruct((batch, n), F32),
        compiler_params=_params(1),
        name="mod",
    )(c, w_mod, b_mod.reshape(1, n))


def _ffn_kernel(*refs, mod_row, n_w, tiles_per_seq, mix_residual, final_norm):
    refs = list(refs)
    x_ref, mod_ref = refs[:2]
    del refs[:2]
    if mix_residual:
        oa_ref, ob_ref, woa_ref, wob_ref = refs[:4]
        del refs[:4]
    g_ref, wg_ref, wu_ref, wd_ref = refs[:4]
    del refs[:4]
    if final_norm:
        gf_ref = refs.pop(0)
    o_ref, wg_s, wu_s, wd_s, h_even, h_odd, xres_even, xres_odd, a_s = refs
    fw = wg_ref.shape[1]
    n_tiles = pl.num_programs(0) - n_w - 1
    step = pl.program_id(0)
    t = step - n_w

    def norm_stage(xres_next, h_next):
        mod = mod_ref[jnp.minimum(t, n_tiles - 1) // tiles_per_seq]
        x = x_ref[...]
        if mix_residual:
            mix = _dot(oa_ref[...], woa_ref[...]) + _dot(ob_ref[...], wob_ref[...])
            x = x + mod[mod_row - 1:mod_row, :] * mix
        xres_next[...] = x
        gain = g_ref[...] * (1.0 + mod[mod_row + 1:mod_row + 2, :])
        ms = jnp.mean(x * x, axis=-1, keepdims=True)
        h_next[...] = (x * lax.rsqrt(ms + EPS) * gain + mod[mod_row:mod_row + 1, :]).astype(BF16)

    def matmul_stage(xres_cur, h_cur):
        for c in range(n_w):
            g = _dot(h_cur[...], wg_s[c])
            u = _dot(h_cur[...], wu_s[c])
            a_s[:, c * fw:(c + 1) * fw] = (_silu(g) * u).astype(BF16)
        y = _dot(a_s[...], wd_s[...])
        gate = mod_ref[(t - 1) // tiles_per_seq][mod_row + 2:mod_row + 3, :]
        out = xres_cur[...] + FFN_RES_WEIGHT * gate * y
        if final_norm:
            out = _rms_norm(out, gf_ref[...])
        o_ref[...] = out

    @pl.when(step < n_w)
    def _():
        wg_s[step] = wg_ref[...].astype(BF16)
        wu_s[step] = wu_ref[...].astype(BF16)
        wd_s[pl.ds(pl.multiple_of(step * fw, fw), fw), :] = wd_ref[...].astype(BF16)

    @pl.when(t == 0)
    def _():
        norm_stage(xres_even, h_even)

    @pl.when(jnp.logical_and(t >= 1, lax.rem(t, 2) == 1))
    def _():
        norm_stage(xres_odd, h_odd)
        matmul_stage(xres_even, h_even)

    @pl.when(jnp.logical_and(t >= 2, lax.rem(t, 2) == 0))
    def _():
        norm_stage(xres_even, h_even)
        matmul_stage(xres_odd, h_odd)


def _ffn(x2d, mod3, gain, wg, wu, wd, *, mod_row, seq, mix=None, final_gain=None, tm=512, fw=256):
    m, d = x2d.shape
    f = wg.shape[1]
    assert f % fw == 0 and seq % tm == 0
    n_w, n_tiles, tiles_per_seq = f // fw, m // tm, seq // tm
    final_norm = final_gain is not None

    def tile(lag):
        return lambda s: (jnp.clip(s - n_w - lag, 0, n_tiles - 1), 0)

    def chunk(s):
        return jnp.minimum(s, n_w - 1)

    in_specs = [pl.BlockSpec((tm, d), tile(0)), _resident(mod3.shape)]
    args = [x2d, mod3]
    if mix is not None:
        out_a, out_b, w_o = mix
        wa, wb = out_a.shape[1], out_b.shape[1]
        in_specs += [pl.BlockSpec((tm, wa), tile(0)), pl.BlockSpec((tm, wb), tile(0)),
                     _resident((wa, d)), _resident((wb, d))]
        args += [out_a, out_b, w_o[:wa], w_o[wa:]]
    in_specs += [_resident((1, d)),
                 pl.BlockSpec((d, fw), lambda s: (0, chunk(s))),
                 pl.BlockSpec((d, fw), lambda s: (0, chunk(s))),
                 pl.BlockSpec((fw, d), lambda s: (chunk(s), 0))]
    args += [gain.reshape(1, d), wg, wu, wd]
    if final_norm:
        in_specs.append(_resident((1, d)))
        args.append(final_gain.reshape(1, d))
    return pl.pallas_call(
        functools.partial(_ffn_kernel, mod_row=mod_row, n_w=n_w, tiles_per_seq=tiles_per_seq,
                          mix_residual=mix is not None, final_norm=final_norm),
        grid=(n_w + n_tiles + 1,),
        in_specs=in_specs,
        out_specs=pl.BlockSpec((tm, d), tile(1)),
        out_shape=jax.ShapeDtypeStruct((m, d), F32),
        scratch_shapes=[pltpu.VMEM((n_w, d, fw), BF16), pltpu.VMEM((n_w, d, fw), BF16), pltpu.VMEM((f, d), BF16),
                        pltpu.VMEM((tm, d), BF16), pltpu.VMEM((tm, d), BF16),
                        pltpu.VMEM((tm, d), F32), pltpu.VMEM((tm, d), F32), pltpu.VMEM((tm, f), BF16)],
        compiler_params=_params(1),
        name="ffn_final" if final_norm else "ffn",
    )(*args)


_QA0, _KVA0, _QLAT0, _KVLAT0, _KR0, _KRS0, _PROJ_W = 0, 512, 768, 1024, 1152, 1216, 1280
_UQ_ROPE0 = MLA_HEADS * MLA_NOPE
_UQ_ROPES0 = _UQ_ROPE0 + MLA_HEADS * MLA_ROPE


def _mix_proj_kernel(x_ref, mod_ref, g_ref, win_ref, qn_ref, kvn_ref, wuq_ref, wukv_ref, cos_ref, sin_ref,
                     qa_ref, kva_ref, qm_ref, km_ref, vm_ref):
    x = x_ref[...]
    shift = mod_ref[0, 3:4, :]
    scale = mod_ref[0, 4:5, :]
    h = (_rms_norm(x, g_ref[...]) * (1.0 + scale) + shift).astype(BF16)
    proj = _dot(h, win_ref[...])
    for hd in range(SWA_HEADS):
        qa_ref[0, hd] = (proj[:, _QA0 + hd * SWA_HEAD_DIM:_QA0 + (hd + 1) * SWA_HEAD_DIM]
                         * (SWA_HEAD_DIM ** -0.5 * LOG2E)).astype(BF16)
    kva_ref[...] = proj[:, _KVA0:_QLAT0].astype(BF16)

    q_lat = _rms_norm(proj[:, _QLAT0:_KVLAT0], qn_ref[...]).astype(BF16)
    kv_lat = _rms_norm(proj[:, _KVLAT0:_KR0], kvn_ref[...]).astype(BF16)
    q_all = _dot(q_lat, wuq_ref[...])
    kv_all = _dot(kv_lat, wukv_ref[...])

    cos = cos_ref[...]
    sin = sin_ref[...]
    k_rope = (proj[:, _KR0:_KRS0] * cos + proj[:, _KRS0:_PROJ_W] * sin).astype(BF16)
    q_scale = MLA_QK ** -0.5 * LOG2E
    for hd in range(MLA_HEADS):
        q_nope = q_all[:, hd * MLA_NOPE:(hd + 1) * MLA_NOPE]
        q_rope = (q_all[:, _UQ_ROPE0 + hd * MLA_ROPE:_UQ_ROPE0 + (hd + 1) * MLA_ROPE] * cos
                  + q_all[:, _UQ_ROPES0 + hd * MLA_ROPE:_UQ_ROPES0 + (hd + 1) * MLA_ROPE] * sin)
        qm_ref[0, hd, :, 0:MLA_NOPE] = (q_nope * q_scale).astype(BF16)
        qm_ref[0, hd, :, MLA_NOPE:MLA_QK] = (q_rope * q_scale).astype(BF16)
        kv0 = hd * (MLA_NOPE + MLA_V)
        km_ref[0, hd, :, 0:MLA_NOPE] = kv_all[:, kv0:kv0 + MLA_NOPE].astype(BF16)
        km_ref[0, hd, :, MLA_NOPE:MLA_QK] = k_rope
        vm_ref[0, hd] = kv_all[:, kv0 + MLA_NOPE:kv0 + MLA_NOPE + MLA_V].astype(BF16)


def _swap_halves(w):
    half = w.shape[-1] // 2
    return jnp.concatenate([w[..., half:], w[..., :half]], axis=-1)


def _mix_proj(x2d, mod3, gain, w_in, q_norm, kv_norm, w_uq, w_ukv, *, batch, seq, tm=512):
    m, d = x2d.shape
    tiles_per_seq = seq // tm
    w_in_ext = jnp.concatenate([w_in, _swap_halves(w_in[:, _KR0:_KRS0])], axis=1).astype(BF16)
    w_uq_h = w_uq.reshape(MLA_Q_RANK, MLA_HEADS, MLA_QK)
    uq_rope = w_uq_h[:, :, MLA_NOPE:]
    w_uq_r = jnp.concatenate([w_uq_h[:, :, :MLA_NOPE].reshape(MLA_Q_RANK, -1),
                              uq_rope.reshape(MLA_Q_RANK, -1),
                              _swap_halves(uq_rope).reshape(MLA_Q_RANK, -1)], axis=1).astype(BF16)
    w_ukv_b = w_ukv.astype(BF16)
    inv = ROPE_THETA ** (-jnp.arange(0, MLA_ROPE, 2, dtype=F32) / MLA_ROPE)
    ang = jnp.arange(seq, dtype=F32)[:, None] * inv[None, :]
    cos2 = jnp.concatenate([jnp.cos(ang), jnp.cos(ang)], axis=-1)
    sin2 = jnp.concatenate([-jnp.sin(ang), jnp.sin(ang)], axis=-1)

    def head_spec(width):
        return pl.BlockSpec((1, MLA_HEADS, tm, width),
                            lambda i: (i // tiles_per_seq, 0, i % tiles_per_seq, 0))

    def head_shape(width):
        return jax.ShapeDtypeStruct((batch, MLA_HEADS, seq, width), BF16)

    return pl.pallas_call(
        _mix_proj_kernel,
        grid=(m // tm,),
        in_specs=[pl.BlockSpec((tm, d), lambda i: (i, 0)),
                  pl.BlockSpec((1, N_MOD, d), lambda i: (i // tiles_per_seq, 0, 0)),
                  _resident((1, d)),
                  _resident((d, _PROJ_W)),
                  _resident((1, MLA_Q_RANK)), _resident((1, MLA_KV_RANK)),
                  _resident(w_uq_r.shape), _resident(w_ukv_b.shape),
                  pl.BlockSpec((tm, MLA_ROPE), lambda i: (i % tiles_per_seq, 0)),
                  pl.BlockSpec((tm, MLA_ROPE), lambda i: (i % tiles_per_seq, 0))],
        out_specs=[pl.BlockSpec((1, SWA_HEADS, tm, SWA_HEAD_DIM),
                                lambda i: (i // tiles_per_seq, 0, i % tiles_per_seq, 0)),
                   pl.BlockSpec((tm, 2 * SWA_KV_HEADS * SWA_HEAD_DIM), lambda i: (i, 0)),
                   head_spec(MLA_QK), head_spec(MLA_QK), head_spec(MLA_V)],
        out_shape=[jax.ShapeDtypeStruct((batch, SWA_HEADS, seq, SWA_HEAD_DIM), BF16),
                   jax.ShapeDtypeStruct((m, 2 * SWA_KV_HEADS * SWA_HEAD_DIM), BF16),
                   head_shape(MLA_QK), head_shape(MLA_QK), head_shape(MLA_V)],
        compiler_params=_params(1),
        name="mix_proj",
    )(x2d, mod3, gain.reshape(1, d), w_in_ext, q_norm.reshape(1, -1), kv_norm.reshape(1, -1),
      w_uq_r, w_ukv_b, cos2, sin2)


def _t5_bucket_table():
    qi = np.arange(WINDOW)[:, None]
    kj = np.arange(2 * WINDOW)[None, :]
    dist = qi + WINDOW - kj
    max_exact = NUM_BUCKETS // 2
    n = np.maximum(dist, 0)
    nf = np.maximum(n, 1).astype(np.float32)
    large = max_exact + (np.log(nf / np.float32(max_exact)) / np.float32(math.log(MAX_DISTANCE / max_exact))
                         * np.float32(NUM_BUCKETS - max_exact)).astype(np.int32)
    large = np.minimum(large, NUM_BUCKETS - 1)
    bucket = np.where(n < max_exact, n, large)
    band = (dist >= 0) & (dist < WINDOW)
    return np.where(band, bucket, -1).astype(np.int32)


def _swa_bias_kernel(rel_ref, bucket_ref, o_ref):
    bucket = bucket_ref[...]
    first_block_ok = lax.broadcasted_iota(jnp.int32, bucket.shape, 0) >= WINDOW
    for hd in range(SWA_HEADS):
        acc = jnp.full(bucket.shape, MASK_VALUE, F32)
        for b in range(NUM_BUCKETS):
            acc = jnp.where(bucket == b, rel_ref[b, hd] * LOG2E, acc)
        g, j = divmod(hd, SWA_GROUP)
        o_ref[0, g, :, j * WINDOW:(j + 1) * WINDOW] = jnp.where(first_block_ok, acc, MASK_VALUE)
        o_ref[1, g, :, j * WINDOW:(j + 1) * WINDOW] = acc


def _swa_bias(rel_bias):
    shape = (2, SWA_KV_HEADS, 2 * WINDOW, SWA_GROUP * WINDOW)
    return pl.pallas_call(
        _swa_bias_kernel,
        in_specs=[pl.BlockSpec(memory_space=pltpu.SMEM),
                  pl.BlockSpec((2 * WINDOW, WINDOW), lambda: (0, 0))],
        out_specs=pl.BlockSpec(shape, lambda: (0, 0, 0, 0)),
        out_shape=jax.ShapeDtypeStruct(shape, F32),
        name="swa_bias",
    )(rel_bias, jnp.asarray(_t5_bucket_table().T))


def _swa_kernel(sink_ref, q_ref, kv_ref, bias_ref, o_ref, kp_ref, vt_ref, *, blocks_per_iter):
    seq = kv_ref.shape[0]
    n_blocks = seq // WINDOW
    kdim = SWA_KV_HEADS * SWA_HEAD_DIM
    kp_ref[0:WINDOW, :] = jnp.zeros((WINDOW, kdim), BF16)
    kp_ref[WINDOW:, :] = kv_ref[:, 0:kdim]
    ones_rows = jnp.ones((SWA_ONES_ROWS, WINDOW), BF16)
    for g in range(SWA_KV_HEADS):
        vt_ref[0, g] = jnp.concatenate([jnp.zeros((SWA_HEAD_DIM, WINDOW), BF16), ones_rows], axis=0)
    for n in range(n_blocks):
        for g in range(SWA_KV_HEADS):
            v_blk = kv_ref[n * WINDOW:(n + 1) * WINDOW, kdim + g * SWA_HEAD_DIM:kdim + (g + 1) * SWA_HEAD_DIM]
            vt_ref[n + 1, g] = jnp.concatenate([v_blk.astype(F32).T.astype(BF16), ones_rows], axis=0)
    sink_rows = [[jnp.full((1, WINDOW), sink_ref[g * SWA_GROUP + j] * LOG2E, F32) for j in range(SWA_GROUP)]
                 for g in range(SWA_KV_HEADS)]

    def body(it, carry):
        units = [(b, g) for b in range(blocks_per_iter) for g in range(SWA_KV_HEADS)]
        blocks = [it * blocks_per_iter + b for b in range(blocks_per_iter)]
        q_starts = [pl.multiple_of(n * WINDOW, WINDOW) for n in blocks]
        tables = [jnp.minimum(n, 1) for n in blocks]
        sts, pts = {}, {}

        def scores(u):
            b, g = units[u]
            k = kp_ref[pl.ds(q_starts[b], 2 * WINDOW), g * SWA_HEAD_DIM:(g + 1) * SWA_HEAD_DIM]
            sts[u] = []
            for j in range(0, SWA_GROUP, 2):
                hd = g * SWA_GROUP + j
                q = jnp.concatenate([q_ref[0, hd, pl.ds(q_starts[b], WINDOW), :],
                                     q_ref[0, hd + 1, pl.ds(q_starts[b], WINDOW), :]], axis=0)
                st = lax.dot_general(k, q, NT_DIMS, preferred_element_type=F32)
                sts[u] += [st[:, 0:WINDOW], st[:, WINDOW:2 * WINDOW]]

        def softmax(u):
            b, g = units[u]
            pts[u] = []
            for j in range(SWA_GROUP):
                st = sts[u][j] + bias_ref[tables[b], g, :, j * WINDOW:(j + 1) * WINDOW]
                mx = jnp.maximum(jnp.max(st, axis=0, keepdims=True), sink_rows[g][j])
                pts[u].append((jnp.exp2(st - mx).astype(BF16), jnp.exp2(sink_rows[g][j] - mx)))

        def values(u):
            b, g = units[u]
            vt = jnp.concatenate([vt_ref[blocks[b], g], vt_ref[blocks[b] + 1, g]], axis=1)
            for j in range(0, SWA_GROUP, 2):
                pt = jnp.concatenate([pts[u][j][0], pts[u][j + 1][0]], axis=1)
                ot = _dot(vt, pt)
                halves = []
                for jj in range(2):
                    lanes = slice(jj * WINDOW, (jj + 1) * WINDOW)
                    denom = ot[SWA_HEAD_DIM:SWA_HEAD_DIM + 1, lanes] + pts[u][j + jj][1]
                    halves.append(ot[0:SWA_HEAD_DIM, lanes] * (1.0 / denom))
                pair = jnp.concatenate(halves, axis=0).T
                lane0 = (g * SWA_GROUP + j) * SWA_HEAD_DIM
                o_ref[pl.ds(q_starts[b], WINDOW), lane0:lane0 + 2 * SWA_HEAD_DIM] = pair.astype(BF16)

        for t in range(len(units) + 2):
            if t < len(units):
                scores(t)
            if 0 <= t - 1 < len(units):
                softmax(t - 1)
            if 0 <= t - 2 < len(units):
                values(t - 2)
        return carry

    lax.fori_loop(0, n_blocks // blocks_per_iter, body, 0)


def _swa(qa, kva, bias, sinks, *, batch, seq, blocks_per_iter=8):
    kvw = kva.shape[1]
    qw = SWA_HEADS * SWA_HEAD_DIM
    assert (seq // WINDOW) % blocks_per_iter == 0
    return pl.pallas_call(
        functools.partial(_swa_kernel, blocks_per_iter=blocks_per_iter),
        grid=(batch,),
        in_specs=[pl.BlockSpec(memory_space=pltpu.SMEM),
                  pl.BlockSpec((1, SWA_HEADS, seq, SWA_HEAD_DIM), lambda b: (b, 0, 0, 0)),
                  pl.BlockSpec((seq, kvw), lambda b: (b, 0)),
                  _resident(bias.shape)],
        out_specs=pl.BlockSpec((seq, qw), lambda b: (b, 0)),
        out_shape=jax.ShapeDtypeStruct((batch * seq, qw), BF16),
        scratch_shapes=[pltpu.VMEM((WINDOW + seq, SWA_KV_HEADS * SWA_HEAD_DIM), BF16),
                        pltpu.VMEM((1 + seq // WINDOW, SWA_KV_HEADS, SWA_HEAD_DIM + SWA_ONES_ROWS, WINDOW), BF16)],
        compiler_params=_params(1),
        name="swa",
    )(sinks, qa, kva, bias)


def _mla_kernel(q_ref, k_ref, v_ref, o_ref, vt_ref, *, tq):
    seq = q_ref.shape[2]
    dv = v_ref.shape[3]
    n_tiles = seq // tq
    for c in range(seq // WINDOW):
        v_blk = v_ref[0, 0, c * WINDOW:(c + 1) * WINDOW, :]
        vt_ref[0:dv, c * WINDOW:(c + 1) * WINDOW] = v_blk.astype(F32).T.astype(BF16)
    vt_ref[dv:, :] = jnp.ones((MLA_ONES_ROWS, seq), BF16)
    key = lax.broadcasted_iota(jnp.int32, (tq, tq), 0)
    qry = lax.broadcasted_iota(jnp.int32, (tq, tq), 1)
    causal = key <= qry
    sts, mxs, pts, accs = {}, {}, {}, {}

    def scores(i, j):
        q = q_ref[0, 0, i * tq:(i + 1) * tq, :]
        s = lax.dot_general(k_ref[0, 0, j * tq:(j + 1) * tq, :], q, NT_DIMS, preferred_element_type=F32)
        sts[i, j] = jnp.where(causal, s, MASK_VALUE) if j == i else s

    def col_max(i, j):
        m = jnp.max(sts[i, j], axis=0, keepdims=True)
        mxs[i] = m if j == 0 else jnp.maximum(mxs[i], m)

    def probs(i, j):
        pts[i, j] = jnp.exp2(sts.pop((i, j)) - mxs[i]).astype(BF16)

    def values(i, j):
        part = _dot(vt_ref[:, j * tq:(j + 1) * tq], pts.pop((i, j)))
        accs[i] = part if j == 0 else accs[i] + part
        if j == i:
            ot = accs.pop(i)
            o = ot[0:dv, :] * (1.0 / ot[dv:dv + 1, :])
            o_ref[i * tq:(i + 1) * tq, :] = o.T.astype(BF16)

    stages = (scores, col_max, probs, values)
    for t in range(n_tiles + len(stages) - 1):
        for j in range(n_tiles):
            for lag, stage in enumerate(stages):
                i = t - lag
                if 0 <= i < n_tiles and j <= i:
                    stage(i, j)


def _mla(qm, km, vm, *, tq=256):
    batch, heads, seq, qk = qm.shape
    dv = vm.shape[-1]
    return pl.pallas_call(
        functools.partial(_mla_kernel, tq=tq),
        grid=(batch, heads),
        in_specs=[pl.BlockSpec((1, 1, seq, qk), lambda b, h: (b, h, 0, 0)),
                  pl.BlockSpec((1, 1, seq, qk), lambda b, h: (b, h, 0, 0)),
                  pl.BlockSpec((1, 1, seq, dv), lambda b, h: (b, h, 0, 0))],
        out_specs=pl.BlockSpec((seq, dv), lambda b, h: (b, h)),
        out_shape=jax.ShapeDtypeStruct((batch * seq, heads * dv), BF16),
        scratch_shapes=[pltpu.VMEM((dv + MLA_ONES_ROWS, seq), BF16)],
        compiler_params=_params(2),
        name="mla",
    )(qm, km, vm)


def kernel(x, c, w_mod, b_mod, norm_ffn1, ffn1_gate, ffn1_up, ffn1_down, norm_mix, w_in, q_norm, kv_norm, w_uq, w_ukv, sinks, w_o, norm_ffn2, ffn2_gate, ffn2_up, ffn2_down, rel_bias, norm_final):
    batch, seq, d = x.shape
    depth = w_mod.shape[0]
    x2d = x.reshape(batch * seq, d)
    bias = _swa_bias(rel_bias)
    for l in range(depth):
        mod3 = _mod(c, w_mod[l], b_mod[l]).reshape(batch, N_MOD, d)
        x2d = _ffn(x2d, mod3, norm_ffn1[l], ffn1_gate[l], ffn1_up[l], ffn1_down[l], mod_row=0, seq=seq)
        qa, kva, qm, km, vm = _mix_proj(x2d, mod3, norm_mix[l], w_in[l], q_norm[l], kv_norm[l],
                                        w_uq[l], w_ukv[l], batch=batch, seq=seq)
        out_a = _swa(qa, kva, bias, sinks[l], batch=batch, seq=seq)
        out_b = _mla(qm, km, vm)
        last = l == depth - 1
        x2d = _ffn(x2d, mod3, norm_ffn2[l], ffn2_gate[l], ffn2_up[l], ffn2_down[l], mod_row=6, seq=seq,
                   mix=(out_a, out_b, w_o[l].astype(BF16)),
                   final_gain=norm_final if last else None)
    return x2d.reshape(batch, seq, d)
```

```python
import functools
import math

import jax
import jax.numpy as jnp
import numpy as np
from jax import lax
from jax.experimental import pallas as pl
from jax.experimental.pallas import tpu as pltpu

EPS = 1e-6
FFN_RES_WEIGHT = 0.5
N_MOD = 9

SWA_HEADS = 8
SWA_KV_HEADS = 2
SWA_HEAD_DIM = 64
SWA_GROUP = SWA_HEADS // SWA_KV_HEADS
WINDOW = 128

MLA_HEADS = 4
MLA_Q_RANK = 256
MLA_KV_RANK = 128
MLA_NOPE = 128
MLA_ROPE = 64
MLA_V = 128
MLA_QK = MLA_NOPE + MLA_ROPE
ROPE_THETA = 10000.0

NUM_BUCKETS = 32
MAX_DISTANCE = 128

SWA_ONES_ROWS = 16
MLA_ONES_ROWS = 16
LOG2E = math.log2(math.e)
MASK_VALUE = -1e30
V7X_VMEM_LIMIT_BYTES = 56 * 1024 * 1024

F32 = jnp.float32
BF16 = jnp.bfloat16
NT_DIMS = (((1,), (1,)), ((), ()))


def _params(n_grid_dims):
    return pltpu.CompilerParams(dimension_semantics=("arbitrary",) * n_grid_dims,
                                vmem_limit_bytes=V7X_VMEM_LIMIT_BYTES)


def _resident(shape):
    return pl.BlockSpec(shape, lambda *_: (0,) * len(shape), pipeline_mode=pl.Buffered(1))


def _rms_norm(x, gain):
    ms = jnp.mean(x * x, axis=-1, keepdims=True)
    return x * lax.rsqrt(ms + EPS) * gain


def _silu(x):
    return x * (1.0 / (1.0 + jnp.exp(-x)))


def _dot(a, b):
    return jnp.dot(a, b, preferred_element_type=F32)


def _mod_kernel(c_ref, w_ref, b_ref, o_ref):
    c_act = _silu(c_ref[...]).astype(BF16)
    o_ref[...] = _dot(c_act, w_ref[...].astype(BF16)) + b_ref[...]


def _mod(c, w_mod, b_mod, *, tn=1024):
    batch, d = c.shape
    n = w_mod.shape[1]
    return pl.pallas_call(
        _mod_kernel,
        grid=(n // tn,),
        in_specs=[pl.BlockSpec((batch, d), lambda j: (0, 0)),
                  pl.BlockSpec((d, tn), lambda j: (0, j)),
                  pl.BlockSpec((1, tn), lambda j: (0, j))],
        out_specs=pl.BlockSpec((batch, tn), lambda j: (0, j)),
        out_shape=jax.ShapeDtypeStruct((batch, n), F32),
        compiler_params=_params(1),
        name="mod",
    )(c, w_mod, b_mod.reshape(1, n))


def _ffn_kernel(*refs, mod_row, n_w, tiles_per_seq, mix_residual, final_norm):
    refs = list(refs)
    x_ref, mod_ref = refs[:2]
    del refs[:2]
    if mix_residual:
        oa_ref, ob_ref, woa_ref, wob_ref = refs[:4]
        del refs[:4]
    g_ref, wg_ref, wu_ref, wd_ref = refs[:4]
    del refs[:4]
    if final_norm:
        gf_ref = refs.pop(0)
    o_ref, wg_s, wu_s, wd_s, h_even, h_odd, xres_even, xres_odd, a_s = refs
    fw = wg_ref.shape[1]
    n_tiles = pl.num_programs(0) - n_w - 1
    step = pl.program_id(0)
    t = step - n_w

    def norm_stage(xres_next, h_next):
        mod = mod_ref[jnp.minimum(t, n_tiles - 1) // tiles_per_seq]
        x = x_ref[...]
        if mix_residual:
            mix = _dot(oa_ref[...], woa_ref[...]) + _dot(ob_ref[...], wob_ref[...])
            x = x + mod[mod_row - 1:mod_row, :] * mix
        xres_next[...] = x
        gain = g_ref[...] * (1.0 + mod[mod_row + 1:mod_row + 2, :])
        ms = jnp.mean(x * x, axis=-1, keepdims=True)
        h_next[...] = (x * lax.rsqrt(ms + EPS) * gain + mod[mod_row:mod_row + 1, :]).astype(BF16)

    def matmul_stage(xres_cur, h_cur):
        for c in range(n_w):
            g = _dot(h_cur[...], wg_s[c])
            u = _dot(h_cur[...], wu_s[c])
            a_s[:, c * fw:(c + 1) * fw] = (_silu(g) * u).astype(BF16)
        y = _dot(a_s[...], wd_s[...])
        gate = mod_ref[(t - 1) // tiles_per_seq][mod_row + 2:mod_row + 3, :]
        out = xres_cur[...] + FFN_RES_WEIGHT * gate * y
        if final_norm:
            out = _rms_norm(out, gf_ref[...])
        o_ref[...] = out

    @pl.when(step < n_w)
    def _():
        wg_s[step] = wg_ref[...].astype(BF16)
        wu_s[step] = wu_ref[...].astype(BF16)
        wd_s[pl.ds(pl.multiple_of(step * fw, fw), fw), :] = wd_ref[...].astype(BF16)

    @pl.when(t == 0)
    def _():
        norm_stage(xres_even, h_even)

    @pl.when(jnp.logical_and(t >= 1, lax.rem(t, 2) == 1))
    def _():
        norm_stage(xres_odd, h_odd)
        matmul_stage(xres_even, h_even)

    @pl.when(jnp.logical_and(t >= 2, lax.rem(t, 2) == 0))
    def _():
        norm_stage(xres_even, h_even)
        matmul_stage(xres_odd, h_odd)


def _ffn(x2d, mod3, gain, wg, wu, wd, *, mod_row, seq, mix=None, final_gain=None, tm=512, fw=256):
    m, d = x2d.shape
    f = wg.shape[1]
    assert f % fw == 0 and seq % tm == 0
    n_w, n_tiles, tiles_per_seq = f // fw, m // tm, seq // tm
    final_norm = final_gain is not None

    def tile(lag):
        return lambda s: (jnp.clip(s - n_w - lag, 0, n_tiles - 1), 0)

    def chunk(s):
        return jnp.minimum(s, n_w - 1)

    in_specs = [pl.BlockSpec((tm, d), tile(0)), _resident(mod3.shape)]
    args = [x2d, mod3]
    if mix is not None:
        out_a, out_b, w_o = mix
        wa, wb = out_a.shape[1], out_b.shape[1]
        in_specs += [pl.BlockSpec((tm, wa), tile(0)), pl.BlockSpec((tm, wb), tile(0)),
                     _resident((wa, d)), _resident((wb, d))]
        args += [out_a, out_b, w_o[:wa], w_o[wa:]]
    in_specs += [_resident((1, d)),
                 pl.BlockSpec((d, fw), lambda s: (0, chunk(s))),
                 pl.BlockSpec((d, fw), lambda s: (0, chunk(s))),
                 pl.BlockSpec((fw, d), lambda s: (chunk(s), 0))]
    args += [gain.reshape(1, d), wg, wu, wd]
    if final_norm:
        in_specs.append(_resident((1, d)))
        args.append(final_gain.reshape(1, d))
    return pl.pallas_call(
        functools.partial(_ffn_kernel, mod_row=mod_row, n_w=n_w, tiles_per_seq=tiles_per_seq,
                          mix_residual=mix is not None, final_norm=final_norm),
        grid=(n_w + n_tiles + 1,),
        in_specs=in_specs,
        out_specs=pl.BlockSpec((tm, d), tile(1)),
        out_shape=jax.ShapeDtypeStruct((m, d), F32),
        scratch_shapes=[pltpu.VMEM((n_w, d, fw), BF16), pltpu.VMEM((n_w, d, fw), BF16), pltpu.VMEM((f, d), BF16),
                        pltpu.VMEM((tm, d), BF16), pltpu.VMEM((tm, d), BF16),
                        pltpu.VMEM((tm, d), F32), pltpu.VMEM((tm, d), F32), pltpu.VMEM((tm, f), BF16)],
        compiler_params=_params(1),
        name="ffn_final" if final_norm else "ffn",
    )(*args)


_QA0, _KVA0, _QLAT0, _KVLAT0, _KR0, _KRS0, _PROJ_W = 0, 512, 768, 1024, 1152, 1216, 1280
_UQ_ROPE0 = MLA_HEADS * MLA_NOPE
_UQ_ROPES0 = _UQ_ROPE0 + MLA_HEADS * MLA_ROPE


def _mix_proj_kernel(x_ref, mod_ref, g_ref, win_ref, qn_ref, kvn_ref, wuq_ref, wukv_ref, cos_ref, sin_ref,
                     qa_ref, kva_ref, qm_ref, km_ref, vm_ref):
    x = x_ref[...]
    shift = mod_ref[0, 3:4, :]
    scale = mod_ref[0, 4:5, :]
    h = (_rms_norm(x, g_ref[...]) * (1.0 + scale) + shift).astype(BF16)
    proj = _dot(h, win_ref[...])
    for hd in range(SWA_HEADS):
        qa_ref[0, hd] = (proj[:, _QA0 + hd * SWA_HEAD_DIM:_QA0 + (hd + 1) * SWA_HEAD_DIM]
                         * (SWA_HEAD_DIM ** -0.5 * LOG2E)).astype(BF16)
    kva_ref[...] = proj[:, _KVA0:_QLAT0].astype(BF16)

    q_lat = _rms_norm(proj[:, _QLAT0:_KVLAT0], qn_ref[...]).astype(BF16)
    kv_lat = _rms_norm(proj[:, _KVLAT0:_KR0], kvn_ref[...]).astype(BF16)
    q_all = _dot(q_lat, wuq_ref[...])
    kv_all = _dot(kv_lat, wukv_ref[...])

    cos = cos_ref[...]
    sin = sin_ref[...]
    k_rope = (proj[:, _KR0:_KRS0] * cos + proj[:, _KRS0:_PROJ_W] * sin).astype(BF16)
    q_scale = MLA_QK ** -0.5 * LOG2E
    for hd in range(MLA_HEADS):
        q_nope = q_all[:, hd * MLA_NOPE:(hd + 1) * MLA_NOPE]
        q_rope = (q_all[:, _UQ_ROPE0 + hd * MLA_ROPE:_UQ_ROPE0 + (hd + 1) * MLA_ROPE] * cos
                  + q_all[:, _UQ_ROPES0 + hd * MLA_ROPE:_UQ_ROPES0 + (hd + 1) * MLA_ROPE] * sin)
        qm_ref[0, hd, :, 0:MLA_NOPE] = (q_nope * q_scale).astype(BF16)
        qm_ref[0, hd, :, MLA_NOPE:MLA_QK] = (q_rope * q_scale).astype(BF16)
        kv0 = hd * (MLA_NOPE + MLA_V)
        km_ref[0, hd, :, 0:MLA_NOPE] = kv_all[:, kv0:kv0 + MLA_NOPE].astype(BF16)
        km_ref[0, hd, :, MLA_NOPE:MLA_QK] = k_rope
        vm_ref[0, hd] = kv_all[:, kv0 + MLA_NOPE:kv0 + MLA_NOPE + MLA_V].astype(BF16)


def _swap_halves(w):
    half = w.shape[-1] // 2
    return jnp.concatenate([w[..., half:], w[..., :half]], axis=-1)


def _mix_proj(x2d, mod3, gain, w_in, q_norm, kv_norm, w_uq, w_ukv, *, batch, seq, tm=512):
    m, d = x2d.shape
    tiles_per_seq = seq // tm
    w_in_ext = jnp.concatenate([w_in, _swap_halves(w_in[:, _KR0:_KRS0])], axis=1).astype(BF16)
    w_uq_h = w_uq.reshape(MLA_Q_RANK, MLA_HEADS, MLA_QK)
    uq_rope = w_uq_h[:, :, MLA_NOPE:]
    w_uq_r = jnp.concatenate([w_uq_h[:, :, :MLA_NOPE].reshape(MLA_Q_RANK, -1),
                              uq_rope.reshape(MLA_Q_RANK, -1),
                              _swap_halves(uq_rope).reshape(MLA_Q_RANK, -1)], axis=1).astype(BF16)
    w_ukv_b = w_ukv.astype(BF16)
    inv = ROPE_THETA ** (-jnp.arange(0, MLA_ROPE, 2, dtype=F32) / MLA_ROPE)
    ang = jnp.arange(seq, dtype=F32)[:, None] * inv[None, :]
    cos2 = jnp.concatenate([jnp.cos(ang), jnp.cos(ang)], axis=-1)
    sin2 = jnp.concatenate([-jnp.sin(ang), jnp.sin(ang)], axis=-1)

    def head_spec(width):
        return pl.BlockSpec((1, MLA_HEADS, tm, width),
                            lambda i: (i // tiles_per_seq, 0, i % tiles_per_seq, 0))

    def head_shape(width):
        return jax.ShapeDtypeStruct((batch, MLA_HEADS, seq, width), BF16)

    return pl.pallas_call(
        _mix_proj_kernel,
        grid=(m // tm,),
        in_specs=[pl.BlockSpec((tm, d), lambda i: (i, 0)),
                  pl.BlockSpec((1, N_MOD, d), lambda i: (i // tiles_per_seq, 0, 0)),
                  _resident((1, d)),
                  _resident((d, _PROJ_W)),
                  _resident((1, MLA_Q_RANK)), _resident((1, MLA_KV_RANK)),
                  _resident(w_uq_r.shape), _resident(w_ukv_b.shape),
                  pl.BlockSpec((tm, MLA_ROPE), lambda i: (i % tiles_per_seq, 0)),
                  pl.BlockSpec((tm, MLA_ROPE), lambda i: (i % tiles_per_seq, 0))],
        out_specs=[pl.BlockSpec((1, SWA_HEADS, tm, SWA_HEAD_DIM),
                                lambda i: (i // tiles_per_seq, 0, i % tiles_per_seq, 0)),
                   pl.BlockSpec((tm, 2 * SWA_KV_HEADS * SWA_HEAD_DIM), lambda i: (i, 0)),
                   head_spec(MLA_QK), head_spec(MLA_QK), head_spec(MLA_V)],
        out_shape=[jax.ShapeDtypeStruct((batch, SWA_HEADS, seq, SWA_HEAD_DIM), BF16),
                   jax.ShapeDtypeStruct((m, 2 * SWA_KV_HEADS * SWA_HEAD_DIM), BF16),
                   head_shape(MLA_QK), head_shape(MLA_QK), head_shape(MLA_V)],
        compiler_params=_params(1),
        name="mix_proj",
    )(x2d, mod3, gain.reshape(1, d), w_in_ext, q_norm.reshape(1, -1), kv_norm.reshape(1, -1),
      w_uq_r, w_ukv_b, cos2, sin2)


def _t5_bucket_table():
    qi = np.arange(WINDOW)[:, None]
    kj = np.arange(2 * WINDOW)[None, :]
    dist = qi + WINDOW - kj
    max_exact = NUM_BUCKETS // 2
    n = np.maximum(dist, 0)
    nf = np.maximum(n, 1).astype(np.float32)
    large = max_exact + (np.log(nf / np.float32(max_exact)) / np.float32(math.log(MAX_DISTANCE / max_exact))
                         * np.float32(NUM_BUCKETS - max_exact)).astype(np.int32)
    large = np.minimum(large, NUM_BUCKETS - 1)
    bucket = np.where(n < max_exact, n, large)
    band = (dist >= 0) & (dist < WINDOW)
    return np.where(band, bucket, -1).astype(np.int32)


def _swa_bias_kernel(rel_ref, bucket_ref, o_ref):
    bucket = bucket_ref[...]
    first_block_ok = lax.broadcasted_iota(jnp.int32, bucket.shape, 0) >= WINDOW
    for hd in range(SWA_HEADS):
        acc = jnp.full(bucket.shape, MASK_VALUE, F32)
        for b in range(NUM_BUCKETS):
            acc = jnp.where(bucket == b, rel_ref[b, hd] * LOG2E, acc)
        g, j = divmod(hd, SWA_GROUP)
        o_ref[0, g, :, j * WINDOW:(j + 1) * WINDOW] = jnp.where(first_block_ok, acc, MASK_VALUE)
        o_ref[1, g, :, j * WINDOW:(j + 1) * WINDOW] = acc


def _swa_bias(rel_bias):
    shape = (2, SWA_KV_HEADS, 2 * WINDOW, SWA_GROUP * WINDOW)
    return pl.pallas_call(
        _swa_bias_kernel,
        in_specs=[pl.BlockSpec(memory_space=pltpu.SMEM),
                  pl.BlockSpec((2 * WINDOW, WINDOW), lambda: (0, 0))],
        out_specs=pl.BlockSpec(shape, lambda: (0, 0, 0, 0)),
        out_shape=jax.ShapeDtypeStruct(shape, F32),
        name="swa_bias",
    )(rel_bias, jnp.asarray(_t5_bucket_table().T))


def _swa_kernel(sink_ref, q_ref, kv_ref, bias_ref, o_ref, kp_ref, vt_ref, *, blocks_per_iter):
    seq = kv_ref.shape[0]
    n_blocks = seq // WINDOW
    kdim = SWA_KV_HEADS * SWA_HEAD_DIM
    kp_ref[0:WINDOW, :] = jnp.zeros((WINDOW, kdim), BF16)
    kp_ref[WINDOW:, :] = kv_ref[:, 0:kdim]
    ones_rows = jnp.ones((SWA_ONES_ROWS, WINDOW), BF16)
    for g in range(SWA_KV_HEADS):
        vt_ref[0, g] = jnp.concatenate([jnp.zeros((SWA_HEAD_DIM, WINDOW), BF16), ones_rows], axis=0)
    for n in range(n_blocks):
        for g in range(SWA_KV_HEADS):
            v_blk = kv_ref[n * WINDOW:(n + 1) * WINDOW, kdim + g * SWA_HEAD_DIM:kdim + (g + 1) * SWA_HEAD_DIM]
            vt_ref[n + 1, g] = jnp.concatenate([v_blk.astype(F32).T.astype(BF16), ones_rows], axis=0)
    sink_rows = [[jnp.full((1, WINDOW), sink_ref[g * SWA_GROUP + j] * LOG2E, F32) for j in range(SWA_GROUP)]
                 for g in range(SWA_KV_HEADS)]

    def body(it, carry):
        units = [(b, g) for b in range(blocks_per_iter) for g in range(SWA_KV_HEADS)]
        blocks = [it * blocks_per_iter + b for b in range(blocks_per_iter)]
        q_starts = [pl.multiple_of(n * WINDOW, WINDOW) for n in blocks]
        tables = [jnp.minimum(n, 1) for n in blocks]
        sts, pts = {}, {}

        def scores(u):
            b, g = units[u]
            k = kp_ref[pl.ds(q_starts[b], 2 * WINDOW), g * SWA_HEAD_DIM:(g + 1) * SWA_HEAD_DIM]
            sts[u] = []
            for j in range(0, SWA_GROUP, 2):
                hd = g * SWA_GROUP + j
                q = jnp.concatenate([q_ref[0, hd, pl.ds(q_starts[b], WINDOW), :],
                                     q_ref[0, hd + 1, pl.ds(q_starts[b], WINDOW), :]], axis=0)
                st = lax.dot_general(k, q, NT_DIMS, preferred_element_type=F32)
                sts[u] += [st[:, 0:WINDOW], st[:, WINDOW:2 * WINDOW]]

        def softmax(u):
            b, g = units[u]
            pts[u] = []
            for j in range(SWA_GROUP):
                st = sts[u][j] + bias_ref[tables[b], g, :, j * WINDOW:(j + 1) * WINDOW]
                mx = jnp.maximum(jnp.max(st, axis=0, keepdims=True), sink_rows[g][j])
                pts[u].append((jnp.exp2(st - mx).astype(BF16), jnp.exp2(sink_rows[g][j] - mx)))

        def values(u):
            b, g = units[u]
            vt = jnp.concatenate([vt_ref[blocks[b], g], vt_ref[blocks[b] + 1, g]], axis=1)
            for j in range(0, SWA_GROUP, 2):
                pt = jnp.concatenate([pts[u][j][0], pts[u][j + 1][0]], axis=1)
                ot = _dot(vt, pt)
                halves = []
                for jj in range(2):
                    lanes = slice(jj * WINDOW, (jj + 1) * WINDOW)
                    denom = ot[SWA_HEAD_DIM:SWA_HEAD_DIM + 1, lanes] + pts[u][j + jj][1]
                    halves.append(ot[0:SWA_HEAD_DIM, lanes] * (1.0 / denom))
                pair = jnp.concatenate(halves, axis=0).T
                lane0 = (g * SWA_GROUP + j) * SWA_HEAD_DIM
                o_ref[pl.ds(q_starts[b], WINDOW), lane0:lane0 + 2 * SWA_HEAD_DIM] = pair.astype(BF16)

        for t in range(len(units) + 2):
            if t < len(units):
                scores(t)
            if 0 <= t - 1 < len(units):
                softmax(t - 1)
            if 0 <= t - 2 < len(units):
                values(t - 2)
        return carry

    lax.fori_loop(0, n_blocks // blocks_per_iter, body, 0)


def _swa(qa, kva, bias, sinks, *, batch, seq, blocks_per_iter=8):
    kvw = kva.shape[1]
    qw = SWA_HEADS * SWA_HEAD_DIM
    assert (seq // WINDOW) % blocks_per_iter == 0
    return pl.pallas_call(
        functools.partial(_swa_kernel, blocks_per_iter=blocks_per_iter),
        grid=(batch,),
        in_specs=[pl.BlockSpec(memory_space=pltpu.SMEM),
                  pl.BlockSpec((1, SWA_HEADS, seq, SWA_HEAD_DIM), lambda b: (b, 0, 0, 0)),
                  pl.BlockSpec((seq, kvw), lambda b: (b, 0)),
                  _resident(bias.shape)],
        out_specs=pl.BlockSpec((seq, qw), lambda b: (b, 0)),
        out_shape=jax.ShapeDtypeStruct((batch * seq, qw), BF16),
        scratch_shapes=[pltpu.VMEM((WINDOW + seq, SWA_KV_HEADS * SWA_HEAD_DIM), BF16),
                        pltpu.VMEM((1 + seq // WINDOW, SWA_KV_HEADS, SWA_HEAD_DIM + SWA_ONES_ROWS, WINDOW), BF16)],
        compiler_params=_params(1),
        name="swa",
    )(sinks, qa, kva, bias)


def _mla_kernel(q_ref, k_ref, v_ref, o_ref, vt_ref, *, tq):
    heads, seq, dv = v_ref.shape[1], v_ref.shape[2], v_ref.shape[3]
    n_tiles = seq // tq
    key = lax.broadcasted_iota(jnp.int32, (tq, tq), 0)
    qry = lax.broadcasted_iota(jnp.int32, (tq, tq), 1)
    causal = key <= qry
    for hd in range(heads):
        for c in range(seq // WINDOW):
            v_blk = v_ref[0, hd, c * WINDOW:(c + 1) * WINDOW, :]
            vt_ref[hd, 0:dv, c * WINDOW:(c + 1) * WINDOW] = v_blk.astype(F32).T.astype(BF16)
        vt_ref[hd, dv:, :] = jnp.ones((MLA_ONES_ROWS, seq), BF16)
    sts, mxs, pts, accs = {}, {}, {}, {}

    def scores(hd, i, j):
        q = q_ref[0, hd, i * tq:(i + 1) * tq, :]
        s = lax.dot_general(k_ref[0, hd, j * tq:(j + 1) * tq, :], q, NT_DIMS, preferred_element_type=F32)
        sts[hd, i, j] = jnp.where(causal, s, MASK_VALUE) if j == i else s

    def col_max(hd, i, j):
        m = jnp.max(sts[hd, i, j], axis=0, keepdims=True)
        mxs[hd, i] = m if j == 0 else jnp.maximum(mxs[hd, i], m)

    def probs(hd, i, j):
        pts[hd, i, j] = jnp.exp2(sts.pop((hd, i, j)) - mxs[hd, i]).astype(BF16)

    def values(hd, i, j):
        part = _dot(vt_ref[hd, :, j * tq:(j + 1) * tq], pts.pop((hd, i, j)))
        accs[hd, i] = part if j == 0 else accs[hd, i] + part
        if j == i:
            ot = accs.pop((hd, i))
            o = ot[0:dv, :] * (1.0 / ot[dv:dv + 1, :])
            o_ref[i * tq:(i + 1) * tq, hd * dv:(hd + 1) * dv] = o.T.astype(BF16)

    stages = (scores, col_max, probs, values)
    units = [(hd, i) for hd in range(heads) for i in range(n_tiles)]
    for t in range(len(units) + len(stages) - 1):
        for j in range(n_tiles):
            for lag, stage in enumerate(stages):
                if 0 <= t - lag < len(units):
                    hd, i = units[t - lag]
                    if j <= i:
                        stage(hd, i, j)


def _mla(qm, km, vm, *, tq=256):
    batch, heads, seq, qk = qm.shape
    dv = vm.shape[-1]
    return pl.pallas_call(
        functools.partial(_mla_kernel, tq=tq),
        grid=(batch,),
        in_specs=[pl.BlockSpec((1, heads, seq, qk), lambda b: (b, 0, 0, 0)),
                  pl.BlockSpec((1, heads, seq, qk), lambda b: (b, 0, 0, 0)),
                  pl.BlockSpec((1, heads, seq, dv), lambda b: (b, 0, 0, 0))],
        out_specs=pl.BlockSpec((seq, heads * dv), lambda b: (b, 0)),
        out_shape=jax.ShapeDtypeStruct((batch * seq, heads * dv), BF16),
        scratch_shapes=[pltpu.VMEM((heads, dv + MLA_ONES_ROWS, seq), BF16)],
        compiler_params=_params(1),
        name="mla",
    )(qm, km, vm)


def kernel(x, c, w_mod, b_mod, norm_ffn1, ffn1_gate, ffn1_up, ffn1_down, norm_mix, w_in, q_norm, kv_norm, w_uq, w_ukv, sinks, w_o, norm_ffn2, ffn2_gate, ffn2_up, ffn2_down, rel_bias, norm_final):
    batch, seq, d = x.shape
    depth = w_mod.shape[0]
    x2d = x.reshape(batch * seq, d)
    bias = _swa_bias(rel_bias)
    for l in range(depth):
        mod3 = _mod(c, w_mod[l], b_mod[l]).reshape(batch, N_MOD, d)
        x2d = _ffn(x2d, mod3, norm_ffn1[l], ffn1_gate[l], ffn1_up[l], ffn1_down[l], mod_row=0, seq=seq)
        qa, kva, qm, km, vm = _mix_proj(x2d, mod3, norm_mix[l], w_in[l], q_norm[l], kv_norm[l],
                                        w_uq[l], w_ukv[l], batch=batch, seq=seq)
        out_a = _swa(qa, kva, bias, sinks[l], batch=batch, seq=seq)
        out_b = _mla(qm, km, vm)
        last = l == depth - 1
        x2d = _ffn(x2d, mod3, norm_ffn2[l], ffn2_gate[l], ffn2_up[l], ffn2_down[l], mod_row=6, seq=seq,
                   mix=(out_a, out_b, w_o[l].astype(BF16)),
                   final_gain=norm_final if last else None)
    return x2d.reshape(batch, seq, d)
```

```python
import functools
import math

import jax
import jax.numpy as jnp
import numpy as np
from jax import lax
from jax.experimental import pallas as pl
from jax.experimental.pallas import tpu as pltpu

EPS = 1e-6
FFN_RES_WEIGHT = 0.5
N_MOD = 9

SWA_HEADS = 8
SWA_KV_HEADS = 2
SWA_HEAD_DIM = 64
SWA_GROUP = SWA_HEADS // SWA_KV_HEADS
WINDOW = 128

MLA_HEADS = 4
MLA_Q_RANK = 256
MLA_KV_RANK = 128
MLA_NOPE = 128
MLA_ROPE = 64
MLA_V = 128
MLA_QK = MLA_NOPE + MLA_ROPE
ROPE_THETA = 10000.0

NUM_BUCKETS = 32
MAX_DISTANCE = 128

SWA_ONES_ROWS = 16
MLA_ONES_ROWS = 16
LOG2E = math.log2(math.e)
MASK_VALUE = -1e30
V7X_VMEM_LIMIT_BYTES = 56 * 1024 * 1024

F32 = jnp.float32
BF16 = jnp.bfloat16
NT_DIMS = (((1,), (1,)), ((), ()))


def _params(n_grid_dims):
    return pltpu.CompilerParams(dimension_semantics=("arbitrary",) * n_grid_dims,
                                vmem_limit_bytes=V7X_VMEM_LIMIT_BYTES)


def _resident(shape):
    return pl.BlockSpec(shape, lambda *_: (0,) * len(shape), pipeline_mode=pl.Buffered(1))


def _rms_norm(x, gain):
    ms = jnp.mean(x * x, axis=-1, keepdims=True)
    return x * lax.rsqrt(ms + EPS) * gain


def _silu(x):
    return x * (1.0 / (1.0 + jnp.exp(-x)))


def _dot(a, b):
    return jnp.dot(a, b, preferred_element_type=F32)


def _mod_kernel(c_ref, w_ref, b_ref, o_ref):
    c_act = _silu(c_ref[...]).astype(BF16)
    o_ref[...] = _dot(c_act, w_ref[...].astype(BF16)) + b_ref[...]


def _mod(c, w_mod, b_mod, *, tn=1024):
    batch, d = c.shape
    n = w_mod.shape[1]
    return pl.pallas_call(
        _mod_kernel,
        grid=(n // tn,),
        in_specs=[pl.BlockSpec((batch, d), lambda j: (0, 0)),
                  pl.BlockSpec((d, tn), lambda j: (0, j)),
                  pl.BlockSpec((1, tn), lambda j: (0, j))],
        out_specs=pl.BlockSpec((batch, tn), lambda j: (0, j)),
        out_shape=jax.ShapeDtypeStruct((batch, n), F32),
        compiler_params=_params(1),
        name="mod",
    )(c, w_mod, b_mod.reshape(1, n))


def _ffn_kernel(*refs, mod_row, tiles_per_seq, mix_residual, final_norm):
    refs = list(refs)
    x_ref, mod_ref = refs[:2]
    del refs[:2]
    if mix_residual:
        oa_ref, ob_ref, woa_ref, wob_ref = refs[:4]
        del refs[:4]
    g_ref, wg_hbm, wu_hbm, wd_hbm = refs[:4]
    del refs[:4]
    if final_norm:
        gf_ref = refs.pop(0)
    (o_ref, wg_s, wu_s, wd_s, h_even, h_odd, xres_even, xres_odd, a_s,
     stage_gu, stage_d, sem_gu, sem_d) = refs
    n_w, _, fw = wg_s.shape
    n_tiles = pl.num_programs(0) - 2
    t = pl.program_id(0) - 1

    def weight_stage():
        gu_slots, gu_rows = stage_gu.shape[0], stage_gu.shape[1]
        d_slots, d_rows = stage_d.shape[0], stage_d.shape[1]
        gu_chunks = [(src, dst, k) for k in range(wg_hbm.shape[0] // gu_rows)
                     for src, dst in ((wg_hbm, wg_s), (wu_hbm, wu_s))]
        d_chunks = list(range(wd_hbm.shape[0] // d_rows))

        def gu_copy(n):
            src, _, k = gu_chunks[n]
            return pltpu.make_async_copy(src.at[pl.ds(k * gu_rows, gu_rows), :], stage_gu.at[n % gu_slots],
                                         sem_gu.at[n % gu_slots])

        def d_copy(n):
            return pltpu.make_async_copy(wd_hbm.at[pl.ds(d_chunks[n] * d_rows, d_rows), :], stage_d.at[n % d_slots],
                                         sem_d.at[n % d_slots])

        for n in range(gu_slots):
            gu_copy(n).start()
        for n in range(d_slots):
            d_copy(n).start()
        for n in range(len(gu_chunks)):
            gu_copy(n).wait()
            _, dst, k = gu_chunks[n]
            for c in range(n_w):
                dst[c, k * gu_rows:(k + 1) * gu_rows, :] = stage_gu[n % gu_slots, :, c * fw:(c + 1) * fw].astype(BF16)
            if n + gu_slots < len(gu_chunks):
                gu_copy(n + gu_slots).start()
            if n % 2 == 1:
                m = n // 2
                d_copy(m).wait()
                wd_s[d_chunks[m] * d_rows:(d_chunks[m] + 1) * d_rows, :] = stage_d[m % d_slots].astype(BF16)
                if m + d_slots < len(d_chunks):
                    d_copy(m + d_slots).start()

    def norm_stage(xres_next, h_next):
        mod = mod_ref[jnp.minimum(t, n_tiles - 1) // tiles_per_seq]
        x = x_ref[...]
        if mix_residual:
            mix = _dot(oa_ref[...], woa_ref[...]) + _dot(ob_ref[...], wob_ref[...])
            x = x + mod[mod_row - 1:mod_row, :] * mix
        xres_next[...] = x
        gain = g_ref[...] * (1.0 + mod[mod_row + 1:mod_row + 2, :])
        ms = jnp.mean(x * x, axis=-1, keepdims=True)
        h_next[...] = (x * lax.rsqrt(ms + EPS) * gain + mod[mod_row:mod_row + 1, :]).astype(BF16)

    def matmul_stage(xres_cur, h_cur):
        for c in range(n_w):
            g = _dot(h_cur[...], wg_s[c])
            u = _dot(h_cur[...], wu_s[c])
            a_s[:, c * fw:(c + 1) * fw] = (_silu(g) * u).astype(BF16)
        y = _dot(a_s[...], wd_s[...])
        gate = mod_ref[(t - 1) // tiles_per_seq][mod_row + 2:mod_row + 3, :]
        out = xres_cur[...] + FFN_RES_WEIGHT * gate * y
        if final_norm:
            out = _rms_norm(out, gf_ref[...])
        o_ref[...] = out

    @pl.when(t == -1)
    def _():
        weight_stage()

    @pl.when(t == 0)
    def _():
        norm_stage(xres_even, h_even)

    @pl.when(jnp.logical_and(t >= 1, lax.rem(t, 2) == 1))
    def _():
        norm_stage(xres_odd, h_odd)
        matmul_stage(xres_even, h_even)

    @pl.when(jnp.logical_and(t >= 2, lax.rem(t, 2) == 0))
    def _():
        norm_stage(xres_even, h_even)
        matmul_stage(xres_odd, h_odd)


def _ffn(x2d, mod3, gain, wg, wu, wd, *, mod_row, seq, mix=None, final_gain=None, tm=512, fw=256,
         stage_chunks=8, gu_slots=4, d_slots=2):
    m, d = x2d.shape
    f = wg.shape[1]
    assert f % fw == 0 and seq % tm == 0 and d % stage_chunks == 0 and f % stage_chunks == 0
    n_w, n_tiles, tiles_per_seq = f // fw, m // tm, seq // tm
    final_norm = final_gain is not None

    def tile(lag):
        return lambda s: (jnp.clip(s - 1 - lag, 0, n_tiles - 1), 0)

    in_specs = [pl.BlockSpec((tm, d), tile(0)), _resident(mod3.shape)]
    args = [x2d, mod3]
    if mix is not None:
        out_a, out_b, w_o = mix
        wa, wb = out_a.shape[1], out_b.shape[1]
        in_specs += [pl.BlockSpec((tm, wa), tile(0)), pl.BlockSpec((tm, wb), tile(0)),
                     _resident((wa, d)), _resident((wb, d))]
        args += [out_a, out_b, w_o[:wa], w_o[wa:]]
    hbm = pl.BlockSpec(memory_space=pl.ANY)
    in_specs += [_resident((1, d)), hbm, hbm, hbm]
    args += [gain.reshape(1, d), wg, wu, wd]
    if final_norm:
        in_specs.append(_resident((1, d)))
        args.append(final_gain.reshape(1, d))
    return pl.pallas_call(
        functools.partial(_ffn_kernel, mod_row=mod_row, tiles_per_seq=tiles_per_seq,
                          mix_residual=mix is not None, final_norm=final_norm),
        grid=(n_tiles + 2,),
        in_specs=in_specs,
        out_specs=pl.BlockSpec((tm, d), tile(1)),
        out_shape=jax.ShapeDtypeStruct((m, d), F32),
        scratch_shapes=[pltpu.VMEM((n_w, d, fw), BF16), pltpu.VMEM((n_w, d, fw), BF16), pltpu.VMEM((f, d), BF16),
                        pltpu.VMEM((tm, d), BF16), pltpu.VMEM((tm, d), BF16),
                        pltpu.VMEM((tm, d), F32), pltpu.VMEM((tm, d), F32), pltpu.VMEM((tm, f), BF16),
                        pltpu.VMEM((gu_slots, d // stage_chunks, f), F32),
                        pltpu.VMEM((d_slots, f // stage_chunks, d), F32),
                        pltpu.SemaphoreType.DMA((gu_slots,)), pltpu.SemaphoreType.DMA((d_slots,))],
        compiler_params=_params(1),
        name="ffn_final" if final_norm else "ffn",
    )(*args)


_QA0, _KVA0, _QLAT0, _KVLAT0, _KR0, _KRS0, _PROJ_W = 0, 512, 768, 1024, 1152, 1216, 1280
_UQ_ROPE0 = MLA_HEADS * MLA_NOPE
_UQ_ROPES0 = _UQ_ROPE0 + MLA_HEADS * MLA_ROPE


def _mix_proj_kernel(x_ref, mod_ref, g_ref, win_ref, qn_ref, kvn_ref, wuq_ref, wukv_ref, cos_ref, sin_ref,
                     qa_ref, kva_ref, qm_ref, km_ref, vm_ref):
    x = x_ref[...]
    shift = mod_ref[0, 3:4, :]
    scale = mod_ref[0, 4:5, :]
    h = (_rms_norm(x, g_ref[...]) * (1.0 + scale) + shift).astype(BF16)
    proj = _dot(h, win_ref[...])
    for hd in range(SWA_HEADS):
        qa_ref[0, hd] = (proj[:, _QA0 + hd * SWA_HEAD_DIM:_QA0 + (hd + 1) * SWA_HEAD_DIM]
                         * (SWA_HEAD_DIM ** -0.5 * LOG2E)).astype(BF16)
    kva_ref[...] = proj[:, _KVA0:_QLAT0].astype(BF16)

    q_lat = _rms_norm(proj[:, _QLAT0:_KVLAT0], qn_ref[...]).astype(BF16)
    kv_lat = _rms_norm(proj[:, _KVLAT0:_KR0], kvn_ref[...]).astype(BF16)
    q_all = _dot(q_lat, wuq_ref[...])
    kv_all = _dot(kv_lat, wukv_ref[...])

    cos = cos_ref[...]
    sin = sin_ref[...]
    k_rope = (proj[:, _KR0:_KRS0] * cos + proj[:, _KRS0:_PROJ_W] * sin).astype(BF16)
    q_scale = MLA_QK ** -0.5 * LOG2E
    for hd in range(MLA_HEADS):
        q_nope = q_all[:, hd * MLA_NOPE:(hd + 1) * MLA_NOPE]
        q_rope = (q_all[:, _UQ_ROPE0 + hd * MLA_ROPE:_UQ_ROPE0 + (hd + 1) * MLA_ROPE] * cos
                  + q_all[:, _UQ_ROPES0 + hd * MLA_ROPE:_UQ_ROPES0 + (hd + 1) * MLA_ROPE] * sin)
        qm_ref[0, hd, :, 0:MLA_NOPE] = (q_nope * q_scale).astype(BF16)
        qm_ref[0, hd, :, MLA_NOPE:MLA_QK] = (q_rope * q_scale).astype(BF16)
        kv0 = hd * (MLA_NOPE + MLA_V)
        km_ref[0, hd, :, 0:MLA_NOPE] = kv_all[:, kv0:kv0 + MLA_NOPE].astype(BF16)
        km_ref[0, hd, :, MLA_NOPE:MLA_QK] = k_rope
        vm_ref[0, hd] = kv_all[:, kv0 + MLA_NOPE:kv0 + MLA_NOPE + MLA_V].astype(BF16)


def _swap_halves(w):
    half = w.shape[-1] // 2
    return jnp.concatenate([w[..., half:], w[..., :half]], axis=-1)


def _mix_proj(x2d, mod3, gain, w_in, q_norm, kv_norm, w_uq, w_ukv, *, batch, seq, tm=512):
    m, d = x2d.shape
    tiles_per_seq = seq // tm
    w_in_ext = jnp.concatenate([w_in, _swap_halves(w_in[:, _KR0:_KRS0])], axis=1).astype(BF16)
    w_uq_h = w_uq.reshape(MLA_Q_RANK, MLA_HEADS, MLA_QK)
    uq_rope = w_uq_h[:, :, MLA_NOPE:]
    w_uq_r = jnp.concatenate([w_uq_h[:, :, :MLA_NOPE].reshape(MLA_Q_RANK, -1),
                              uq_rope.reshape(MLA_Q_RANK, -1),
                              _swap_halves(uq_rope).reshape(MLA_Q_RANK, -1)], axis=1).astype(BF16)
    w_ukv_b = w_ukv.astype(BF16)
    inv = ROPE_THETA ** (-jnp.arange(0, MLA_ROPE, 2, dtype=F32) / MLA_ROPE)
    ang = jnp.arange(seq, dtype=F32)[:, None] * inv[None, :]
    cos2 = jnp.concatenate([jnp.cos(ang), jnp.cos(ang)], axis=-1)
    sin2 = jnp.concatenate([-jnp.sin(ang), jnp.sin(ang)], axis=-1)

    def head_spec(width):
        return pl.BlockSpec((1, MLA_HEADS, tm, width),
                            lambda i: (i // tiles_per_seq, 0, i % tiles_per_seq, 0))

    def head_shape(width):
        return jax.ShapeDtypeStruct((batch, MLA_HEADS, seq, width), BF16)

    return pl.pallas_call(
        _mix_proj_kernel,
        grid=(m // tm,),
        in_specs=[pl.BlockSpec((tm, d), lambda i: (i, 0)),
                  pl.BlockSpec((1, N_MOD, d), lambda i: (i // tiles_per_seq, 0, 0)),
                  _resident((1, d)),
                  _resident((d, _PROJ_W)),
                  _resident((1, MLA_Q_RANK)), _resident((1, MLA_KV_RANK)),
                  _resident(w_uq_r.shape), _resident(w_ukv_b.shape),
                  pl.BlockSpec((tm, MLA_ROPE), lambda i: (i % tiles_per_seq, 0)),
                  pl.BlockSpec((tm, MLA_ROPE), lambda i: (i % tiles_per_seq, 0))],
        out_specs=[pl.BlockSpec((1, SWA_HEADS, tm, SWA_HEAD_DIM),
                                lambda i: (i // tiles_per_seq, 0, i % tiles_per_seq, 0)),
                   pl.BlockSpec((tm, 2 * SWA_KV_HEADS * SWA_HEAD_DIM), lambda i: (i, 0)),
                   head_spec(MLA_QK), head_spec(MLA_QK), head_spec(MLA_V)],
        out_shape=[jax.ShapeDtypeStruct((batch, SWA_HEADS, seq, SWA_HEAD_DIM), BF16),
                   jax.ShapeDtypeStruct((m, 2 * SWA_KV_HEADS * SWA_HEAD_DIM), BF16),
                   head_shape(MLA_QK), head_shape(MLA_QK), head_shape(MLA_V)],
        compiler_params=_params(1),
        name="mix_proj",
    )(x2d, mod3, gain.reshape(1, d), w_in_ext, q_norm.reshape(1, -1), kv_norm.reshape(1, -1),
      w_uq_r, w_ukv_b, cos2, sin2)


def _t5_bucket_table():
    qi = np.arange(WINDOW)[:, None]
    kj = np.arange(2 * WINDOW)[None, :]
    dist = qi + WINDOW - kj
    max_exact = NUM_BUCKETS // 2
    n = np.maximum(dist, 0)
    nf = np.maximum(n, 1).astype(np.float32)
    large = max_exact + (np.log(nf / np.float32(max_exact)) / np.float32(math.log(MAX_DISTANCE / max_exact))
                         * np.float32(NUM_BUCKETS - max_exact)).astype(np.int32)
    large = np.minimum(large, NUM_BUCKETS - 1)
    bucket = np.where(n < max_exact, n, large)
    band = (dist >= 0) & (dist < WINDOW)
    return np.where(band, bucket, -1).astype(np.int32)


def _swa_bias_kernel(rel_ref, bucket_ref, o_ref):
    bucket = bucket_ref[...]
    first_block_ok = lax.broadcasted_iota(jnp.int32, bucket.shape, 0) >= WINDOW
    for hd in range(SWA_HEADS):
        acc = jnp.full(bucket.shape, MASK_VALUE, F32)
        for b in range(NUM_BUCKETS):
            acc = jnp.where(bucket == b, rel_ref[b, hd] * LOG2E, acc)
        g, j = divmod(hd, SWA_GROUP)
        o_ref[0, g, :, j * WINDOW:(j + 1) * WINDOW] = jnp.where(first_block_ok, acc, MASK_VALUE)
        o_ref[1, g, :, j * WINDOW:(j + 1) * WINDOW] = acc


def _swa_bias(rel_bias):
    shape = (2, SWA_KV_HEADS, 2 * WINDOW, SWA_GROUP * WINDOW)
    return pl.pallas_call(
        _swa_bias_kernel,
        in_specs=[pl.BlockSpec(memory_space=pltpu.SMEM),
                  pl.BlockSpec((2 * WINDOW, WINDOW), lambda: (0, 0))],
        out_specs=pl.BlockSpec(shape, lambda: (0, 0, 0, 0)),
        out_shape=jax.ShapeDtypeStruct(shape, F32),
        name="swa_bias",
    )(rel_bias, jnp.asarray(_t5_bucket_table().T))


def _swa_kernel(sink_ref, q_ref, kv_ref, bias_ref, o_ref, kp_ref, vt_ref, *, blocks_per_iter):
    seq = kv_ref.shape[0]
    n_blocks = seq // WINDOW
    kdim = SWA_KV_HEADS * SWA_HEAD_DIM
    kp_ref[0:WINDOW, :] = jnp.zeros((WINDOW, kdim), BF16)
    kp_ref[WINDOW:, :] = kv_ref[:, 0:kdim]
    ones_rows = jnp.ones((SWA_ONES_ROWS, WINDOW), BF16)
    for g in range(SWA_KV_HEADS):
        vt_ref[0, g] = jnp.concatenate([jnp.zeros((SWA_HEAD_DIM, WINDOW), BF16), ones_rows], axis=0)
    for n in range(n_blocks):
        for g in range(SWA_KV_HEADS):
            v_blk = kv_ref[n * WINDOW:(n + 1) * WINDOW, kdim + g * SWA_HEAD_DIM:kdim + (g + 1) * SWA_HEAD_DIM]
            vt_ref[n + 1, g] = jnp.concatenate([v_blk.astype(F32).T.astype(BF16), ones_rows], axis=0)
    sink_rows = [[jnp.full((1, WINDOW), sink_ref[g * SWA_GROUP + j] * LOG2E, F32) for j in range(SWA_GROUP)]
                 for g in range(SWA_KV_HEADS)]

    def body(it, carry):
        units = [(b, g) for b in range(blocks_per_iter) for g in range(SWA_KV_HEADS)]
        blocks = [it * blocks_per_iter + b for b in range(blocks_per_iter)]
        q_starts = [pl.multiple_of(n * WINDOW, WINDOW) for n in blocks]
        tables = [jnp.minimum(n, 1) for n in blocks]
        sts, pts = {}, {}

        def scores(u):
            b, g = units[u]
            k = kp_ref[pl.ds(q_starts[b], 2 * WINDOW), g * SWA_HEAD_DIM:(g + 1) * SWA_HEAD_DIM]
            sts[u] = []
            for j in range(0, SWA_GROUP, 2):
                hd = g * SWA_GROUP + j
                q = jnp.concatenate([q_ref[0, hd, pl.ds(q_starts[b], WINDOW), :],
                                     q_ref[0, hd + 1, pl.ds(q_starts[b], WINDOW), :]], axis=0)
                st = lax.dot_general(k, q, NT_DIMS, preferred_element_type=F32)
                sts[u] += [st[:, 0:WINDOW], st[:, WINDOW:2 * WINDOW]]

        def softmax(u):
            b, g = units[u]
            pts[u] = []
            for j in range(SWA_GROUP):
                st = sts[u][j] + bias_ref[tables[b], g, :, j * WINDOW:(j + 1) * WINDOW]
                mx = jnp.maximum(jnp.max(st, axis=0, keepdims=True), sink_rows[g][j])
                pts[u].append((jnp.exp2(st - mx).astype(BF16), jnp.exp2(sink_rows[g][j] - mx)))

        def values(u):
            b, g = units[u]
            vt = jnp.concatenate([vt_ref[blocks[b], g], vt_ref[blocks[b] + 1, g]], axis=1)
            for j in range(0, SWA_GROUP, 2):
                pt = jnp.concatenate([pts[u][j][0], pts[u][j + 1][0]], axis=1)
                ot = _dot(vt, pt)
                halves = []
                for jj in range(2):
                    lanes = slice(jj * WINDOW, (jj + 1) * WINDOW)
                    denom = ot[SWA_HEAD_DIM:SWA_HEAD_DIM + 1, lanes] + pts[u][j + jj][1]
                    halves.append(ot[0:SWA_HEAD_DIM, lanes] * (1.0 / denom))
                pair = jnp.concatenate(halves, axis=0).T
                lane0 = (g * SWA_GROUP + j) * SWA_HEAD_DIM
                o_ref[pl.ds(q_starts[b], WINDOW), lane0:lane0 + 2 * SWA_HEAD_DIM] = pair.astype(BF16)

        for t in range(len(units) + 2):
            if t < len(units):
                scores(t)
            if 0 <= t - 1 < len(units):
                softmax(t - 1)
            if 0 <= t - 2 < len(units):
                values(t - 2)
        return carry

    lax.fori_loop(0, n_blocks // blocks_per_iter, body, 0)


def _swa(qa, kva, bias, sinks, *, batch, seq, blocks_per_iter=8):
    kvw = kva.shape[1]
    qw = SWA_HEADS * SWA_HEAD_DIM
    assert (seq // WINDOW) % blocks_per_iter == 0
    return pl.pallas_call(
        functools.partial(_swa_kernel, blocks_per_iter=blocks_per_iter),
        grid=(batch,),
        in_specs=[pl.BlockSpec(memory_space=pltpu.SMEM),
                  pl.BlockSpec((1, SWA_HEADS, seq, SWA_HEAD_DIM), lambda b: (b, 0, 0, 0)),
                  pl.BlockSpec((seq, kvw), lambda b: (b, 0)),
                  _resident(bias.shape)],
        out_specs=pl.BlockSpec((seq, qw), lambda b: (b, 0)),
        out_shape=jax.ShapeDtypeStruct((batch * seq, qw), BF16),
        scratch_shapes=[pltpu.VMEM((WINDOW + seq, SWA_KV_HEADS * SWA_HEAD_DIM), BF16),
                        pltpu.VMEM((1 + seq // WINDOW, SWA_KV_HEADS, SWA_HEAD_DIM + SWA_ONES_ROWS, WINDOW), BF16)],
        compiler_params=_params(1),
        name="swa",
    )(sinks, qa, kva, bias)


def _mla_kernel(q_ref, k_ref, v_ref, o_ref, vt_ref, *, tq):
    heads, seq, dv = v_ref.shape[1], v_ref.shape[2], v_ref.shape[3]
    n_tiles = seq // tq
    key = lax.broadcasted_iota(jnp.int32, (tq, tq), 0)
    qry = lax.broadcasted_iota(jnp.int32, (tq, tq), 1)
    causal = key <= qry
    for hd in range(heads):
        for c in range(seq // WINDOW):
            v_blk = v_ref[0, hd, c * WINDOW:(c + 1) * WINDOW, :]
            vt_ref[hd, 0:dv, c * WINDOW:(c + 1) * WINDOW] = v_blk.astype(F32).T.astype(BF16)
        vt_ref[hd, dv:, :] = jnp.ones((MLA_ONES_ROWS, seq), BF16)
    sts, mxs, pts, accs = {}, {}, {}, {}

    def scores(hd, i, j):
        q = q_ref[0, hd, i * tq:(i + 1) * tq, :]
        s = lax.dot_general(k_ref[0, hd, j * tq:(j + 1) * tq, :], q, NT_DIMS, preferred_element_type=F32)
        sts[hd, i, j] = jnp.where(causal, s, MASK_VALUE) if j == i else s

    def col_max(hd, i, j):
        m = jnp.max(sts[hd, i, j], axis=0, keepdims=True)
        mxs[hd, i] = m if j == 0 else jnp.maximum(mxs[hd, i], m)

    def probs(hd, i, j):
        pts[hd, i, j] = jnp.exp2(sts.pop((hd, i, j)) - mxs[hd, i]).astype(BF16)

    def values(hd, i, j):
        part = _dot(vt_ref[hd, :, j * tq:(j + 1) * tq], pts.pop((hd, i, j)))
        accs[hd, i] = part if j == 0 else accs[hd, i] + part
        if j == i:
            ot = accs.pop((hd, i))
            o = ot[0:dv, :] * (1.0 / ot[dv:dv + 1, :])
            o_ref[i * tq:(i + 1) * tq, hd * dv:(hd + 1) * dv] = o.T.astype(BF16)

    stages = (scores, col_max, probs, values)
    units = [(hd, i) for hd in range(heads) for i in range(n_tiles)]
    for t in range(len(units) + len(stages) - 1):
        for j in range(n_tiles):
            for lag, stage in enumerate(stages):
                if 0 <= t - lag < len(units):
                    hd, i = units[t - lag]
                    if j <= i:
                        stage(hd, i, j)


def _mla(qm, km, vm, *, tq=256):
    batch, heads, seq, qk = qm.shape
    dv = vm.shape[-1]
    return pl.pallas_call(
        functools.partial(_mla_kernel, tq=tq),
        grid=(batch,),
        in_specs=[pl.BlockSpec((1, heads, seq, qk), lambda b: (b, 0, 0, 0)),
                  pl.BlockSpec((1, heads, seq, qk), lambda b: (b, 0, 0, 0)),
                  pl.BlockSpec((1, heads, seq, dv), lambda b: (b, 0, 0, 0))],
        out_specs=pl.BlockSpec((seq, heads * dv), lambda b: (b, 0)),
        out_shape=jax.ShapeDtypeStruct((batch * seq, heads * dv), BF16),
        scratch_shapes=[pltpu.VMEM((heads, dv + MLA_ONES_ROWS, seq), BF16)],
        compiler_params=_params(1),
        name="mla",
    )(qm, km, vm)


def kernel(x, c, w_mod, b_mod, norm_ffn1, ffn1_gate, ffn1_up, ffn1_down, norm_mix, w_in, q_norm, kv_norm, w_uq, w_ukv, sinks, w_o, norm_ffn2, ffn2_gate, ffn2_up, ffn2_down, rel_bias, norm_final):
    batch, seq, d = x.shape
    depth = w_mod.shape[0]
    x2d = x.reshape(batch * seq, d)
    bias = _swa_bias(rel_bias)
    for l in range(depth):
        mod3 = _mod(c, w_mod[l], b_mod[l]).reshape(batch, N_MOD, d)
        x2d = _ffn(x2d, mod3, norm_ffn1[l], ffn1_gate[l], ffn1_up[l], ffn1_down[l], mod_row=0, seq=seq)
        qa, kva, qm, km, vm = _mix_proj(x2d, mod3, norm_mix[l], w_in[l], q_norm[l], kv_norm[l],
                                        w_uq[l], w_ukv[l], batch=batch, seq=seq)
        out_a = _swa(qa, kva, bias, sinks[l], batch=batch, seq=seq)
        out_b = _mla(qm, km, vm)
        last = l == depth - 1
        x2d = _ffn(x2d, mod3, norm_ffn2[l], ffn2_gate[l], ffn2_up[l], ffn2_down[l], mod_row=6, seq=seq,
                   mix=(out_a, out_b, w_o[l].astype(BF16)),
                   final_gain=norm_final if last else None)
    return x2d.reshape(batch, seq, d)
```

```python
import functools
import math

import jax
import jax.numpy as jnp
import numpy as np
from jax import lax
from jax.experimental import pallas as pl
from jax.experimental.pallas import tpu as pltpu

EPS = 1e-6
FFN_RES_WEIGHT = 0.5
N_MOD = 9

SWA_HEADS = 8
SWA_KV_HEADS = 2
SWA_HEAD_DIM = 64
SWA_GROUP = SWA_HEADS // SWA_KV_HEADS
WINDOW = 128

MLA_HEADS = 4
MLA_Q_RANK = 256
MLA_KV_RANK = 128
MLA_NOPE = 128
MLA_ROPE = 64
MLA_V = 128
MLA_QK = MLA_NOPE + MLA_ROPE
ROPE_THETA = 10000.0

NUM_BUCKETS = 32
MAX_DISTANCE = 128

SWA_ONES_ROWS = 16
MLA_ONES_ROWS = 16
LOG2E = math.log2(math.e)
MASK_VALUE = -1e30
V7X_VMEM_LIMIT_BYTES = 56 * 1024 * 1024

F32 = jnp.float32
BF16 = jnp.bfloat16
NT_DIMS = (((1,), (1,)), ((), ()))


def _params(n_grid_dims):
    return pltpu.CompilerParams(dimension_semantics=("arbitrary",) * n_grid_dims,
                                vmem_limit_bytes=V7X_VMEM_LIMIT_BYTES)


def _resident(shape):
    return pl.BlockSpec(shape, lambda *_: (0,) * len(shape), pipeline_mode=pl.Buffered(1))


def _rms_norm(x, gain):
    ms = jnp.mean(x * x, axis=-1, keepdims=True)
    return x * lax.rsqrt(ms + EPS) * gain


def _silu(x):
    return x * (1.0 / (1.0 + jnp.exp(-x)))


def _dot(a, b):
    return jnp.dot(a, b, preferred_element_type=F32)


def _mod_kernel(c_ref, w_ref, b_ref, o_ref):
    c_act = _silu(c_ref[...]).astype(BF16)
    res = _dot(c_act, w_ref[...].astype(BF16)) + b_ref[...]
    d = o_ref.shape[2]
    for r in range(o_ref.shape[0]):
        o_ref[r] = res[:, r * d:(r + 1) * d]


def _mod(c, w_mod, b_mod, *, rows_per_step=3):
    batch, d = c.shape
    n = w_mod.shape[1]
    tn = rows_per_step * d
    return pl.pallas_call(
        _mod_kernel,
        grid=(n // tn,),
        in_specs=[pl.BlockSpec((batch, d), lambda j: (0, 0)),
                  pl.BlockSpec((d, tn), lambda j: (0, j)),
                  pl.BlockSpec((1, tn), lambda j: (0, j))],
        out_specs=pl.BlockSpec((rows_per_step, batch, d), lambda j: (j, 0, 0)),
        out_shape=jax.ShapeDtypeStruct((n // d, batch, d), F32),
        compiler_params=_params(1),
        name="mod",
    )(c, w_mod, b_mod.reshape(1, n))


def _mod_row(mod_ref, row, b):
    return mod_ref[row, pl.ds(b, 1), :]


def _ffn_kernel(*refs, mod_row, tiles_per_seq, mix_residual, final_norm):
    refs = list(refs)
    x_ref, mod_ref = refs[:2]
    del refs[:2]
    if mix_residual:
        oa_ref, ob_ref, wo_ref = refs[:3]
        del refs[:3]
    g_ref, wg_hbm, wu_hbm, wd_hbm = refs[:4]
    del refs[:4]
    if final_norm:
        gf_ref = refs.pop(0)
    (o_ref, wg_s, wu_s, wd_s, h_even, h_odd, xres_even, xres_odd, a_s,
     stage_gu, stage_d, sem_gu, sem_d) = refs
    n_w, _, fw = wg_s.shape
    n_tiles = pl.num_programs(0) - 2
    t = pl.program_id(0) - 1

    def weight_stage():
        gu_slots, gu_rows = stage_gu.shape[0], stage_gu.shape[1]
        d_slots, d_rows = stage_d.shape[0], stage_d.shape[1]
        gu_chunks = [(src, dst, k) for k in range(wg_hbm.shape[0] // gu_rows)
                     for src, dst in ((wg_hbm, wg_s), (wu_hbm, wu_s))]
        d_chunks = list(range(wd_hbm.shape[0] // d_rows))

        def gu_copy(n):
            src, _, k = gu_chunks[n]
            return pltpu.make_async_copy(src.at[pl.ds(k * gu_rows, gu_rows), :], stage_gu.at[n % gu_slots],
                                         sem_gu.at[n % gu_slots])

        def d_copy(n):
            return pltpu.make_async_copy(wd_hbm.at[pl.ds(d_chunks[n] * d_rows, d_rows), :], stage_d.at[n % d_slots],
                                         sem_d.at[n % d_slots])

        for n in range(gu_slots):
            gu_copy(n).start()
        for n in range(d_slots):
            d_copy(n).start()
        for n in range(len(gu_chunks)):
            gu_copy(n).wait()
            _, dst, k = gu_chunks[n]
            for c in range(n_w):
                dst[c, k * gu_rows:(k + 1) * gu_rows, :] = stage_gu[n % gu_slots, :, c * fw:(c + 1) * fw].astype(BF16)
            if n + gu_slots < len(gu_chunks):
                gu_copy(n + gu_slots).start()
            if n % 2 == 1:
                m = n // 2
                d_copy(m).wait()
                wd_s[d_chunks[m] * d_rows:(d_chunks[m] + 1) * d_rows, :] = stage_d[m % d_slots].astype(BF16)
                if m + d_slots < len(d_chunks):
                    d_copy(m + d_slots).start()

    def norm_stage(xres_next, h_next):
        b = jnp.minimum(t, n_tiles - 1) // tiles_per_seq
        x = x_ref[...]
        if mix_residual:
            wa = oa_ref.shape[1]
            mix = _dot(oa_ref[...], wo_ref[0:wa, :]) + _dot(ob_ref[...], wo_ref[wa:, :])
            x = x + _mod_row(mod_ref, mod_row - 1, b) * mix
        xres_next[...] = x
        gain = g_ref[...] * (1.0 + _mod_row(mod_ref, mod_row + 1, b))
        ms = jnp.mean(x * x, axis=-1, keepdims=True)
        h_next[...] = (x * lax.rsqrt(ms + EPS) * gain + _mod_row(mod_ref, mod_row, b)).astype(BF16)

    def matmul_stage(xres_cur, h_cur):
        for c in range(n_w):
            g = _dot(h_cur[...], wg_s[c])
            u = _dot(h_cur[...], wu_s[c])
            a_s[:, c * fw:(c + 1) * fw] = (_silu(g) * u).astype(BF16)
        y = _dot(a_s[...], wd_s[...])
        gate = _mod_row(mod_ref, mod_row + 2, (t - 1) // tiles_per_seq)
        out = xres_cur[...] + FFN_RES_WEIGHT * gate * y
        if final_norm:
            out = _rms_norm(out, gf_ref[...])
        o_ref[...] = out

    @pl.when(t == -1)
    def _():
        weight_stage()

    @pl.when(t == 0)
    def _():
        norm_stage(xres_even, h_even)

    @pl.when(jnp.logical_and(t >= 1, lax.rem(t, 2) == 1))
    def _():
        norm_stage(xres_odd, h_odd)
        matmul_stage(xres_even, h_even)

    @pl.when(jnp.logical_and(t >= 2, lax.rem(t, 2) == 0))
    def _():
        norm_stage(xres_even, h_even)
        matmul_stage(xres_odd, h_odd)


def _ffn(x2d, mod3, gain, wg, wu, wd, *, mod_row, seq, mix=None, final_gain=None, tm=512, fw=256,
         stage_chunks=8, gu_slots=4, d_slots=2):
    m, d = x2d.shape
    f = wg.shape[1]
    assert f % fw == 0 and seq % tm == 0 and d % stage_chunks == 0 and f % stage_chunks == 0
    n_w, n_tiles, tiles_per_seq = f // fw, m // tm, seq // tm
    final_norm = final_gain is not None

    def tile(lag):
        return lambda s: (jnp.clip(s - 1 - lag, 0, n_tiles - 1), 0)

    in_specs = [pl.BlockSpec((tm, d), tile(0)), _resident(mod3.shape)]
    args = [x2d, mod3]
    if mix is not None:
        out_a, out_b, w_o = mix
        wa, wb = out_a.shape[1], out_b.shape[1]
        in_specs += [pl.BlockSpec((tm, wa), tile(0)), pl.BlockSpec((tm, wb), tile(0)), _resident((wa + wb, d))]
        args += [out_a, out_b, w_o]
    hbm = pl.BlockSpec(memory_space=pl.ANY)
    in_specs += [_resident((1, d)), hbm, hbm, hbm]
    args += [gain.reshape(1, d), wg, wu, wd]
    if final_norm:
        in_specs.append(_resident((1, d)))
        args.append(final_gain.reshape(1, d))
    return pl.pallas_call(
        functools.partial(_ffn_kernel, mod_row=mod_row, tiles_per_seq=tiles_per_seq,
                          mix_residual=mix is not None, final_norm=final_norm),
        grid=(n_tiles + 2,),
        in_specs=in_specs,
        out_specs=pl.BlockSpec((tm, d), tile(1)),
        out_shape=jax.ShapeDtypeStruct((m, d), F32),
        scratch_shapes=[pltpu.VMEM((n_w, d, fw), BF16), pltpu.VMEM((n_w, d, fw), BF16), pltpu.VMEM((f, d), BF16),
                        pltpu.VMEM((tm, d), BF16), pltpu.VMEM((tm, d), BF16),
                        pltpu.VMEM((tm, d), F32), pltpu.VMEM((tm, d), F32), pltpu.VMEM((tm, f), BF16),
                        pltpu.VMEM((gu_slots, d // stage_chunks, f), F32),
                        pltpu.VMEM((d_slots, f // stage_chunks, d), F32),
                        pltpu.SemaphoreType.DMA((gu_slots,)), pltpu.SemaphoreType.DMA((d_slots,))],
        compiler_params=_params(1),
        name="ffn_final" if final_norm else "ffn",
    )(*args)


_QA0, _KVA0, _QLAT0, _KVLAT0, _KR0, _KRS0, _PROJ_W = 0, 512, 768, 1024, 1152, 1216, 1280
_UQ_ROPE0 = MLA_HEADS * MLA_NOPE
_UQ_ROPES0 = _UQ_ROPE0 + MLA_HEADS * MLA_ROPE


def _mix_proj_kernel(x_ref, mod_ref, g_ref, win_ref, qn_ref, kvn_ref, wuq_ref, wukv_ref, cos_ref, sin_ref,
                     qa_ref, kva_ref, qm_ref, km_ref, vm_ref, win_s, *, tiles_per_seq):
    @pl.when(pl.program_id(0) == 0)
    def _():
        w = win_ref[...]
        half = MLA_ROPE // 2
        win_s[:, 0:_KRS0] = w.astype(BF16)
        win_s[:, _KRS0:_KRS0 + half] = w[:, _KR0 + half:_KRS0].astype(BF16)
        win_s[:, _KRS0 + half:_PROJ_W] = w[:, _KR0:_KR0 + half].astype(BF16)

    x = x_ref[...]
    b = pl.program_id(0) // tiles_per_seq
    shift = _mod_row(mod_ref, 3, b)
    scale = _mod_row(mod_ref, 4, b)
    h = (_rms_norm(x, g_ref[...]) * (1.0 + scale) + shift).astype(BF16)
    proj = _dot(h, win_s[...])
    for hd in range(SWA_HEADS):
        qa_ref[0, hd] = (proj[:, _QA0 + hd * SWA_HEAD_DIM:_QA0 + (hd + 1) * SWA_HEAD_DIM]
                         * (SWA_HEAD_DIM ** -0.5 * LOG2E)).astype(BF16)
    kva_ref[...] = proj[:, _KVA0:_QLAT0].astype(BF16)

    q_lat = _rms_norm(proj[:, _QLAT0:_KVLAT0], qn_ref[...]).astype(BF16)
    kv_lat = _rms_norm(proj[:, _KVLAT0:_KR0], kvn_ref[...]).astype(BF16)
    q_all = _dot(q_lat, wuq_ref[...])
    kv_all = _dot(kv_lat, wukv_ref[...])

    cos = cos_ref[...]
    sin = sin_ref[...]
    k_rope = (proj[:, _KR0:_KRS0] * cos + proj[:, _KRS0:_PROJ_W] * sin).astype(BF16)
    q_scale = MLA_QK ** -0.5 * LOG2E
    for hd in range(MLA_HEADS):
        q_nope = q_all[:, hd * MLA_NOPE:(hd + 1) * MLA_NOPE]
        q_rope = (q_all[:, _UQ_ROPE0 + hd * MLA_ROPE:_UQ_ROPE0 + (hd + 1) * MLA_ROPE] * cos
                  + q_all[:, _UQ_ROPES0 + hd * MLA_ROPE:_UQ_ROPES0 + (hd + 1) * MLA_ROPE] * sin)
        qm_ref[0, hd, :, 0:MLA_NOPE] = (q_nope * q_scale).astype(BF16)
        qm_ref[0, hd, :, MLA_NOPE:MLA_QK] = (q_rope * q_scale).astype(BF16)
        kv0 = hd * (MLA_NOPE + MLA_V)
        km_ref[0, hd, :, 0:MLA_NOPE] = kv_all[:, kv0:kv0 + MLA_NOPE].astype(BF16)
        km_ref[0, hd, :, MLA_NOPE:MLA_QK] = k_rope
        vm_ref[0, hd] = kv_all[:, kv0 + MLA_NOPE:kv0 + MLA_NOPE + MLA_V].astype(BF16)


def _rope_tables(seq):
    inv = np.float32(ROPE_THETA) ** (-np.arange(0, MLA_ROPE, 2, dtype=np.float32) / np.float32(MLA_ROPE))
    ang = np.arange(seq, dtype=np.float32)[:, None] * inv[None, :].astype(np.float32)
    cos, sin = np.cos(ang).astype(np.float32), np.sin(ang).astype(np.float32)
    return np.concatenate([cos, cos], axis=-1), np.concatenate([-sin, sin], axis=-1)


def _swap_halves(w):
    half = w.shape[-1] // 2
    return jnp.concatenate([w[..., half:], w[..., :half]], axis=-1)


def _mix_proj(x2d, mod3, gain, w_in, q_norm, kv_norm, w_uq, w_ukv, *, batch, seq, tm=512):
    m, d = x2d.shape
    tiles_per_seq = seq // tm
    w_uq_h = w_uq.reshape(MLA_Q_RANK, MLA_HEADS, MLA_QK)
    uq_rope = w_uq_h[:, :, MLA_NOPE:]
    w_uq_r = jnp.concatenate([w_uq_h[:, :, :MLA_NOPE].reshape(MLA_Q_RANK, -1),
                              uq_rope.reshape(MLA_Q_RANK, -1),
                              _swap_halves(uq_rope).reshape(MLA_Q_RANK, -1)], axis=1).astype(BF16)
    w_ukv_b = w_ukv.astype(BF16)
    cos2, sin2 = (jnp.asarray(tab) for tab in _rope_tables(seq))

    def head_spec(width):
        return pl.BlockSpec((1, MLA_HEADS, tm, width),
                            lambda i: (i // tiles_per_seq, 0, i % tiles_per_seq, 0))

    def head_shape(width):
        return jax.ShapeDtypeStruct((batch, MLA_HEADS, seq, width), BF16)

    return pl.pallas_call(
        functools.partial(_mix_proj_kernel, tiles_per_seq=tiles_per_seq),
        grid=(m // tm,),
        in_specs=[pl.BlockSpec((tm, d), lambda i: (i, 0)),
                  _resident(mod3.shape),
                  _resident((1, d)),
                  _resident(w_in.shape),
                  _resident((1, MLA_Q_RANK)), _resident((1, MLA_KV_RANK)),
                  _resident(w_uq_r.shape), _resident(w_ukv_b.shape),
                  pl.BlockSpec((tm, MLA_ROPE), lambda i: (i % tiles_per_seq, 0)),
                  pl.BlockSpec((tm, MLA_ROPE), lambda i: (i % tiles_per_seq, 0))],
        out_specs=[pl.BlockSpec((1, SWA_HEADS, tm, SWA_HEAD_DIM),
                                lambda i: (i // tiles_per_seq, 0, i % tiles_per_seq, 0)),
                   pl.BlockSpec((tm, 2 * SWA_KV_HEADS * SWA_HEAD_DIM), lambda i: (i, 0)),
                   head_spec(MLA_QK), head_spec(MLA_QK), head_spec(MLA_V)],
        out_shape=[jax.ShapeDtypeStruct((batch, SWA_HEADS, seq, SWA_HEAD_DIM), BF16),
                   jax.ShapeDtypeStruct((m, 2 * SWA_KV_HEADS * SWA_HEAD_DIM), BF16),
                   head_shape(MLA_QK), head_shape(MLA_QK), head_shape(MLA_V)],
        scratch_shapes=[pltpu.VMEM((d, _PROJ_W), BF16)],
        compiler_params=_params(1),
        name="mix_proj",
    )(x2d, mod3, gain.reshape(1, d), w_in, q_norm.reshape(1, -1), kv_norm.reshape(1, -1),
      w_uq_r, w_ukv_b, cos2, sin2)


def _t5_bucket_table():
    qi = np.arange(WINDOW)[:, None]
    kj = np.arange(2 * WINDOW)[None, :]
    dist = qi + WINDOW - kj
    max_exact = NUM_BUCKETS // 2
    n = np.maximum(dist, 0)
    nf = np.maximum(n, 1).astype(np.float32)
    large = max_exact + (np.log(nf / np.float32(max_exact)) / np.float32(math.log(MAX_DISTANCE / max_exact))
                         * np.float32(NUM_BUCKETS - max_exact)).astype(np.int32)
    large = np.minimum(large, NUM_BUCKETS - 1)
    bucket = np.where(n < max_exact, n, large)
    band = (dist >= 0) & (dist < WINDOW)
    return np.where(band, bucket, -1).astype(np.int32)


def _swa_bias_kernel(rel_ref, bucket_ref, o_ref):
    bucket = bucket_ref[...]
    first_block_ok = lax.broadcasted_iota(jnp.int32, bucket.shape, 0) >= WINDOW
    for hd in range(SWA_HEADS):
        acc = jnp.full(bucket.shape, MASK_VALUE, F32)
        for b in range(NUM_BUCKETS):
            acc = jnp.where(bucket == b, rel_ref[b, hd] * LOG2E, acc)
        g, j = divmod(hd, SWA_GROUP)
        o_ref[0, g, :, j * WINDOW:(j + 1) * WINDOW] = jnp.where(first_block_ok, acc, MASK_VALUE)
        o_ref[1, g, :, j * WINDOW:(j + 1) * WINDOW] = acc


def _swa_bias(rel_bias):
    shape = (2, SWA_KV_HEADS, 2 * WINDOW, SWA_GROUP * WINDOW)
    return pl.pallas_call(
        _swa_bias_kernel,
        in_specs=[pl.BlockSpec(memory_space=pltpu.SMEM),
                  pl.BlockSpec((2 * WINDOW, WINDOW), lambda: (0, 0))],
        out_specs=pl.BlockSpec(shape, lambda: (0, 0, 0, 0)),
        out_shape=jax.ShapeDtypeStruct(shape, F32),
        name="swa_bias",
    )(rel_bias, jnp.asarray(_t5_bucket_table().T))


def _swa_kernel(sink_ref, q_ref, kv_ref, bias_ref, o_ref, kp_ref, vt_ref, *, blocks_per_iter):
    seq = kv_ref.shape[0]
    n_blocks = seq // WINDOW
    kdim = SWA_KV_HEADS * SWA_HEAD_DIM
    kp_ref[0:WINDOW, :] = jnp.zeros((WINDOW, kdim), BF16)
    kp_ref[WINDOW:, :] = kv_ref[:, 0:kdim]
    ones_rows = jnp.ones((SWA_ONES_ROWS, WINDOW), BF16)
    for g in range(SWA_KV_HEADS):
        vt_ref[0, g] = jnp.concatenate([jnp.zeros((SWA_HEAD_DIM, WINDOW), BF16), ones_rows], axis=0)
    for n in range(n_blocks):
        for g in range(SWA_KV_HEADS):
            v_blk = kv_ref[n * WINDOW:(n + 1) * WINDOW, kdim + g * SWA_HEAD_DIM:kdim + (g + 1) * SWA_HEAD_DIM]
            vt_ref[n + 1, g] = jnp.concatenate([v_blk.astype(F32).T.astype(BF16), ones_rows], axis=0)
    sink_rows = [[jnp.full((1, WINDOW), sink_ref[g * SWA_GROUP + j] * LOG2E, F32) for j in range(SWA_GROUP)]
                 for g in range(SWA_KV_HEADS)]

    def body(it, carry):
        units = [(b, g) for b in range(blocks_per_iter) for g in range(SWA_KV_HEADS)]
        blocks = [it * blocks_per_iter + b for b in range(blocks_per_iter)]
        q_starts = [pl.multiple_of(n * WINDOW, WINDOW) for n in blocks]
        tables = [jnp.minimum(n, 1) for n in blocks]
        sts, pts = {}, {}

        def scores(u):
            b, g = units[u]
            k = kp_ref[pl.ds(q_starts[b], 2 * WINDOW), g * SWA_HEAD_DIM:(g + 1) * SWA_HEAD_DIM]
            sts[u] = []
            for j in range(0, SWA_GROUP, 2):
                hd = g * SWA_GROUP + j
                q = jnp.concatenate([q_ref[0, hd, pl.ds(q_starts[b], WINDOW), :],
                                     q_ref[0, hd + 1, pl.ds(q_starts[b], WINDOW), :]], axis=0)
                st = lax.dot_general(k, q, NT_DIMS, preferred_element_type=F32)
                sts[u] += [st[:, 0:WINDOW], st[:, WINDOW:2 * WINDOW]]

        def softmax(u):
            b, g = units[u]
            pts[u] = []
            for j in range(SWA_GROUP):
                st = sts[u][j] + bias_ref[tables[b], g, :, j * WINDOW:(j + 1) * WINDOW]
                mx = jnp.maximum(jnp.max(st, axis=0, keepdims=True), sink_rows[g][j])
                pts[u].append((jnp.exp2(st - mx).astype(BF16), jnp.exp2(sink_rows[g][j] - mx)))

        def values(u):
            b, g = units[u]
            vt = jnp.concatenate([vt_ref[blocks[b], g], vt_ref[blocks[b] + 1, g]], axis=1)
            for j in range(0, SWA_GROUP, 2):
                pt = jnp.concatenate([pts[u][j][0], pts[u][j + 1][0]], axis=1)
                ot = _dot(vt, pt)
                halves = []
                for jj in range(2):
                    lanes = slice(jj * WINDOW, (jj + 1) * WINDOW)
                    denom = ot[SWA_HEAD_DIM:SWA_HEAD_DIM + 1, lanes] + pts[u][j + jj][1]
                    halves.append(ot[0:SWA_HEAD_DIM, lanes] * (1.0 / denom))
                pair = jnp.concatenate(halves, axis=0).T
                lane0 = (g * SWA_GROUP + j) * SWA_HEAD_DIM
                o_ref[pl.ds(q_starts[b], WINDOW), lane0:lane0 + 2 * SWA_HEAD_DIM] = pair.astype(BF16)

        for t in range(len(units) + 2):
            if t < len(units):
                scores(t)
            if 0 <= t - 1 < len(units):
                softmax(t - 1)
            if 0 <= t - 2 < len(units):
                values(t - 2)
        return carry

    lax.fori_loop(0, n_blocks // blocks_per_iter, body, 0)


def _swa(qa, kva, bias, sinks, *, batch, seq, blocks_per_iter=8):
    kvw = kva.shape[1]
    qw = SWA_HEADS * SWA_HEAD_DIM
    assert (seq // WINDOW) % blocks_per_iter == 0
    return pl.pallas_call(
        functools.partial(_swa_kernel, blocks_per_iter=blocks_per_iter),
        grid=(batch,),
        in_specs=[pl.BlockSpec(memory_space=pltpu.SMEM),
                  pl.BlockSpec((1, SWA_HEADS, seq, SWA_HEAD_DIM), lambda b: (b, 0, 0, 0)),
                  pl.BlockSpec((seq, kvw), lambda b: (b, 0)),
                  _resident(bias.shape)],
        out_specs=pl.BlockSpec((seq, qw), lambda b: (b, 0)),
        out_shape=jax.ShapeDtypeStruct((batch * seq, qw), BF16),
        scratch_shapes=[pltpu.VMEM((WINDOW + seq, SWA_KV_HEADS * SWA_HEAD_DIM), BF16),
                        pltpu.VMEM((1 + seq // WINDOW, SWA_KV_HEADS, SWA_HEAD_DIM + SWA_ONES_ROWS, WINDOW), BF16)],
        compiler_params=_params(1),
        name="swa",
    )(sinks, qa, kva, bias)


def _mla_kernel(q_ref, k_ref, v_ref, o_ref, vt_ref, *, tq):
    heads, seq, dv = v_ref.shape[1], v_ref.shape[2], v_ref.shape[3]
    n_tiles = seq // tq
    key = lax.broadcasted_iota(jnp.int32, (tq, tq), 0)
    qry = lax.broadcasted_iota(jnp.int32, (tq, tq), 1)
    causal = key <= qry
    for hd in range(heads):
        for c in range(seq // WINDOW):
            v_blk = v_ref[0, hd, c * WINDOW:(c + 1) * WINDOW, :]
            vt_ref[hd, 0:dv, c * WINDOW:(c + 1) * WINDOW] = v_blk.astype(F32).T.astype(BF16)
        vt_ref[hd, dv:, :] = jnp.ones((MLA_ONES_ROWS, seq), BF16)
    sts, mxs, pts, accs = {}, {}, {}, {}

    def scores(hd, i, j):
        q = q_ref[0, hd, i * tq:(i + 1) * tq, :]
        s = lax.dot_general(k_ref[0, hd, j * tq:(j + 1) * tq, :], q, NT_DIMS, preferred_element_type=F32)
        sts[hd, i, j] = jnp.where(causal, s, MASK_VALUE) if j == i else s

    def col_max(hd, i, j):
        m = jnp.max(sts[hd, i, j], axis=0, keepdims=True)
        mxs[hd, i] = m if j == 0 else jnp.maximum(mxs[hd, i], m)

    def probs(hd, i, j):
        pts[hd, i, j] = jnp.exp2(sts.pop((hd, i, j)) - mxs[hd, i]).astype(BF16)

    def values(hd, i, j):
        part = _dot(vt_ref[hd, :, j * tq:(j + 1) * tq], pts.pop((hd, i, j)))
        accs[hd, i] = part if j == 0 else accs[hd, i] + part
        if j == i:
            ot = accs.pop((hd, i))
            o = ot[0:dv, :] * (1.0 / ot[dv:dv + 1, :])
            o_ref[i * tq:(i + 1) * tq, hd * dv:(hd + 1) * dv] = o.T.astype(BF16)

    stages = (scores, col_max, probs, values)
    units = [(hd, i) for hd in range(heads) for i in range(n_tiles)]
    for t in range(len(units) + len(stages) - 1):
        for j in range(n_tiles):
            for lag, stage in enumerate(stages):
                if 0 <= t - lag < len(units):
                    hd, i = units[t - lag]
                    if j <= i:
                        stage(hd, i, j)


def _mla(qm, km, vm, *, tq=256):
    batch, heads, seq, qk = qm.shape
    dv = vm.shape[-1]
    return pl.pallas_call(
        functools.partial(_mla_kernel, tq=tq),
        grid=(batch,),
        in_specs=[pl.BlockSpec((1, heads, seq, qk), lambda b: (b, 0, 0, 0)),
                  pl.BlockSpec((1, heads, seq, qk), lambda b: (b, 0, 0, 0)),
                  pl.BlockSpec((1, heads, seq, dv), lambda b: (b, 0, 0, 0))],
        out_specs=pl.BlockSpec((seq, heads * dv), lambda b: (b, 0)),
        out_shape=jax.ShapeDtypeStruct((batch * seq, heads * dv), BF16),
        scratch_shapes=[pltpu.VMEM((heads, dv + MLA_ONES_ROWS, seq), BF16)],
        compiler_params=_params(1),
        name="mla",
    )(qm, km, vm)


def kernel(x, c, w_mod, b_mod, norm_ffn1, ffn1_gate, ffn1_up, ffn1_down, norm_mix, w_in, q_norm, kv_norm, w_uq, w_ukv, sinks, w_o, norm_ffn2, ffn2_gate, ffn2_up, ffn2_down, rel_bias, norm_final):
    batch, seq, d = x.shape
    depth = w_mod.shape[0]
    x2d = x.reshape(batch * seq, d)
    bias = _swa_bias(rel_bias)
    for l in range(depth):
        mod3 = _mod(c, w_mod[l], b_mod[l])
        x2d = _ffn(x2d, mod3, norm_ffn1[l], ffn1_gate[l], ffn1_up[l], ffn1_down[l], mod_row=0, seq=seq)
        qa, kva, qm, km, vm = _mix_proj(x2d, mod3, norm_mix[l], w_in[l], q_norm[l], kv_norm[l],
                                        w_uq[l], w_ukv[l], batch=batch, seq=seq)
        out_a = _swa(qa, kva, bias, sinks[l], batch=batch, seq=seq)
        out_b = _mla(qm, km, vm)
        last = l == depth - 1
        x2d = _ffn(x2d, mod3, norm_ffn2[l], ffn2_gate[l], ffn2_up[l], ffn2_down[l], mod_row=6, seq=seq,
                   mix=(out_a, out_b, w_o[l].astype(BF16)),
                   final_gain=norm_final if last else None)
    return x2d.reshape(batch, seq, d)
```

```python
import functools
import math

import jax
import jax.numpy as jnp
import numpy as np
from jax import lax
from jax.experimental import pallas as pl
from jax.experimental.pallas import tpu as pltpu

EPS = 1e-6
FFN_RES_WEIGHT = 0.5
N_MOD = 9

SWA_HEADS = 8
SWA_KV_HEADS = 2
SWA_HEAD_DIM = 64
SWA_GROUP = SWA_HEADS // SWA_KV_HEADS
WINDOW = 128

MLA_HEADS = 4
MLA_Q_RANK = 256
MLA_KV_RANK = 128
MLA_NOPE = 128
MLA_ROPE = 64
MLA_V = 128
MLA_QK = MLA_NOPE + MLA_ROPE
ROPE_THETA = 10000.0

NUM_BUCKETS = 32
MAX_DISTANCE = 128

SWA_ONES_ROWS = 16
MLA_ONES_ROWS = 16
LOG2E = math.log2(math.e)
MASK_VALUE = -1e30
V7X_VMEM_LIMIT_BYTES = 56 * 1024 * 1024

F32 = jnp.float32
BF16 = jnp.bfloat16
NT_DIMS = (((1,), (1,)), ((), ()))


def _params(n_grid_dims):
    return pltpu.CompilerParams(dimension_semantics=("arbitrary",) * n_grid_dims,
                                vmem_limit_bytes=V7X_VMEM_LIMIT_BYTES)


def _resident(shape):
    return pl.BlockSpec(shape, lambda *_: (0,) * len(shape), pipeline_mode=pl.Buffered(1))


def _rms_norm(x, gain):
    ms = jnp.mean(x * x, axis=-1, keepdims=True)
    return x * lax.rsqrt(ms + EPS) * gain


def _silu(x):
    return x * (1.0 / (1.0 + jnp.exp(-x)))


def _dot(a, b):
    return jnp.dot(a, b, preferred_element_type=F32)


def _mod_kernel(c_ref, w_ref, b_ref, o_ref):
    c_act = _silu(c_ref[...]).astype(BF16)
    res = _dot(c_act, w_ref[...].astype(BF16)) + b_ref[...]
    d = o_ref.shape[2]
    for r in range(o_ref.shape[0]):
        o_ref[r] = res[:, r * d:(r + 1) * d]


def _mod(c, w_mod, b_mod, *, rows_per_step=3):
    batch, d = c.shape
    n = w_mod.shape[1]
    tn = rows_per_step * d
    return pl.pallas_call(
        _mod_kernel,
        grid=(n // tn,),
        in_specs=[pl.BlockSpec((batch, d), lambda j: (0, 0)),
                  pl.BlockSpec((d, tn), lambda j: (0, j)),
                  pl.BlockSpec((1, tn), lambda j: (0, j))],
        out_specs=pl.BlockSpec((rows_per_step, batch, d), lambda j: (j, 0, 0)),
        out_shape=jax.ShapeDtypeStruct((n // d, batch, d), F32),
        compiler_params=_params(1),
        name="mod",
    )(c, w_mod, b_mod.reshape(1, n))


def _mod_row(mod_ref, row, b):
    return mod_ref[row, pl.ds(b, 1), :]


def _ffn_kernel(*refs, mod_row, tiles_per_seq, mix_residual, final_norm):
    refs = list(refs)
    x_ref, mod_ref = refs[:2]
    del refs[:2]
    if mix_residual:
        oa_ref, ob_ref, wo_ref = refs[:3]
        del refs[:3]
    g_ref, wg_hbm, wu_hbm, wd_hbm = refs[:4]
    del refs[:4]
    if final_norm:
        gf_ref = refs.pop(0)
    (o_ref, wg_s, wu_s, wd_s, h_even, h_odd, xres_s, a_s,
     stage_gu, stage_d, sem_gu, sem_d) = refs
    n_w, _, fw = wg_s.shape
    tm = h_even.shape[0]
    n_sub = x_ref.shape[0] // tm
    t = pl.program_id(0) - 1

    def weight_stage():
        gu_slots, gu_rows = stage_gu.shape[0], stage_gu.shape[1]
        d_slots, d_rows = stage_d.shape[0], stage_d.shape[1]
        gu_chunks = [(src, dst, k) for k in range(wg_hbm.shape[0] // gu_rows)
                     for src, dst in ((wg_hbm, wg_s), (wu_hbm, wu_s))]
        d_chunks = list(range(wd_hbm.shape[0] // d_rows))

        def gu_copy(n):
            src, _, k = gu_chunks[n]
            return pltpu.make_async_copy(src.at[pl.ds(k * gu_rows, gu_rows), :], stage_gu.at[n % gu_slots],
                                         sem_gu.at[n % gu_slots])

        def d_copy(n):
            return pltpu.make_async_copy(wd_hbm.at[pl.ds(d_chunks[n] * d_rows, d_rows), :], stage_d.at[n % d_slots],
                                         sem_d.at[n % d_slots])

        for n in range(gu_slots):
            gu_copy(n).start()
        for n in range(d_slots):
            d_copy(n).start()
        for n in range(len(gu_chunks)):
            gu_copy(n).wait()
            _, dst, k = gu_chunks[n]
            for c in range(n_w):
                dst[c, k * gu_rows:(k + 1) * gu_rows, :] = stage_gu[n % gu_slots, :, c * fw:(c + 1) * fw].astype(BF16)
            if n + gu_slots < len(gu_chunks):
                gu_copy(n + gu_slots).start()
            if n % 2 == 1:
                m = n // 2
                d_copy(m).wait()
                wd_s[d_chunks[m] * d_rows:(d_chunks[m] + 1) * d_rows, :] = stage_d[m % d_slots].astype(BF16)
                if m + d_slots < len(d_chunks):
                    d_copy(m + d_slots).start()

    def batch_of(sub):
        return (t * n_sub + sub) // tiles_per_seq

    def norm_stage(sub, h_next):
        rows = pl.ds(sub * tm, tm)
        b = batch_of(sub)
        x = x_ref[rows, :]
        if mix_residual:
            wa = oa_ref.shape[1]
            mix = _dot(oa_ref[rows, :], wo_ref[0:wa, :]) + _dot(ob_ref[rows, :], wo_ref[wa:, :])
            x = x + _mod_row(mod_ref, mod_row - 1, b) * mix
            xres_s[rows, :] = x
        gain = g_ref[...] * (1.0 + _mod_row(mod_ref, mod_row + 1, b))
        ms = jnp.mean(x * x, axis=-1, keepdims=True)
        h_next[...] = (x * lax.rsqrt(ms + EPS) * gain + _mod_row(mod_ref, mod_row, b)).astype(BF16)

    def matmul_stage(sub, h_cur):
        rows = pl.ds(sub * tm, tm)
        for c in range(n_w):
            g = _dot(h_cur[...], wg_s[c])
            u = _dot(h_cur[...], wu_s[c])
            a_s[:, c * fw:(c + 1) * fw] = (_silu(g) * u).astype(BF16)
        y = _dot(a_s[...], wd_s[...])
        x = xres_s[rows, :] if mix_residual else x_ref[rows, :]
        out = x + FFN_RES_WEIGHT * _mod_row(mod_ref, mod_row + 2, batch_of(sub)) * y
        if final_norm:
            out = _rms_norm(out, gf_ref[...])
        o_ref[rows, :] = out

    @pl.when(t == -1)
    def _():
        weight_stage()

    @pl.when(t >= 0)
    def _():
        bufs = (h_even, h_odd)
        norm_stage(0, bufs[0])
        for sub in range(n_sub):
            if sub + 1 < n_sub:
                norm_stage(sub + 1, bufs[(sub + 1) % 2])
            matmul_stage(sub, bufs[sub % 2])


def _ffn(x2d, mod3, gain, wg, wu, wd, *, mod_row, seq, mix=None, final_gain=None, tm=512, tiles_per_step=2, fw=256,
         stage_chunks=8, gu_slots=2, d_slots=2):
    m, d = x2d.shape
    f = wg.shape[1]
    bm = tm * tiles_per_step
    assert f % fw == 0 and seq % tm == 0 and m % bm == 0 and d % stage_chunks == 0 and f % stage_chunks == 0
    n_w, n_blocks, tiles_per_seq = f // fw, m // bm, seq // tm
    final_norm = final_gain is not None

    def block(s):
        return (jnp.maximum(s - 1, 0), 0)

    in_specs = [pl.BlockSpec((bm, d), block), _resident(mod3.shape)]
    args = [x2d, mod3]
    if mix is not None:
        out_a, out_b, w_o = mix
        wa, wb = out_a.shape[1], out_b.shape[1]
        in_specs += [pl.BlockSpec((bm, wa), block), pl.BlockSpec((bm, wb), block), _resident((wa + wb, d))]
        args += [out_a, out_b, w_o]
    hbm = pl.BlockSpec(memory_space=pl.ANY)
    in_specs += [_resident((1, d)), hbm, hbm, hbm]
    args += [gain.reshape(1, d), wg, wu, wd]
    if final_norm:
        in_specs.append(_resident((1, d)))
        args.append(final_gain.reshape(1, d))
    xres_rows = bm if mix is not None else 8
    return pl.pallas_call(
        functools.partial(_ffn_kernel, mod_row=mod_row, tiles_per_seq=tiles_per_seq,
                          mix_residual=mix is not None, final_norm=final_norm),
        grid=(n_blocks + 1,),
        in_specs=in_specs,
        out_specs=pl.BlockSpec((bm, d), block),
        out_shape=jax.ShapeDtypeStruct((m, d), F32),
        scratch_shapes=[pltpu.VMEM((n_w, d, fw), BF16), pltpu.VMEM((n_w, d, fw), BF16), pltpu.VMEM((f, d), BF16),
                        pltpu.VMEM((tm, d), BF16), pltpu.VMEM((tm, d), BF16),
                        pltpu.VMEM((xres_rows, d), F32), pltpu.VMEM((tm, f), BF16),
                        pltpu.VMEM((gu_slots, d // stage_chunks, f), F32),
                        pltpu.VMEM((d_slots, f // stage_chunks, d), F32),
                        pltpu.SemaphoreType.DMA((gu_slots,)), pltpu.SemaphoreType.DMA((d_slots,))],
        compiler_params=_params(1),
        name="ffn_final" if final_norm else "ffn",
    )(*args)


_QA0, _KVA0, _QLAT0, _KVLAT0, _KR0, _KRS0, _PROJ_W = 0, 512, 768, 1024, 1152, 1216, 1280
_UQ_ROPE0 = MLA_HEADS * MLA_NOPE
_UQ_ROPES0 = _UQ_ROPE0 + MLA_HEADS * MLA_ROPE


def _mix_proj_kernel(x_ref, mod_ref, g_ref, win_ref, qn_ref, kvn_ref, wuq_ref, wukv_ref, cos_ref, sin_ref,
                     qa_ref, kva_ref, qm_ref, km_ref, vm_ref, win_s, *, tiles_per_seq):
    @pl.when(pl.program_id(0) == 0)
    def _():
        w = win_ref[...]
        half = MLA_ROPE // 2
        win_s[:, 0:_KRS0] = w.astype(BF16)
        win_s[:, _KRS0:_KRS0 + half] = w[:, _KR0 + half:_KRS0].astype(BF16)
        win_s[:, _KRS0 + half:_PROJ_W] = w[:, _KR0:_KR0 + half].astype(BF16)

    x = x_ref[...]
    b = pl.program_id(0) // tiles_per_seq
    shift = _mod_row(mod_ref, 3, b)
    scale = _mod_row(mod_ref, 4, b)
    h = (_rms_norm(x, g_ref[...]) * (1.0 + scale) + shift).astype(BF16)
    proj = _dot(h, win_s[...])
    for hd in range(SWA_HEADS):
        qa_ref[0, hd] = (proj[:, _QA0 + hd * SWA_HEAD_DIM:_QA0 + (hd + 1) * SWA_HEAD_DIM]
                         * (SWA_HEAD_DIM ** -0.5 * LOG2E)).astype(BF16)
    kva_ref[...] = proj[:, _KVA0:_QLAT0].astype(BF16)

    q_lat = _rms_norm(proj[:, _QLAT0:_KVLAT0], qn_ref[...]).astype(BF16)
    kv_lat = _rms_norm(proj[:, _KVLAT0:_KR0], kvn_ref[...]).astype(BF16)
    q_all = _dot(q_lat, wuq_ref[...])
    kv_all = _dot(kv_lat, wukv_ref[...])

    cos = cos_ref[...]
    sin = sin_ref[...]
    k_rope = (proj[:, _KR0:_KRS0] * cos + proj[:, _KRS0:_PROJ_W] * sin).astype(BF16)
    q_scale = MLA_QK ** -0.5 * LOG2E
    for hd in range(MLA_HEADS):
        q_nope = q_all[:, hd * MLA_NOPE:(hd + 1) * MLA_NOPE]
        q_rope = (q_all[:, _UQ_ROPE0 + hd * MLA_ROPE:_UQ_ROPE0 + (hd + 1) * MLA_ROPE] * cos
                  + q_all[:, _UQ_ROPES0 + hd * MLA_ROPE:_UQ_ROPES0 + (hd + 1) * MLA_ROPE] * sin)
        qm_ref[0, hd, :, 0:MLA_NOPE] = (q_nope * q_scale).astype(BF16)
        qm_ref[0, hd, :, MLA_NOPE:MLA_QK] = (q_rope * q_scale).astype(BF16)
        kv0 = hd * (MLA_NOPE + MLA_V)
        km_ref[0, hd, :, 0:MLA_NOPE] = kv_all[:, kv0:kv0 + MLA_NOPE].astype(BF16)
        km_ref[0, hd, :, MLA_NOPE:MLA_QK] = k_rope
        vm_ref[0, hd] = kv_all[:, kv0 + MLA_NOPE:kv0 + MLA_NOPE + MLA_V].astype(BF16)


def _rope_tables(seq):
    inv = np.float32(ROPE_THETA) ** (-np.arange(0, MLA_ROPE, 2, dtype=np.float32) / np.float32(MLA_ROPE))
    ang = np.arange(seq, dtype=np.float32)[:, None] * inv[None, :].astype(np.float32)
    cos, sin = np.cos(ang).astype(np.float32), np.sin(ang).astype(np.float32)
    return np.concatenate([cos, cos], axis=-1), np.concatenate([-sin, sin], axis=-1)


def _swap_halves(w):
    half = w.shape[-1] // 2
    return jnp.concatenate([w[..., half:], w[..., :half]], axis=-1)


def _mix_proj(x2d, mod3, gain, w_in, q_norm, kv_norm, w_uq, w_ukv, *, batch, seq, tm=512):
    m, d = x2d.shape
    tiles_per_seq = seq // tm
    w_uq_h = w_uq.reshape(MLA_Q_RANK, MLA_HEADS, MLA_QK)
    uq_rope = w_uq_h[:, :, MLA_NOPE:]
    w_uq_r = jnp.concatenate([w_uq_h[:, :, :MLA_NOPE].reshape(MLA_Q_RANK, -1),
                              uq_rope.reshape(MLA_Q_RANK, -1),
                              _swap_halves(uq_rope).reshape(MLA_Q_RANK, -1)], axis=1).astype(BF16)
    w_ukv_b = w_ukv.astype(BF16)
    cos2, sin2 = (jnp.asarray(tab) for tab in _rope_tables(seq))

    def head_spec(width):
        return pl.BlockSpec((1, MLA_HEADS, tm, width),
                            lambda i: (i // tiles_per_seq, 0, i % tiles_per_seq, 0))

    def head_shape(width):
        return jax.ShapeDtypeStruct((batch, MLA_HEADS, seq, width), BF16)

    return pl.pallas_call(
        functools.partial(_mix_proj_kernel, tiles_per_seq=tiles_per_seq),
        grid=(m // tm,),
        in_specs=[pl.BlockSpec((tm, d), lambda i: (i, 0)),
                  _resident(mod3.shape),
                  _resident((1, d)),
                  _resident(w_in.shape),
                  _resident((1, MLA_Q_RANK)), _resident((1, MLA_KV_RANK)),
                  _resident(w_uq_r.shape), _resident(w_ukv_b.shape),
                  pl.BlockSpec((tm, MLA_ROPE), lambda i: (i % tiles_per_seq, 0)),
                  pl.BlockSpec((tm, MLA_ROPE), lambda i: (i % tiles_per_seq, 0))],
        out_specs=[pl.BlockSpec((1, SWA_HEADS, tm, SWA_HEAD_DIM),
                                lambda i: (i // tiles_per_seq, 0, i % tiles_per_seq, 0)),
                   pl.BlockSpec((tm, 2 * SWA_KV_HEADS * SWA_HEAD_DIM), lambda i: (i, 0)),
                   head_spec(MLA_QK), head_spec(MLA_QK), head_spec(MLA_V)],
        out_shape=[jax.ShapeDtypeStruct((batch, SWA_HEADS, seq, SWA_HEAD_DIM), BF16),
                   jax.ShapeDtypeStruct((m, 2 * SWA_KV_HEADS * SWA_HEAD_DIM), BF16),
                   head_shape(MLA_QK), head_shape(MLA_QK), head_shape(MLA_V)],
        scratch_shapes=[pltpu.VMEM((d, _PROJ_W), BF16)],
        compiler_params=_params(1),
        name="mix_proj",
    )(x2d, mod3, gain.reshape(1, d), w_in, q_norm.reshape(1, -1), kv_norm.reshape(1, -1),
      w_uq_r, w_ukv_b, cos2, sin2)


def _t5_bucket_table():
    qi = np.arange(WINDOW)[:, None]
    kj = np.arange(2 * WINDOW)[None, :]
    dist = qi + WINDOW - kj
    max_exact = NUM_BUCKETS // 2
    n = np.maximum(dist, 0)
    nf = np.maximum(n, 1).astype(np.float32)
    large = max_exact + (np.log(nf / np.float32(max_exact)) / np.float32(math.log(MAX_DISTANCE / max_exact))
                         * np.float32(NUM_BUCKETS - max_exact)).astype(np.int32)
    large = np.minimum(large, NUM_BUCKETS - 1)
    bucket = np.where(n < max_exact, n, large)
    band = (dist >= 0) & (dist < WINDOW)
    return np.where(band, bucket, -1).astype(np.int32)


def _swa_bias_kernel(rel_ref, bucket_ref, o_ref):
    bucket = bucket_ref[...]
    first_block_ok = lax.broadcasted_iota(jnp.int32, bucket.shape, 0) >= WINDOW
    for hd in range(SWA_HEADS):
        acc = jnp.full(bucket.shape, MASK_VALUE, F32)
        for b in range(NUM_BUCKETS):
            acc = jnp.where(bucket == b, rel_ref[b, hd] * LOG2E, acc)
        g, j = divmod(hd, SWA_GROUP)
        o_ref[0, g, :, j * WINDOW:(j + 1) * WINDOW] = jnp.where(first_block_ok, acc, MASK_VALUE)
        o_ref[1, g, :, j * WINDOW:(j + 1) * WINDOW] = acc


def _swa_bias(rel_bias):
    shape = (2, SWA_KV_HEADS, 2 * WINDOW, SWA_GROUP * WINDOW)
    return pl.pallas_call(
        _swa_bias_kernel,
        in_specs=[pl.BlockSpec(memory_space=pltpu.SMEM),
                  pl.BlockSpec((2 * WINDOW, WINDOW), lambda: (0, 0))],
        out_specs=pl.BlockSpec(shape, lambda: (0, 0, 0, 0)),
        out_shape=jax.ShapeDtypeStruct(shape, F32),
        name="swa_bias",
    )(rel_bias, jnp.asarray(_t5_bucket_table().T))


def _swa_kernel(sink_ref, q_ref, kv_ref, bias_ref, o_ref, kp_ref, vt_ref, *, blocks_per_iter):
    seq = kv_ref.shape[0]
    n_blocks = seq // WINDOW
    kdim = SWA_KV_HEADS * SWA_HEAD_DIM
    kp_ref[0:WINDOW, :] = jnp.zeros((WINDOW, kdim), BF16)
    kp_ref[WINDOW:, :] = kv_ref[:, 0:kdim]
    ones_rows = jnp.ones((SWA_ONES_ROWS, WINDOW), BF16)
    for g in range(SWA_KV_HEADS):
        vt_ref[0, g] = jnp.concatenate([jnp.zeros((SWA_HEAD_DIM, WINDOW), BF16), ones_rows], axis=0)
    for n in range(n_blocks):
        for g in range(SWA_KV_HEADS):
            v_blk = kv_ref[n * WINDOW:(n + 1) * WINDOW, kdim + g * SWA_HEAD_DIM:kdim + (g + 1) * SWA_HEAD_DIM]
            vt_ref[n + 1, g] = jnp.concatenate([v_blk.astype(F32).T.astype(BF16), ones_rows], axis=0)
    sink_rows = [[jnp.full((1, WINDOW), sink_ref[g * SWA_GROUP + j] * LOG2E, F32) for j in range(SWA_GROUP)]
                 for g in range(SWA_KV_HEADS)]

    def body(it, carry):
        units = [(b, g) for b in range(blocks_per_iter) for g in range(SWA_KV_HEADS)]
        blocks = [it * blocks_per_iter + b for b in range(blocks_per_iter)]
        q_starts = [pl.multiple_of(n * WINDOW, WINDOW) for n in blocks]
        tables = [jnp.minimum(n, 1) for n in blocks]
        sts, pts = {}, {}

        def scores(u):
            b, g = units[u]
            k = kp_ref[pl.ds(q_starts[b], 2 * WINDOW), g * SWA_HEAD_DIM:(g + 1) * SWA_HEAD_DIM]
            sts[u] = []
            for j in range(0, SWA_GROUP, 2):
                hd = g * SWA_GROUP + j
                q = jnp.concatenate([q_ref[0, hd, pl.ds(q_starts[b], WINDOW), :],
                                     q_ref[0, hd + 1, pl.ds(q_starts[b], WINDOW), :]], axis=0)
                st = lax.dot_general(k, q, NT_DIMS, preferred_element_type=F32)
                sts[u] += [st[:, 0:WINDOW], st[:, WINDOW:2 * WINDOW]]

        def softmax(u):
            b, g = units[u]
            pts[u] = []
            for j in range(SWA_GROUP):
                st = sts[u][j] + bias_ref[tables[b], g, :, j * WINDOW:(j + 1) * WINDOW]
                mx = jnp.maximum(jnp.max(st, axis=0, keepdims=True), sink_rows[g][j])
                pts[u].append((jnp.exp2(st - mx).astype(BF16), jnp.exp2(sink_rows[g][j] - mx)))

        def values(u):
            b, g = units[u]
            vt = jnp.concatenate([vt_ref[blocks[b], g], vt_ref[blocks[b] + 1, g]], axis=1)
            for j in range(0, SWA_GROUP, 2):
                pt = jnp.concatenate([pts[u][j][0], pts[u][j + 1][0]], axis=1)
                ot = _dot(vt, pt)
                halves = []
                for jj in range(2):
                    lanes = slice(jj * WINDOW, (jj + 1) * WINDOW)
                    denom = ot[SWA_HEAD_DIM:SWA_HEAD_DIM + 1, lanes] + pts[u][j + jj][1]
                    halves.append(ot[0:SWA_HEAD_DIM, lanes] * (1.0 / denom))
                pair = jnp.concatenate(halves, axis=0).T
                lane0 = (g * SWA_GROUP + j) * SWA_HEAD_DIM
                o_ref[pl.ds(q_starts[b], WINDOW), lane0:lane0 + 2 * SWA_HEAD_DIM] = pair.astype(BF16)

        for t in range(len(units) + 2):
            if t < len(units):
                scores(t)
            if 0 <= t - 1 < len(units):
                softmax(t - 1)
            if 0 <= t - 2 < len(units):
                values(t - 2)
        return carry

    lax.fori_loop(0, n_blocks // blocks_per_iter, body, 0)


def _swa(qa, kva, bias, sinks, *, batch, seq, blocks_per_iter=8):
    kvw = kva.shape[1]
    qw = SWA_HEADS * SWA_HEAD_DIM
    assert (seq // WINDOW) % blocks_per_iter == 0
    return pl.pallas_call(
        functools.partial(_swa_kernel, blocks_per_iter=blocks_per_iter),
        grid=(batch,),
        in_specs=[pl.BlockSpec(memory_space=pltpu.SMEM),
                  pl.BlockSpec((1, SWA_HEADS, seq, SWA_HEAD_DIM), lambda b: (b, 0, 0, 0)),
                  pl.BlockSpec((seq, kvw), lambda b: (b, 0)),
                  _resident(bias.shape)],
        out_specs=pl.BlockSpec((seq, qw), lambda b: (b, 0)),
        out_shape=jax.ShapeDtypeStruct((batch * seq, qw), BF16),
        scratch_shapes=[pltpu.VMEM((WINDOW + seq, SWA_KV_HEADS * SWA_HEAD_DIM), BF16),
                        pltpu.VMEM((1 + seq // WINDOW, SWA_KV_HEADS, SWA_HEAD_DIM + SWA_ONES_ROWS, WINDOW), BF16)],
        compiler_params=_params(1),
        name="swa",
    )(sinks, qa, kva, bias)


def _mla_kernel(q_ref, k_ref, v_ref, o_ref, vt_ref, *, tq):
    heads, seq, dv = v_ref.shape[1], v_ref.shape[2], v_ref.shape[3]
    n_tiles = seq // tq
    key = lax.broadcasted_iota(jnp.int32, (tq, tq), 0)
    qry = lax.broadcasted_iota(jnp.int32, (tq, tq), 1)
    causal = key <= qry
    for hd in range(heads):
        for c in range(seq // WINDOW):
            v_blk = v_ref[0, hd, c * WINDOW:(c + 1) * WINDOW, :]
            vt_ref[hd, 0:dv, c * WINDOW:(c + 1) * WINDOW] = v_blk.astype(F32).T.astype(BF16)
        vt_ref[hd, dv:, :] = jnp.ones((MLA_ONES_ROWS, seq), BF16)
    sts, mxs, pts, accs = {}, {}, {}, {}

    def scores(hd, i, j):
        q = q_ref[0, hd, i * tq:(i + 1) * tq, :]
        s = lax.dot_general(k_ref[0, hd, j * tq:(j + 1) * tq, :], q, NT_DIMS, preferred_element_type=F32)
        sts[hd, i, j] = jnp.where(causal, s, MASK_VALUE) if j == i else s

    def col_max(hd, i, j):
        m = jnp.max(sts[hd, i, j], axis=0, keepdims=True)
        mxs[hd, i] = m if j == 0 else jnp.maximum(mxs[hd, i], m)

    def probs(hd, i, j):
        pts[hd, i, j] = jnp.exp2(sts.pop((hd, i, j)) - mxs[hd, i]).astype(BF16)

    def values(hd, i, j):
        part = _dot(vt_ref[hd, :, j * tq:(j + 1) * tq], pts.pop((hd, i, j)))
        accs[hd, i] = part if j == 0 else accs[hd, i] + part
        if j == i:
            ot = accs.pop((hd, i))
            o = ot[0:dv, :] * (1.0 / ot[dv:dv + 1, :])
            o_ref[i * tq:(i + 1) * tq, hd * dv:(hd + 1) * dv] = o.T.astype(BF16)

    stages = (scores, col_max, probs, values)
    units = [(hd, i) for hd in range(heads) for i in range(n_tiles)]
    for t in range(len(units) + len(stages) - 1):
        for j in range(n_tiles):
            for lag, stage in enumerate(stages):
                if 0 <= t - lag < len(units):
                    hd, i = units[t - lag]
                    if j <= i:
                        stage(hd, i, j)


def _mla(qm, km, vm, *, tq=256):
    batch, heads, seq, qk = qm.shape
    dv = vm.shape[-1]
    return pl.pallas_call(
        functools.partial(_mla_kernel, tq=tq),
        grid=(batch,),
        in_specs=[pl.BlockSpec((1, heads, seq, qk), lambda b: (b, 0, 0, 0)),
                  pl.BlockSpec((1, heads, seq, qk), lambda b: (b, 0, 0, 0)),
                  pl.BlockSpec((1, heads, seq, dv), lambda b: (b, 0, 0, 0))],
        out_specs=pl.BlockSpec((seq, heads * dv), lambda b: (b, 0)),
        out_shape=jax.ShapeDtypeStruct((batch * seq, heads * dv), BF16),
        scratch_shapes=[pltpu.VMEM((heads, dv + MLA_ONES_ROWS, seq), BF16)],
        compiler_params=_params(1),
        name="mla",
    )(qm, km, vm)


def kernel(x, c, w_mod, b_mod, norm_ffn1, ffn1_gate, ffn1_up, ffn1_down, norm_mix, w_in, q_norm, kv_norm, w_uq, w_ukv, sinks, w_o, norm_ffn2, ffn2_gate, ffn2_up, ffn2_down, rel_bias, norm_final):
    batch, seq, d = x.shape
    depth = w_mod.shape[0]
    x2d = x.reshape(batch * seq, d)
    bias = _swa_bias(rel_bias)
    for l in range(depth):
        mod3 = _mod(c, w_mod[l], b_mod[l])
        x2d = _ffn(x2d, mod3, norm_ffn1[l], ffn1_gate[l], ffn1_up[l], ffn1_down[l], mod_row=0, seq=seq)
        qa, kva, qm, km, vm = _mix_proj(x2d, mod3, norm_mix[l], w_in[l], q_norm[l], kv_norm[l],
                                        w_uq[l], w_ukv[l], batch=batch, seq=seq)
        out_a = _swa(qa, kva, bias, sinks[l], batch=batch, seq=seq)
        out_b = _mla(qm, km, vm)
        last = l == depth - 1
        x2d = _ffn(x2d, mod3, norm_ffn2[l], ffn2_gate[l], ffn2_up[l], ffn2_down[l], mod_row=6, seq=seq,
                   mix=(out_a, out_b, w_o[l].astype(BF16)),
                   final_gain=norm_final if last else None)
    return x2d.reshape(batch, seq, d)
```

```python
import functools
import math

import jax
import jax.numpy as jnp
import numpy as np
from jax import lax
from jax.experimental import pallas as pl
from jax.experimental.pallas import tpu as pltpu

EPS = 1e-6
FFN_RES_WEIGHT = 0.5
N_MOD = 9

SWA_HEADS = 8
SWA_KV_HEADS = 2
SWA_HEAD_DIM = 64
SWA_GROUP = SWA_HEADS // SWA_KV_HEADS
WINDOW = 128

MLA_HEADS = 4
MLA_Q_RANK = 256
MLA_KV_RANK = 128
MLA_NOPE = 128
MLA_ROPE = 64
MLA_V = 128
MLA_QK = MLA_NOPE + MLA_ROPE
ROPE_THETA = 10000.0

NUM_BUCKETS = 32
MAX_DISTANCE = 128

SWA_ONES_ROWS = 16
MLA_ONES_ROWS = 16
LOG2E = math.log2(math.e)
MASK_VALUE = -1e30
V7X_VMEM_LIMIT_BYTES = 56 * 1024 * 1024

F32 = jnp.float32
BF16 = jnp.bfloat16
NT_DIMS = (((1,), (1,)), ((), ()))


def _params(n_grid_dims):
    return pltpu.CompilerParams(dimension_semantics=("arbitrary",) * n_grid_dims,
                                vmem_limit_bytes=V7X_VMEM_LIMIT_BYTES)


def _resident(shape):
    return pl.BlockSpec(shape, lambda *_: (0,) * len(shape), pipeline_mode=pl.Buffered(1))


def _rms_norm(x, gain):
    ms = jnp.mean(x * x, axis=-1, keepdims=True)
    return x * lax.rsqrt(ms + EPS) * gain


def _silu(x):
    return x * (1.0 / (1.0 + jnp.exp(-x)))


def _dot(a, b):
    return jnp.dot(a, b, preferred_element_type=F32)


def _mod_kernel(c_ref, w_ref, b_ref, o_ref):
    c_act = _silu(c_ref[...]).astype(BF16)
    res = _dot(c_act, w_ref[...].astype(BF16)) + b_ref[...]
    d = o_ref.shape[2]
    for r in range(o_ref.shape[0]):
        o_ref[r] = res[:, r * d:(r + 1) * d]


def _mod(c, w_mod, b_mod, *, rows_per_step=3):
    batch, d = c.shape
    n = w_mod.shape[1]
    tn = rows_per_step * d
    return pl.pallas_call(
        _mod_kernel,
        grid=(n // tn,),
        in_specs=[pl.BlockSpec((batch, d), lambda j: (0, 0)),
                  pl.BlockSpec((d, tn), lambda j: (0, j)),
                  pl.BlockSpec((1, tn), lambda j: (0, j))],
        out_specs=pl.BlockSpec((rows_per_step, batch, d), lambda j: (j, 0, 0)),
        out_shape=jax.ShapeDtypeStruct((n // d, batch, d), F32),
        compiler_params=_params(1),
        name="mod",
    )(c, w_mod, b_mod.reshape(1, n))


def _mod_row(mod_ref, row, b):
    return mod_ref[row, pl.ds(b, 1), :]


def _ffn_kernel(*refs, mod_row, tiles_per_seq, mix_residual, final_norm):
    refs = list(refs)
    x_ref, mod_ref = refs[:2]
    del refs[:2]
    if mix_residual:
        oa_ref, ob_ref, wo_ref = refs[:3]
        del refs[:3]
    g_ref, wg_hbm, wu_hbm, wd_hbm = refs[:4]
    del refs[:4]
    if final_norm:
        gf_ref = refs.pop(0)
    (o_ref, wg_s, wu_s, wd_s, h_even, h_odd, xres_s, a_s,
     stage_gu, stage_d, sem_gu, sem_d) = refs
    n_w, _, fw = wg_s.shape
    tm = h_even.shape[0]
    n_sub = x_ref.shape[0] // tm
    t = pl.program_id(0) - 1

    def weight_stage():
        gu_slots, gu_rows = stage_gu.shape[0], stage_gu.shape[1]
        d_slots, d_rows = stage_d.shape[0], stage_d.shape[1]
        gu_chunks = [(src, dst, k) for k in range(wg_hbm.shape[0] // gu_rows)
                     for src, dst in ((wg_hbm, wg_s), (wu_hbm, wu_s))]
        d_chunks = list(range(wd_hbm.shape[0] // d_rows))

        def gu_copy(n):
            src, _, k = gu_chunks[n]
            return pltpu.make_async_copy(src.at[pl.ds(k * gu_rows, gu_rows), :], stage_gu.at[n % gu_slots],
                                         sem_gu.at[n % gu_slots])

        def d_copy(n):
            return pltpu.make_async_copy(wd_hbm.at[pl.ds(d_chunks[n] * d_rows, d_rows), :], stage_d.at[n % d_slots],
                                         sem_d.at[n % d_slots])

        for n in range(gu_slots):
            gu_copy(n).start()
        for n in range(d_slots):
            d_copy(n).start()
        for n in range(len(gu_chunks)):
            gu_copy(n).wait()
            _, dst, k = gu_chunks[n]
            for c in range(n_w):
                dst[c, k * gu_rows:(k + 1) * gu_rows, :] = stage_gu[n % gu_slots, :, c * fw:(c + 1) * fw].astype(BF16)
            if n + gu_slots < len(gu_chunks):
                gu_copy(n + gu_slots).start()
            if n % 2 == 1:
                m = n // 2
                d_copy(m).wait()
                wd_s[d_chunks[m] * d_rows:(d_chunks[m] + 1) * d_rows, :] = stage_d[m % d_slots].astype(BF16)
                if m + d_slots < len(d_chunks):
                    d_copy(m + d_slots).start()

    def batch_of(sub):
        return (t * n_sub + sub) // tiles_per_seq

    def norm_stage(sub, h_next):
        rows = pl.ds(sub * tm, tm)
        b = batch_of(sub)
        x = x_ref[rows, :]
        if mix_residual:
            wa = oa_ref.shape[1]
            mix = _dot(oa_ref[rows, :], wo_ref[0:wa, :]) + _dot(ob_ref[rows, :], wo_ref[wa:, :])
            x = x + _mod_row(mod_ref, mod_row - 1, b) * mix
            xres_s[rows, :] = x
        gain = g_ref[...] * (1.0 + _mod_row(mod_ref, mod_row + 1, b))
        ms = jnp.mean(x * x, axis=-1, keepdims=True)
        h_next[...] = (x * lax.rsqrt(ms + EPS) * gain + _mod_row(mod_ref, mod_row, b)).astype(BF16)

    def matmul_stage(sub, h_cur):
        rows = pl.ds(sub * tm, tm)
        for c in range(n_w):
            g = _dot(h_cur[...], wg_s[c])
            u = _dot(h_cur[...], wu_s[c])
            a_s[:, c * fw:(c + 1) * fw] = (_silu(g) * u).astype(BF16)
        y = _dot(a_s[...], wd_s[...])
        x = xres_s[rows, :] if mix_residual else x_ref[rows, :]
        out = x + FFN_RES_WEIGHT * _mod_row(mod_ref, mod_row + 2, batch_of(sub)) * y
        if final_norm:
            out = _rms_norm(out, gf_ref[...])
        o_ref[rows, :] = out

    @pl.when(t == -1)
    def _():
        weight_stage()

    @pl.when(t >= 0)
    def _():
        bufs = (h_even, h_odd)
        norm_stage(0, bufs[0])
        for sub in range(n_sub):
            if sub + 1 < n_sub:
                norm_stage(sub + 1, bufs[(sub + 1) % 2])
            matmul_stage(sub, bufs[sub % 2])


def _ffn(x2d, mod3, gain, wg, wu, wd, *, mod_row, seq, mix=None, final_gain=None, tm=512, tiles_per_step=2, fw=256,
         stage_chunks=8, gu_slots=2, d_slots=2):
    m, d = x2d.shape
    f = wg.shape[1]
    bm = tm * tiles_per_step
    assert f % fw == 0 and seq % tm == 0 and m % bm == 0 and d % stage_chunks == 0 and f % stage_chunks == 0
    n_w, n_blocks, tiles_per_seq = f // fw, m // bm, seq // tm
    final_norm = final_gain is not None

    def block(s):
        return (jnp.maximum(s - 1, 0), 0)

    in_specs = [pl.BlockSpec((bm, d), block), _resident(mod3.shape)]
    args = [x2d, mod3]
    if mix is not None:
        out_a, out_b, w_o = mix
        wa, wb = out_a.shape[1], out_b.shape[1]
        in_specs += [pl.BlockSpec((bm, wa), block), pl.BlockSpec((bm, wb), block), _resident((wa + wb, d))]
        args += [out_a, out_b, w_o]
    hbm = pl.BlockSpec(memory_space=pl.ANY)
    in_specs += [_resident((1, d)), hbm, hbm, hbm]
    args += [gain.reshape(1, d), wg, wu, wd]
    if final_norm:
        in_specs.append(_resident((1, d)))
        args.append(final_gain.reshape(1, d))
    xres_rows = bm if mix is not None else 8
    return pl.pallas_call(
        functools.partial(_ffn_kernel, mod_row=mod_row, tiles_per_seq=tiles_per_seq,
                          mix_residual=mix is not None, final_norm=final_norm),
        grid=(n_blocks + 1,),
        in_specs=in_specs,
        out_specs=pl.BlockSpec((bm, d), block),
        out_shape=jax.ShapeDtypeStruct((m, d), F32),
        scratch_shapes=[pltpu.VMEM((n_w, d, fw), BF16), pltpu.VMEM((n_w, d, fw), BF16), pltpu.VMEM((f, d), BF16),
                        pltpu.VMEM((tm, d), BF16), pltpu.VMEM((tm, d), BF16),
                        pltpu.VMEM((xres_rows, d), F32), pltpu.VMEM((tm, f), BF16),
                        pltpu.VMEM((gu_slots, d // stage_chunks, f), F32),
                        pltpu.VMEM((d_slots, f // stage_chunks, d), F32),
                        pltpu.SemaphoreType.DMA((gu_slots,)), pltpu.SemaphoreType.DMA((d_slots,))],
        compiler_params=_params(1),
        name="ffn_final" if final_norm else "ffn",
    )(*args)


_QA0, _KVA0, _QLAT0, _KVLAT0, _KR0, _KRS0, _PROJ_W = 0, 512, 768, 1024, 1152, 1216, 1280
_UQ_ROPE0 = MLA_HEADS * MLA_NOPE
_UQ_ROPES0 = _UQ_ROPE0 + MLA_HEADS * MLA_ROPE


def _mix_proj_kernel(x_ref, mod_ref, g_ref, win_ref, qn_ref, kvn_ref, wuq_ref, wukv_ref, cos_ref, sin_ref,
                     qa_ref, kva_ref, qm_ref, km_ref, vm_ref, win_s, *, blocks_per_seq, sub_rows):
    @pl.when(pl.program_id(0) == 0)
    def _():
        w = win_ref[...]
        half = MLA_ROPE // 2
        win_s[:, 0:_KRS0] = w.astype(BF16)
        win_s[:, _KRS0:_KRS0 + half] = w[:, _KR0 + half:_KRS0].astype(BF16)
        win_s[:, _KRS0 + half:_PROJ_W] = w[:, _KR0:_KR0 + half].astype(BF16)

    b = pl.program_id(0) // blocks_per_seq
    shift = _mod_row(mod_ref, 3, b)
    scale = _mod_row(mod_ref, 4, b)
    q_scale = MLA_QK ** -0.5 * LOG2E
    for r0 in range(0, x_ref.shape[0], sub_rows):
        rows = slice(r0, r0 + sub_rows)
        h = (_rms_norm(x_ref[rows, :], g_ref[...]) * (1.0 + scale) + shift).astype(BF16)
        proj = _dot(h, win_s[...])
        for hd in range(SWA_HEADS):
            qa_ref[0, hd, rows, :] = (proj[:, _QA0 + hd * SWA_HEAD_DIM:_QA0 + (hd + 1) * SWA_HEAD_DIM]
                                      * (SWA_HEAD_DIM ** -0.5 * LOG2E)).astype(BF16)
        kva_ref[rows, :] = proj[:, _KVA0:_QLAT0].astype(BF16)

        q_lat = _rms_norm(proj[:, _QLAT0:_KVLAT0], qn_ref[...]).astype(BF16)
        kv_lat = _rms_norm(proj[:, _KVLAT0:_KR0], kvn_ref[...]).astype(BF16)
        q_all = _dot(q_lat, wuq_ref[...])
        kv_all = _dot(kv_lat, wukv_ref[...])

        cos = cos_ref[rows, :]
        sin = sin_ref[rows, :]
        k_rope = (proj[:, _KR0:_KRS0] * cos + proj[:, _KRS0:_PROJ_W] * sin).astype(BF16)
        for hd in range(MLA_HEADS):
            q_nope = q_all[:, hd * MLA_NOPE:(hd + 1) * MLA_NOPE]
            q_rope = (q_all[:, _UQ_ROPE0 + hd * MLA_ROPE:_UQ_ROPE0 + (hd + 1) * MLA_ROPE] * cos
                      + q_all[:, _UQ_ROPES0 + hd * MLA_ROPE:_UQ_ROPES0 + (hd + 1) * MLA_ROPE] * sin)
            qm_ref[0, hd, rows, 0:MLA_NOPE] = (q_nope * q_scale).astype(BF16)
            qm_ref[0, hd, rows, MLA_NOPE:MLA_QK] = (q_rope * q_scale).astype(BF16)
            kv0 = hd * (MLA_NOPE + MLA_V)
            km_ref[0, hd, rows, 0:MLA_NOPE] = kv_all[:, kv0:kv0 + MLA_NOPE].astype(BF16)
            km_ref[0, hd, rows, MLA_NOPE:MLA_QK] = k_rope
            vm_ref[0, hd, rows, :] = kv_all[:, kv0 + MLA_NOPE:kv0 + MLA_NOPE + MLA_V].astype(BF16)


def _rope_tables(seq):
    inv = np.float32(ROPE_THETA) ** (-np.arange(0, MLA_ROPE, 2, dtype=np.float32) / np.float32(MLA_ROPE))
    ang = np.arange(seq, dtype=np.float32)[:, None] * inv[None, :].astype(np.float32)
    cos, sin = np.cos(ang).astype(np.float32), np.sin(ang).astype(np.float32)
    return np.concatenate([cos, cos], axis=-1), np.concatenate([-sin, sin], axis=-1)


def _swap_halves(w):
    half = w.shape[-1] // 2
    return jnp.concatenate([w[..., half:], w[..., :half]], axis=-1)


def _mix_proj(x2d, mod3, gain, w_in, q_norm, kv_norm, w_uq, w_ukv, *, batch, seq, tm=1024, sub_rows=512):
    m, d = x2d.shape
    tiles_per_seq = seq // tm
    assert seq % tm == 0 and tm % sub_rows == 0
    w_uq_h = w_uq.reshape(MLA_Q_RANK, MLA_HEADS, MLA_QK)
    uq_rope = w_uq_h[:, :, MLA_NOPE:]
    w_uq_r = jnp.concatenate([w_uq_h[:, :, :MLA_NOPE].reshape(MLA_Q_RANK, -1),
                              uq_rope.reshape(MLA_Q_RANK, -1),
                              _swap_halves(uq_rope).reshape(MLA_Q_RANK, -1)], axis=1).astype(BF16)
    w_ukv_b = w_ukv.astype(BF16)
    cos2, sin2 = (jnp.asarray(tab) for tab in _rope_tables(seq))

    def head_spec(width):
        return pl.BlockSpec((1, MLA_HEADS, tm, width),
                            lambda i: (i // tiles_per_seq, 0, i % tiles_per_seq, 0))

    def head_shape(width):
        return jax.ShapeDtypeStruct((batch, MLA_HEADS, seq, width), BF16)

    return pl.pallas_call(
        functools.partial(_mix_proj_kernel, blocks_per_seq=tiles_per_seq, sub_rows=sub_rows),
        grid=(m // tm,),
        in_specs=[pl.BlockSpec((tm, d), lambda i: (i, 0)),
                  _resident(mod3.shape),
                  _resident((1, d)),
                  _resident(w_in.shape),
                  _resident((1, MLA_Q_RANK)), _resident((1, MLA_KV_RANK)),
                  _resident(w_uq_r.shape), _resident(w_ukv_b.shape),
                  pl.BlockSpec((tm, MLA_ROPE), lambda i: (i % tiles_per_seq, 0)),
                  pl.BlockSpec((tm, MLA_ROPE), lambda i: (i % tiles_per_seq, 0))],
        out_specs=[pl.BlockSpec((1, SWA_HEADS, tm, SWA_HEAD_DIM),
                                lambda i: (i // tiles_per_seq, 0, i % tiles_per_seq, 0)),
                   pl.BlockSpec((tm, 2 * SWA_KV_HEADS * SWA_HEAD_DIM), lambda i: (i, 0)),
                   head_spec(MLA_QK), head_spec(MLA_QK), head_spec(MLA_V)],
        out_shape=[jax.ShapeDtypeStruct((batch, SWA_HEADS, seq, SWA_HEAD_DIM), BF16),
                   jax.ShapeDtypeStruct((m, 2 * SWA_KV_HEADS * SWA_HEAD_DIM), BF16),
                   head_shape(MLA_QK), head_shape(MLA_QK), head_shape(MLA_V)],
        scratch_shapes=[pltpu.VMEM((d, _PROJ_W), BF16)],
        compiler_params=_params(1),
        name="mix_proj",
    )(x2d, mod3, gain.reshape(1, d), w_in, q_norm.reshape(1, -1), kv_norm.reshape(1, -1),
      w_uq_r, w_ukv_b, cos2, sin2)


def _t5_bucket_table():
    qi = np.arange(WINDOW)[:, None]
    kj = np.arange(2 * WINDOW)[None, :]
    dist = qi + WINDOW - kj
    max_exact = NUM_BUCKETS // 2
    n = np.maximum(dist, 0)
    nf = np.maximum(n, 1).astype(np.float32)
    large = max_exact + (np.log(nf / np.float32(max_exact)) / np.float32(math.log(MAX_DISTANCE / max_exact))
                         * np.float32(NUM_BUCKETS - max_exact)).astype(np.int32)
    large = np.minimum(large, NUM_BUCKETS - 1)
    bucket = np.where(n < max_exact, n, large)
    band = (dist >= 0) & (dist < WINDOW)
    return np.where(band, bucket, -1).astype(np.int32)


def _swa_bias_kernel(rel_ref, bucket_ref, o_ref):
    bucket = bucket_ref[...]
    first_block_ok = lax.broadcasted_iota(jnp.int32, bucket.shape, 0) >= WINDOW
    for hd in range(SWA_HEADS):
        acc = jnp.full(bucket.shape, MASK_VALUE, F32)
        for b in range(NUM_BUCKETS):
            acc = jnp.where(bucket == b, rel_ref[b, hd] * LOG2E, acc)
        g, j = divmod(hd, SWA_GROUP)
        o_ref[0, g, :, j * WINDOW:(j + 1) * WINDOW] = jnp.where(first_block_ok, acc, MASK_VALUE)
        o_ref[1, g, :, j * WINDOW:(j + 1) * WINDOW] = acc


def _swa_bias(rel_bias):
    shape = (2, SWA_KV_HEADS, 2 * WINDOW, SWA_GROUP * WINDOW)
    return pl.pallas_call(
        _swa_bias_kernel,
        in_specs=[pl.BlockSpec(memory_space=pltpu.SMEM),
                  pl.BlockSpec((2 * WINDOW, WINDOW), lambda: (0, 0))],
        out_specs=pl.BlockSpec(shape, lambda: (0, 0, 0, 0)),
        out_shape=jax.ShapeDtypeStruct(shape, F32),
        name="swa_bias",
    )(rel_bias, jnp.asarray(_t5_bucket_table().T))


def _swa_kernel(sink_ref, q_ref, kv_ref, bias_ref, o_ref, kp_ref, vt_ref, *, blocks_per_iter):
    seq = kv_ref.shape[0]
    n_blocks = seq // WINDOW
    kdim = SWA_KV_HEADS * SWA_HEAD_DIM
    kp_ref[0:WINDOW, :] = jnp.zeros((WINDOW, kdim), BF16)
    kp_ref[WINDOW:, :] = kv_ref[:, 0:kdim]
    ones_rows = jnp.ones((SWA_ONES_ROWS, WINDOW), BF16)
    for g in range(SWA_KV_HEADS):
        vt_ref[0, g] = jnp.concatenate([jnp.zeros((SWA_HEAD_DIM, WINDOW), BF16), ones_rows], axis=0)
    for n in range(n_blocks):
        for g in range(SWA_KV_HEADS):
            v_blk = kv_ref[n * WINDOW:(n + 1) * WINDOW, kdim + g * SWA_HEAD_DIM:kdim + (g + 1) * SWA_HEAD_DIM]
            vt_ref[n + 1, g] = jnp.concatenate([v_blk.astype(F32).T.astype(BF16), ones_rows], axis=0)
    sink_rows = [[jnp.full((1, WINDOW), sink_ref[g * SWA_GROUP + j] * LOG2E, F32) for j in range(SWA_GROUP)]
                 for g in range(SWA_KV_HEADS)]

    def body(it, carry):
        units = [(b, g) for b in range(blocks_per_iter) for g in range(SWA_KV_HEADS)]
        blocks = [it * blocks_per_iter + b for b in range(blocks_per_iter)]
        q_starts = [pl.multiple_of(n * WINDOW, WINDOW) for n in blocks]
        tables = [jnp.minimum(n, 1) for n in blocks]
        sts, pts = {}, {}

        def scores(u):
            b, g = units[u]
            k = kp_ref[pl.ds(q_starts[b], 2 * WINDOW), g * SWA_HEAD_DIM:(g + 1) * SWA_HEAD_DIM]
            sts[u] = []
            for j in range(0, SWA_GROUP, 2):
                hd = g * SWA_GROUP + j
                q = jnp.concatenate([q_ref[0, hd, pl.ds(q_starts[b], WINDOW), :],
                                     q_ref[0, hd + 1, pl.ds(q_starts[b], WINDOW), :]], axis=0)
                st = lax.dot_general(k, q, NT_DIMS, preferred_element_type=F32)
                sts[u] += [st[:, 0:WINDOW], st[:, WINDOW:2 * WINDOW]]

        def softmax(u):
            b, g = units[u]
            pts[u] = []
            for j in range(SWA_GROUP):
                st = sts[u][j] + bias_ref[tables[b], g, :, j * WINDOW:(j + 1) * WINDOW]
                mx = jnp.maximum(jnp.max(st, axis=0, keepdims=True), sink_rows[g][j])
                pts[u].append((jnp.exp2(st - mx).astype(BF16), jnp.exp2(sink_rows[g][j] - mx)))

        def values(u):
            b, g = units[u]
            vt = jnp.concatenate([vt_ref[blocks[b], g], vt_ref[blocks[b] + 1, g]], axis=1)
            for j in range(0, SWA_GROUP, 2):
                pt = jnp.concatenate([pts[u][j][0], pts[u][j + 1][0]], axis=1)
                ot = _dot(vt, pt)
                halves = []
                for jj in range(2):
                    lanes = slice(jj * WINDOW, (jj + 1) * WINDOW)
                    denom = ot[SWA_HEAD_DIM:SWA_HEAD_DIM + 1, lanes] + pts[u][j + jj][1]
                    halves.append(ot[0:SWA_HEAD_DIM, lanes] * (1.0 / denom))
                pair = jnp.concatenate(halves, axis=0).T
                lane0 = (g * SWA_GROUP + j) * SWA_HEAD_DIM
                o_ref[pl.ds(q_starts[b], WINDOW), lane0:lane0 + 2 * SWA_HEAD_DIM] = pair.astype(BF16)

        for t in range(len(units) + 2):
            if t < len(units):
                scores(t)
            if 0 <= t - 1 < len(units):
                softmax(t - 1)
            if 0 <= t - 2 < len(units):
                values(t - 2)
        return carry

    lax.fori_loop(0, n_blocks // blocks_per_iter, body, 0)


def _swa(qa, kva, bias, sinks, *, batch, seq, blocks_per_iter=8):
    kvw = kva.shape[1]
    qw = SWA_HEADS * SWA_HEAD_DIM
    assert (seq // WINDOW) % blocks_per_iter == 0
    return pl.pallas_call(
        functools.partial(_swa_kernel, blocks_per_iter=blocks_per_iter),
        grid=(batch,),
        in_specs=[pl.BlockSpec(memory_space=pltpu.SMEM),
                  pl.BlockSpec((1, SWA_HEADS, seq, SWA_HEAD_DIM), lambda b: (b, 0, 0, 0)),
                  pl.BlockSpec((seq, kvw), lambda b: (b, 0)),
                  _resident(bias.shape)],
        out_specs=pl.BlockSpec((seq, qw), lambda b: (b, 0)),
        out_shape=jax.ShapeDtypeStruct((batch * seq, qw), BF16),
        scratch_shapes=[pltpu.VMEM((WINDOW + seq, SWA_KV_HEADS * SWA_HEAD_DIM), BF16),
                        pltpu.VMEM((1 + seq // WINDOW, SWA_KV_HEADS, SWA_HEAD_DIM + SWA_ONES_ROWS, WINDOW), BF16)],
        compiler_params=_params(1),
        name="swa",
    )(sinks, qa, kva, bias)


def _mla_kernel(q_ref, k_ref, v_ref, o_ref, vt_ref, qt_ref, *, tq):
    heads, seq, dv = v_ref.shape[1], v_ref.shape[2], v_ref.shape[3]
    n_tiles = seq // tq
    key = lax.broadcasted_iota(jnp.int32, (tq, tq), 0)
    qry = lax.broadcasted_iota(jnp.int32, (tq, tq), 1)
    causal = key <= qry
    qk = q_ref.shape[3]
    for hd in range(heads):
        for c in range(seq // WINDOW):
            v_blk = v_ref[0, hd, c * WINDOW:(c + 1) * WINDOW, :]
            vt_ref[hd, 0:dv, c * WINDOW:(c + 1) * WINDOW] = v_blk.astype(F32).T.astype(BF16)
            q_blk = q_ref[0, hd, c * WINDOW:(c + 1) * WINDOW, :]
            qt_ref[hd, :, c * WINDOW:(c + 1) * WINDOW] = q_blk.astype(F32).T.astype(BF16)
        vt_ref[hd, dv:, :] = jnp.ones((MLA_ONES_ROWS, seq), BF16)
    sts, mxs, pts, accs = {}, {}, {}, {}

    def scores(hd, i, j):
        s = _dot(k_ref[0, hd, j * tq:(j + 1) * tq, :], qt_ref[hd, :, i * tq:(i + 1) * tq])
        sts[hd, i, j] = jnp.where(causal, s, MASK_VALUE) if j == i else s

    def col_max(hd, i, j):
        m = jnp.max(sts[hd, i, j], axis=0, keepdims=True)
        mxs[hd, i] = m if j == 0 else jnp.maximum(mxs[hd, i], m)

    def probs(hd, i, j):
        pts[hd, i, j] = jnp.exp2(sts.pop((hd, i, j)) - mxs[hd, i]).astype(BF16)

    def values(hd, i, j):
        part = _dot(vt_ref[hd, :, j * tq:(j + 1) * tq], pts.pop((hd, i, j)))
        accs[hd, i] = part if j == 0 else accs[hd, i] + part
        if j == i:
            ot = accs.pop((hd, i))
            o = ot[0:dv, :] * (1.0 / ot[dv:dv + 1, :])
            o_ref[i * tq:(i + 1) * tq, hd * dv:(hd + 1) * dv] = o.T.astype(BF16)

    stages = (scores, col_max, probs, values)
    units = [(hd, i) for hd in range(heads) for i in range(n_tiles)]
    for t in range(len(units) + len(stages) - 1):
        for j in range(n_tiles):
            for lag, stage in enumerate(stages):
                if 0 <= t - lag < len(units):
                    hd, i = units[t - lag]
                    if j <= i:
                        stage(hd, i, j)


def _mla(qm, km, vm, *, tq=256):
    batch, heads, seq, qk = qm.shape
    dv = vm.shape[-1]
    return pl.pallas_call(
        functools.partial(_mla_kernel, tq=tq),
        grid=(batch,),
        in_specs=[pl.BlockSpec((1, heads, seq, qk), lambda b: (b, 0, 0, 0)),
                  pl.BlockSpec((1, heads, seq, qk), lambda b: (b, 0, 0, 0)),
                  pl.BlockSpec((1, heads, seq, dv), lambda b: (b, 0, 0, 0))],
        out_specs=pl.BlockSpec((seq, heads * dv), lambda b: (b, 0)),
        out_shape=jax.ShapeDtypeStruct((batch * seq, heads * dv), BF16),
        scratch_shapes=[pltpu.VMEM((heads, dv + MLA_ONES_ROWS, seq), BF16), pltpu.VMEM((heads, qk, seq), BF16)],
        compiler_params=_params(1),
        name="mla",
    )(qm, km, vm)


def kernel(x, c, w_mod, b_mod, norm_ffn1, ffn1_gate, ffn1_up, ffn1_down, norm_mix, w_in, q_norm, kv_norm, w_uq, w_ukv, sinks, w_o, norm_ffn2, ffn2_gate, ffn2_up, ffn2_down, rel_bias, norm_final):
    batch, seq, d = x.shape
    depth = w_mod.shape[0]
    x2d = x.reshape(batch * seq, d)
    bias = _swa_bias(rel_bias)
    for l in range(depth):
        mod3 = _mod(c, w_mod[l], b_mod[l])
        x2d = _ffn(x2d, mod3, norm_ffn1[l], ffn1_gate[l], ffn1_up[l], ffn1_down[l], mod_row=0, seq=seq)
        qa, kva, qm, km, vm = _mix_proj(x2d, mod3, norm_mix[l], w_in[l], q_norm[l], kv_norm[l],
                                        w_uq[l], w_ukv[l], batch=batch, seq=seq)
        out_a = _swa(qa, kva, bias, sinks[l], batch=batch, seq=seq)
        out_b = _mla(qm, km, vm)
        last = l == depth - 1
        x2d = _ffn(x2d, mod3, norm_ffn2[l], ffn2_gate[l], ffn2_up[l], ffn2_down[l], mod_row=6, seq=seq,
                   mix=(out_a, out_b, w_o[l].astype(BF16)),
                   final_gain=norm_final if last else None)
    return x2d.reshape(batch, seq, d)
```

```python
import functools
import math

import jax
import jax.numpy as jnp
import numpy as np
from jax import lax
from jax.experimental import pallas as pl
from jax.experimental.pallas import tpu as pltpu

EPS = 1e-6
FFN_RES_WEIGHT = 0.5
N_MOD = 9

SWA_HEADS = 8
SWA_KV_HEADS = 2
SWA_HEAD_DIM = 64
SWA_GROUP = SWA_HEADS // SWA_KV_HEADS
WINDOW = 128

MLA_HEADS = 4
MLA_Q_RANK = 256
MLA_KV_RANK = 128
MLA_NOPE = 128
MLA_ROPE = 64
MLA_V = 128
MLA_QK = MLA_NOPE + MLA_ROPE
ROPE_THETA = 10000.0

NUM_BUCKETS = 32
MAX_DISTANCE = 128

SWA_ONES_ROWS = 16
MLA_ONES_ROWS = 16
LOG2E = math.log2(math.e)
MASK_VALUE = -1e30
V7X_VMEM_LIMIT_BYTES = 56 * 1024 * 1024

F32 = jnp.float32
BF16 = jnp.bfloat16
NT_DIMS = (((1,), (1,)), ((), ()))


def _params(n_grid_dims):
    return pltpu.CompilerParams(dimension_semantics=("arbitrary",) * n_grid_dims,
                                vmem_limit_bytes=V7X_VMEM_LIMIT_BYTES)


def _resident(shape):
    return pl.BlockSpec(shape, lambda *_: (0,) * len(shape), pipeline_mode=pl.Buffered(1))


def _rms_norm(x, gain):
    ms = jnp.mean(x * x, axis=-1, keepdims=True)
    return x * lax.rsqrt(ms + EPS) * gain


def _silu(x):
    return x * (1.0 / (1.0 + jnp.exp(-x)))


def _dot(a, b):
    return jnp.dot(a, b, preferred_element_type=F32)


def _mod_kernel(c_ref, w_ref, b_ref, o_ref):
    c_act = _silu(c_ref[...]).astype(BF16)
    res = _dot(c_act, w_ref[...].astype(BF16)) + b_ref[...]
    d = o_ref.shape[2]
    for r in range(o_ref.shape[0]):
        o_ref[r] = res[:, r * d:(r + 1) * d]


def _mod(c, w_mod, b_mod, *, rows_per_step=3):
    batch, d = c.shape
    n = w_mod.shape[1]
    tn = rows_per_step * d
    return pl.pallas_call(
        _mod_kernel,
        grid=(n // tn,),
        in_specs=[pl.BlockSpec((batch, d), lambda j: (0, 0)),
                  pl.BlockSpec((d, tn), lambda j: (0, j)),
                  pl.BlockSpec((1, tn), lambda j: (0, j))],
        out_specs=pl.BlockSpec((rows_per_step, batch, d), lambda j: (j, 0, 0)),
        out_shape=jax.ShapeDtypeStruct((n // d, batch, d), F32),
        compiler_params=_params(1),
        name="mod",
    )(c, w_mod, b_mod.reshape(1, n))


def _mod_row(mod_ref, row, b):
    return mod_ref[row, pl.ds(b, 1), :]


def _ffn_kernel(*refs, mod_row, tiles_per_seq, mix_residual, final_norm):
    refs = list(refs)
    x_ref, mod_ref = refs[:2]
    del refs[:2]
    if mix_residual:
        oa_ref, ob_ref, wo_ref = refs[:3]
        del refs[:3]
    g_ref, wg_hbm, wu_hbm, wd_hbm = refs[:4]
    del refs[:4]
    if final_norm:
        gf_ref = refs.pop(0)
    (o_ref, wg_s, wu_s, wd_s, h_even, h_odd, xres_s, a_s,
     stage_gu, stage_d, sem_gu, sem_d) = refs
    n_w, _, fw = wg_s.shape
    tm = h_even.shape[0]
    n_sub = x_ref.shape[0] // tm
    t = pl.program_id(0) - 1

    def weight_stage():
        gu_slots, gu_rows = stage_gu.shape[0], stage_gu.shape[1]
        d_slots, d_rows = stage_d.shape[0], stage_d.shape[1]
        gu_chunks = [(src, dst, k) for k in range(wg_hbm.shape[0] // gu_rows)
                     for src, dst in ((wg_hbm, wg_s), (wu_hbm, wu_s))]
        d_chunks = list(range(wd_hbm.shape[0] // d_rows))

        def gu_copy(n):
            src, _, k = gu_chunks[n]
            return pltpu.make_async_copy(src.at[pl.ds(k * gu_rows, gu_rows), :], stage_gu.at[n % gu_slots],
                                         sem_gu.at[n % gu_slots])

        def d_copy(n):
            return pltpu.make_async_copy(wd_hbm.at[pl.ds(d_chunks[n] * d_rows, d_rows), :], stage_d.at[n % d_slots],
                                         sem_d.at[n % d_slots])

        for n in range(gu_slots):
            gu_copy(n).start()
        for n in range(d_slots):
            d_copy(n).start()
        for n in range(len(gu_chunks)):
            gu_copy(n).wait()
            _, dst, k = gu_chunks[n]
            for c in range(n_w):
                dst[c, k * gu_rows:(k + 1) * gu_rows, :] = stage_gu[n % gu_slots, :, c * fw:(c + 1) * fw].astype(BF16)
            if n + gu_slots < len(gu_chunks):
                gu_copy(n + gu_slots).start()
            if n % 2 == 1:
                m = n // 2
                d_copy(m).wait()
                wd_s[d_chunks[m] * d_rows:(d_chunks[m] + 1) * d_rows, :] = stage_d[m % d_slots].astype(BF16)
                if m + d_slots < len(d_chunks):
                    d_copy(m + d_slots).start()

    def batch_of(sub):
        return (t * n_sub + sub) // tiles_per_seq

    def norm_stage(sub, h_next):
        rows = pl.ds(sub * tm, tm)
        b = batch_of(sub)
        x = x_ref[rows, :]
        if mix_residual:
            wa = oa_ref.shape[1]
            mix = _dot(oa_ref[rows, :], wo_ref[0:wa, :]) + _dot(ob_ref[rows, :], wo_ref[wa:, :])
            x = x + _mod_row(mod_ref, mod_row - 1, b) * mix
            xres_s[rows, :] = x
        gain = g_ref[...] * (1.0 + _mod_row(mod_ref, mod_row + 1, b))
        ms = jnp.mean(x * x, axis=-1, keepdims=True)
        h_next[...] = (x * lax.rsqrt(ms + EPS) * gain + _mod_row(mod_ref, mod_row, b)).astype(BF16)

    def matmul_stage(sub, h_cur):
        rows = pl.ds(sub * tm, tm)
        for c in range(n_w):
            g = _dot(h_cur[...], wg_s[c])
            u = _dot(h_cur[...], wu_s[c])
            a_s[:, c * fw:(c + 1) * fw] = (_silu(g) * u).astype(BF16)
        y = _dot(a_s[...], wd_s[...])
        x = xres_s[rows, :] if mix_residual else x_ref[rows, :]
        out = x + FFN_RES_WEIGHT * _mod_row(mod_ref, mod_row + 2, batch_of(sub)) * y
        if final_norm:
            out = _rms_norm(out, gf_ref[...])
        o_ref[rows, :] = out

    @pl.when(t == -1)
    def _():
        weight_stage()

    @pl.when(t >= 0)
    def _():
        bufs = (h_even, h_odd)
        norm_stage(0, bufs[0])
        for sub in range(n_sub):
            if sub + 1 < n_sub:
                norm_stage(sub + 1, bufs[(sub + 1) % 2])
            matmul_stage(sub, bufs[sub % 2])


def _ffn(x2d, mod3, gain, wg, wu, wd, *, mod_row, seq, mix=None, final_gain=None, tm=512, tiles_per_step=2, fw=256,
         stage_chunks=8, gu_slots=2, d_slots=2):
    m, d = x2d.shape
    f = wg.shape[1]
    bm = tm * tiles_per_step
    assert f % fw == 0 and seq % tm == 0 and m % bm == 0 and d % stage_chunks == 0 and f % stage_chunks == 0
    n_w, n_blocks, tiles_per_seq = f // fw, m // bm, seq // tm
    final_norm = final_gain is not None

    def block(s):
        return (jnp.maximum(s - 1, 0), 0)

    in_specs = [pl.BlockSpec((bm, d), block), _resident(mod3.shape)]
    args = [x2d, mod3]
    if mix is not None:
        out_a, out_b, w_o = mix
        wa, wb = out_a.shape[1], out_b.shape[1]
        in_specs += [pl.BlockSpec((bm, wa), block), pl.BlockSpec((bm, wb), block), _resident((wa + wb, d))]
        args += [out_a, out_b, w_o]
    hbm = pl.BlockSpec(memory_space=pl.ANY)
    in_specs += [_resident((1, d)), hbm, hbm, hbm]
    args += [gain.reshape(1, d), wg, wu, wd]
    if final_norm:
        in_specs.append(_resident((1, d)))
        args.append(final_gain.reshape(1, d))
    xres_rows = bm if mix is not None else 8
    return pl.pallas_call(
        functools.partial(_ffn_kernel, mod_row=mod_row, tiles_per_seq=tiles_per_seq,
                          mix_residual=mix is not None, final_norm=final_norm),
        grid=(n_blocks + 1,),
        in_specs=in_specs,
        out_specs=pl.BlockSpec((bm, d), block),
        out_shape=jax.ShapeDtypeStruct((m, d), F32),
        scratch_shapes=[pltpu.VMEM((n_w, d, fw), BF16), pltpu.VMEM((n_w, d, fw), BF16), pltpu.VMEM((f, d), BF16),
                        pltpu.VMEM((tm, d), BF16), pltpu.VMEM((tm, d), BF16),
                        pltpu.VMEM((xres_rows, d), F32), pltpu.VMEM((tm, f), BF16),
                        pltpu.VMEM((gu_slots, d // stage_chunks, f), F32),
                        pltpu.VMEM((d_slots, f // stage_chunks, d), F32),
                        pltpu.SemaphoreType.DMA((gu_slots,)), pltpu.SemaphoreType.DMA((d_slots,))],
        compiler_params=_params(1),
        name="ffn_final" if final_norm else "ffn",
    )(*args)


_QA0, _KVA0, _QLAT0, _KVLAT0, _KR0, _KRS0, _PROJ_W = 0, 512, 768, 1024, 1152, 1216, 1280
_UQ_ROPE0 = MLA_HEADS * MLA_NOPE
_UQ_ROPES0 = _UQ_ROPE0 + MLA_HEADS * MLA_ROPE


def _mix_proj_kernel(x_ref, mod_ref, g_ref, win_ref, qn_ref, kvn_ref, wuq_ref, wukv_ref, cos_ref, sin_ref,
                     qa_ref, kva_ref, qm_ref, km_ref, vm_ref, win_s, *, blocks_per_seq, sub_rows):
    @pl.when(pl.program_id(0) == 0)
    def _():
        w = win_ref[...]
        half = MLA_ROPE // 2
        win_s[:, 0:_KRS0] = w.astype(BF16)
        win_s[:, _KRS0:_KRS0 + half] = w[:, _KR0 + half:_KRS0].astype(BF16)
        win_s[:, _KRS0 + half:_PROJ_W] = w[:, _KR0:_KR0 + half].astype(BF16)

    b = pl.program_id(0) // blocks_per_seq
    shift = _mod_row(mod_ref, 3, b)
    scale = _mod_row(mod_ref, 4, b)
    q_scale = MLA_QK ** -0.5 * LOG2E
    for r0 in range(0, x_ref.shape[0], sub_rows):
        rows = slice(r0, r0 + sub_rows)
        h = (_rms_norm(x_ref[rows, :], g_ref[...]) * (1.0 + scale) + shift).astype(BF16)
        proj = _dot(h, win_s[...])
        for hd in range(SWA_HEADS):
            qa_ref[0, hd, rows, :] = (proj[:, _QA0 + hd * SWA_HEAD_DIM:_QA0 + (hd + 1) * SWA_HEAD_DIM]
                                      * (SWA_HEAD_DIM ** -0.5 * LOG2E)).astype(BF16)
        kva_ref[rows, :] = proj[:, _KVA0:_QLAT0].astype(BF16)

        q_lat = _rms_norm(proj[:, _QLAT0:_KVLAT0], qn_ref[...]).astype(BF16)
        kv_lat = _rms_norm(proj[:, _KVLAT0:_KR0], kvn_ref[...]).astype(BF16)
        q_all = _dot(q_lat, wuq_ref[...])
        kv_all = _dot(kv_lat, wukv_ref[...])

        cos = cos_ref[rows, :]
        sin = sin_ref[rows, :]
        k_rope = (proj[:, _KR0:_KRS0] * cos + proj[:, _KRS0:_PROJ_W] * sin).astype(BF16)
        for hd in range(MLA_HEADS):
            q_nope = q_all[:, hd * MLA_NOPE:(hd + 1) * MLA_NOPE]
            q_rope = (q_all[:, _UQ_ROPE0 + hd * MLA_ROPE:_UQ_ROPE0 + (hd + 1) * MLA_ROPE] * cos
                      + q_all[:, _UQ_ROPES0 + hd * MLA_ROPE:_UQ_ROPES0 + (hd + 1) * MLA_ROPE] * sin)
            qm_ref[0, hd, rows, 0:MLA_NOPE] = (q_nope * q_scale).astype(BF16)
            qm_ref[0, hd, rows, MLA_NOPE:MLA_QK] = (q_rope * q_scale).astype(BF16)
            kv0 = hd * (MLA_NOPE + MLA_V)
            km_ref[0, hd, rows, 0:MLA_NOPE] = kv_all[:, kv0:kv0 + MLA_NOPE].astype(BF16)
            km_ref[0, hd, rows, MLA_NOPE:MLA_QK] = k_rope
            vm_ref[0, hd, rows, :] = kv_all[:, kv0 + MLA_NOPE:kv0 + MLA_NOPE + MLA_V].astype(BF16)


def _rope_tables(seq):
    inv = np.float32(ROPE_THETA) ** (-np.arange(0, MLA_ROPE, 2, dtype=np.float32) / np.float32(MLA_ROPE))
    ang = np.arange(seq, dtype=np.float32)[:, None] * inv[None, :].astype(np.float32)
    cos, sin = np.cos(ang).astype(np.float32), np.sin(ang).astype(np.float32)
    return np.concatenate([cos, cos], axis=-1), np.concatenate([-sin, sin], axis=-1)


def _swap_halves(w):
    half = w.shape[-1] // 2
    return jnp.concatenate([w[..., half:], w[..., :half]], axis=-1)


def _mix_proj(x2d, mod3, gain, w_in, q_norm, kv_norm, w_uq, w_ukv, *, batch, seq, tm=1024, sub_rows=512):
    m, d = x2d.shape
    tiles_per_seq = seq // tm
    assert seq % tm == 0 and tm % sub_rows == 0
    w_uq_h = w_uq.reshape(MLA_Q_RANK, MLA_HEADS, MLA_QK)
    uq_rope = w_uq_h[:, :, MLA_NOPE:]
    w_uq_r = jnp.concatenate([w_uq_h[:, :, :MLA_NOPE].reshape(MLA_Q_RANK, -1),
                              uq_rope.reshape(MLA_Q_RANK, -1),
                              _swap_halves(uq_rope).reshape(MLA_Q_RANK, -1)], axis=1).astype(BF16)
    w_ukv_b = w_ukv.astype(BF16)
    cos2, sin2 = (jnp.asarray(tab) for tab in _rope_tables(seq))

    def head_spec(width):
        return pl.BlockSpec((1, MLA_HEADS, tm, width),
                            lambda i: (i // tiles_per_seq, 0, i % tiles_per_seq, 0))

    def head_shape(width):
        return jax.ShapeDtypeStruct((batch, MLA_HEADS, seq, width), BF16)

    return pl.pallas_call(
        functools.partial(_mix_proj_kernel, blocks_per_seq=tiles_per_seq, sub_rows=sub_rows),
        grid=(m // tm,),
        in_specs=[pl.BlockSpec((tm, d), lambda i: (i, 0)),
                  _resident(mod3.shape),
                  _resident((1, d)),
                  _resident(w_in.shape),
                  _resident((1, MLA_Q_RANK)), _resident((1, MLA_KV_RANK)),
                  _resident(w_uq_r.shape), _resident(w_ukv_b.shape),
                  pl.BlockSpec((tm, MLA_ROPE), lambda i: (i % tiles_per_seq, 0)),
                  pl.BlockSpec((tm, MLA_ROPE), lambda i: (i % tiles_per_seq, 0))],
        out_specs=[pl.BlockSpec((1, SWA_HEADS, tm, SWA_HEAD_DIM),
                                lambda i: (i // tiles_per_seq, 0, i % tiles_per_seq, 0)),
                   pl.BlockSpec((tm, 2 * SWA_KV_HEADS * SWA_HEAD_DIM), lambda i: (i, 0)),
                   head_spec(MLA_QK), head_spec(MLA_QK), head_spec(MLA_V)],
        out_shape=[jax.ShapeDtypeStruct((batch, SWA_HEADS, seq, SWA_HEAD_DIM), BF16),
                   jax.ShapeDtypeStruct((m, 2 * SWA_KV_HEADS * SWA_HEAD_DIM), BF16),
                   head_shape(MLA_QK), head_shape(MLA_QK), head_shape(MLA_V)],
        scratch_shapes=[pltpu.VMEM((d, _PROJ_W), BF16)],
        compiler_params=_params(1),
        name="mix_proj",
    )(x2d, mod3, gain.reshape(1, d), w_in, q_norm.reshape(1, -1), kv_norm.reshape(1, -1),
      w_uq_r, w_ukv_b, cos2, sin2)


def _t5_bucket_table():
    qi = np.arange(WINDOW)[:, None]
    kj = np.arange(2 * WINDOW)[None, :]
    dist = qi + WINDOW - kj
    max_exact = NUM_BUCKETS // 2
    n = np.maximum(dist, 0)
    nf = np.maximum(n, 1).astype(np.float32)
    large = max_exact + (np.log(nf / np.float32(max_exact)) / np.float32(math.log(MAX_DISTANCE / max_exact))
                         * np.float32(NUM_BUCKETS - max_exact)).astype(np.int32)
    large = np.minimum(large, NUM_BUCKETS - 1)
    bucket = np.where(n < max_exact, n, large)
    band = (dist >= 0) & (dist < WINDOW)
    return np.where(band, bucket, -1).astype(np.int32)


def _swa_bias_kernel(rel_ref, bucket_ref, o_ref):
    bucket = bucket_ref[...]
    first_block_ok = lax.broadcasted_iota(jnp.int32, bucket.shape, 0) >= WINDOW
    for hd in range(SWA_HEADS):
        acc = jnp.full(bucket.shape, MASK_VALUE, F32)
        for b in range(NUM_BUCKETS):
            acc = jnp.where(bucket == b, rel_ref[b, hd] * LOG2E, acc)
        g, j = divmod(hd, SWA_GROUP)
        o_ref[0, g, :, j * WINDOW:(j + 1) * WINDOW] = jnp.where(first_block_ok, acc, MASK_VALUE)
        o_ref[1, g, :, j * WINDOW:(j + 1) * WINDOW] = acc


def _swa_bias(rel_bias):
    shape = (2, SWA_KV_HEADS, 2 * WINDOW, SWA_GROUP * WINDOW)
    return pl.pallas_call(
        _swa_bias_kernel,
        in_specs=[pl.BlockSpec(memory_space=pltpu.SMEM),
                  pl.BlockSpec((2 * WINDOW, WINDOW), lambda: (0, 0))],
        out_specs=pl.BlockSpec(shape, lambda: (0, 0, 0, 0)),
        out_shape=jax.ShapeDtypeStruct(shape, F32),
        name="swa_bias",
    )(rel_bias, jnp.asarray(_t5_bucket_table().T))


def _swa_kernel(sink_ref, q_ref, kv_ref, bias_ref, o_ref, kp_ref, vt_ref, *, blocks_per_iter):
    seq = kv_ref.shape[0]
    n_blocks = seq // WINDOW
    kdim = SWA_KV_HEADS * SWA_HEAD_DIM
    kp_ref[0:WINDOW, :] = jnp.zeros((WINDOW, kdim), BF16)
    kp_ref[WINDOW:, :] = kv_ref[:, 0:kdim]
    ones_rows = jnp.ones((SWA_ONES_ROWS, WINDOW), BF16)
    for g in range(SWA_KV_HEADS):
        vt_ref[0, g] = jnp.concatenate([jnp.zeros((SWA_HEAD_DIM, WINDOW), BF16), ones_rows], axis=0)
    for n in range(n_blocks):
        for g in range(SWA_KV_HEADS):
            v_blk = kv_ref[n * WINDOW:(n + 1) * WINDOW, kdim + g * SWA_HEAD_DIM:kdim + (g + 1) * SWA_HEAD_DIM]
            vt_ref[n + 1, g] = jnp.concatenate([v_blk.astype(F32).T.astype(BF16), ones_rows], axis=0)
    sink_rows = [[jnp.full((1, WINDOW), sink_ref[g * SWA_GROUP + j] * LOG2E, F32) for j in range(SWA_GROUP)]
                 for g in range(SWA_KV_HEADS)]

    def body(it, carry):
        units = [(b, g) for b in range(blocks_per_iter) for g in range(SWA_KV_HEADS)]
        blocks = [it * blocks_per_iter + b for b in range(blocks_per_iter)]
        q_starts = [pl.multiple_of(n * WINDOW, WINDOW) for n in blocks]
        tables = [jnp.minimum(n, 1) for n in blocks]
        sts, pts = {}, {}

        def scores(u):
            b, g = units[u]
            k = kp_ref[pl.ds(q_starts[b], 2 * WINDOW), g * SWA_HEAD_DIM:(g + 1) * SWA_HEAD_DIM]
            sts[u] = []
            for j in range(0, SWA_GROUP, 2):
                hd = g * SWA_GROUP + j
                q = jnp.concatenate([q_ref[0, hd, pl.ds(q_starts[b], WINDOW), :],
                                     q_ref[0, hd + 1, pl.ds(q_starts[b], WINDOW), :]], axis=0)
                st = lax.dot_general(k, q, NT_DIMS, preferred_element_type=F32)
                sts[u] += [st[:, 0:WINDOW], st[:, WINDOW:2 * WINDOW]]

        def softmax(u):
            b, g = units[u]
            pts[u] = []
            for j in range(SWA_GROUP):
                st = sts[u][j] + bias_ref[tables[b], g, :, j * WINDOW:(j + 1) * WINDOW]
                mx = jnp.maximum(jnp.max(st, axis=0, keepdims=True), sink_rows[g][j])
                pts[u].append((jnp.exp2(st - mx).astype(BF16), jnp.exp2(sink_rows[g][j] - mx)))

        def values(u):
            b, g = units[u]
            vt = jnp.concatenate([vt_ref[blocks[b], g], vt_ref[blocks[b] + 1, g]], axis=1)
            for j in range(0, SWA_GROUP, 2):
                pt = jnp.concatenate([pts[u][j][0], pts[u][j + 1][0]], axis=1)
                ot = _dot(vt, pt)
                halves = []
                for jj in range(2):
                    lanes = slice(jj * WINDOW, (jj + 1) * WINDOW)
                    denom = ot[SWA_HEAD_DIM:SWA_HEAD_DIM + 1, lanes] + pts[u][j + jj][1]
                    halves.append(ot[0:SWA_HEAD_DIM, lanes] * (1.0 / denom))
                pair = jnp.concatenate(halves, axis=0).T
                lane0 = (g * SWA_GROUP + j) * SWA_HEAD_DIM
                o_ref[pl.ds(q_starts[b], WINDOW), lane0:lane0 + 2 * SWA_HEAD_DIM] = pair.astype(BF16)

        for t in range(len(units) + 2):
            if t < len(units):
                scores(t)
            if 0 <= t - 1 < len(units):
                softmax(t - 1)
            if 0 <= t - 2 < len(units):
                values(t - 2)
        return carry

    lax.fori_loop(0, n_blocks // blocks_per_iter, body, 0)


def _swa(qa, kva, bias, sinks, *, batch, seq, blocks_per_iter=16):
    kvw = kva.shape[1]
    qw = SWA_HEADS * SWA_HEAD_DIM
    assert (seq // WINDOW) % blocks_per_iter == 0
    return pl.pallas_call(
        functools.partial(_swa_kernel, blocks_per_iter=blocks_per_iter),
        grid=(batch,),
        in_specs=[pl.BlockSpec(memory_space=pltpu.SMEM),
                  pl.BlockSpec((1, SWA_HEADS, seq, SWA_HEAD_DIM), lambda b: (b, 0, 0, 0)),
                  pl.BlockSpec((seq, kvw), lambda b: (b, 0)),
                  _resident(bias.shape)],
        out_specs=pl.BlockSpec((seq, qw), lambda b: (b, 0)),
        out_shape=jax.ShapeDtypeStruct((batch * seq, qw), BF16),
        scratch_shapes=[pltpu.VMEM((WINDOW + seq, SWA_KV_HEADS * SWA_HEAD_DIM), BF16),
                        pltpu.VMEM((1 + seq // WINDOW, SWA_KV_HEADS, SWA_HEAD_DIM + SWA_ONES_ROWS, WINDOW), BF16)],
        compiler_params=_params(1),
        name="swa",
    )(sinks, qa, kva, bias)


def _mla_kernel(q_ref, k_ref, v_ref, o_ref, vt_ref, *, tq):
    heads, seq, dv = v_ref.shape[1], v_ref.shape[2], v_ref.shape[3]
    n_tiles = seq // tq
    key = lax.broadcasted_iota(jnp.int32, (tq, tq), 0)
    qry = lax.broadcasted_iota(jnp.int32, (tq, tq), 1)
    causal = key <= qry
    for hd in range(heads):
        for c in range(seq // WINDOW):
            v_blk = v_ref[0, hd, c * WINDOW:(c + 1) * WINDOW, :]
            vt_ref[hd, 0:dv, c * WINDOW:(c + 1) * WINDOW] = v_blk.astype(F32).T.astype(BF16)
        vt_ref[hd, dv:, :] = jnp.ones((MLA_ONES_ROWS, seq), BF16)
    sts, mxs, pts, accs = {}, {}, {}, {}

    def scores(hd, i, j):
        q = q_ref[0, hd, i * tq:(i + 1) * tq, :]
        s = lax.dot_general(k_ref[0, hd, j * tq:(j + 1) * tq, :], q, NT_DIMS, preferred_element_type=F32)
        sts[hd, i, j] = jnp.where(causal, s, MASK_VALUE) if j == i else s

    def col_max(hd, i, j):
        m = jnp.max(sts[hd, i, j], axis=0, keepdims=True)
        mxs[hd, i] = m if j == 0 else jnp.maximum(mxs[hd, i], m)

    def probs(hd, i, j):
        pts[hd, i, j] = jnp.exp2(sts.pop((hd, i, j)) - mxs[hd, i]).astype(BF16)

    def values(hd, i, j):
        part = _dot(vt_ref[hd, :, j * tq:(j + 1) * tq], pts.pop((hd, i, j)))
        accs[hd, i] = part if j == 0 else accs[hd, i] + part
        if j == i:
            ot = accs.pop((hd, i))
            o = ot[0:dv, :] * (1.0 / ot[dv:dv + 1, :])
            o_ref[i * tq:(i + 1) * tq, hd * dv:(hd + 1) * dv] = o.T.astype(BF16)

    stages = (scores, col_max, probs, values)
    units = [(hd, i) for hd in range(heads) for i in range(n_tiles)]
    for t in range(len(units) + len(stages) - 1):
        for j in range(n_tiles):
            for lag, stage in enumerate(stages):
                if 0 <= t - lag < len(units):
                    hd, i = units[t - lag]
                    if j <= i:
                        stage(hd, i, j)


def _mla(qm, km, vm, *, tq=256):
    batch, heads, seq, qk = qm.shape
    dv = vm.shape[-1]
    return pl.pallas_call(
        functools.partial(_mla_kernel, tq=tq),
        grid=(batch,),
        in_specs=[pl.BlockSpec((1, heads, seq, qk), lambda b: (b, 0, 0, 0)),
                  pl.BlockSpec((1, heads, seq, qk), lambda b: (b, 0, 0, 0)),
                  pl.BlockSpec((1, heads, seq, dv), lambda b: (b, 0, 0, 0))],
        out_specs=pl.BlockSpec((seq, heads * dv), lambda b: (b, 0)),
        out_shape=jax.ShapeDtypeStruct((batch * seq, heads * dv), BF16),
        scratch_shapes=[pltpu.VMEM((heads, dv + MLA_ONES_ROWS, seq), BF16)],
        compiler_params=_params(1),
        name="mla",
    )(qm, km, vm)


def kernel(x, c, w_mod, b_mod, norm_ffn1, ffn1_gate, ffn1_up, ffn1_down, norm_mix, w_in, q_norm, kv_norm, w_uq, w_ukv, sinks, w_o, norm_ffn2, ffn2_gate, ffn2_up, ffn2_down, rel_bias, norm_final):
    batch, seq, d = x.shape
    depth = w_mod.shape[0]
    x2d = x.reshape(batch * seq, d)
    bias = _swa_bias(rel_bias)
    for l in range(depth):
        mod3 = _mod(c, w_mod[l], b_mod[l])
        x2d = _ffn(x2d, mod3, norm_ffn1[l], ffn1_gate[l], ffn1_up[l], ffn1_down[l], mod_row=0, seq=seq)
        qa, kva, qm, km, vm = _mix_proj(x2d, mod3, norm_mix[l], w_in[l], q_norm[l], kv_norm[l],
                                        w_uq[l], w_ukv[l], batch=batch, seq=seq)
        out_a = _swa(qa, kva, bias, sinks[l], batch=batch, seq=seq)
        out_b = _mla(qm, km, vm)
        last = l == depth - 1
        x2d = _ffn(x2d, mod3, norm_ffn2[l], ffn2_gate[l], ffn2_up[l], ffn2_down[l], mod_row=6, seq=seq,
                   mix=(out_a, out_b, w_o[l].astype(BF16)),
                   final_gain=norm_final if last else None)
    return x2d.reshape(batch, seq, d)
```

```python
import functools
import math

import jax
import jax.numpy as jnp
import numpy as np
from jax import lax
from jax.experimental import pallas as pl
from jax.experimental.pallas import tpu as pltpu

EPS = 1e-6
FFN_RES_WEIGHT = 0.5
N_MOD = 9

SWA_HEADS = 8
SWA_KV_HEADS = 2
SWA_HEAD_DIM = 64
SWA_GROUP = SWA_HEADS // SWA_KV_HEADS
WINDOW = 128

MLA_HEADS = 4
MLA_Q_RANK = 256
MLA_KV_RANK = 128
MLA_NOPE = 128
MLA_ROPE = 64
MLA_V = 128
MLA_QK = MLA_NOPE + MLA_ROPE
ROPE_THETA = 10000.0

NUM_BUCKETS = 32
MAX_DISTANCE = 128

SWA_ONES_ROWS = 16
MLA_ONES_ROWS = 16
LOG2E = math.log2(math.e)
MASK_VALUE = -1e30
V7X_VMEM_LIMIT_BYTES = 58 * 1024 * 1024

F32 = jnp.float32
BF16 = jnp.bfloat16
NT_DIMS = (((1,), (1,)), ((), ()))


def _params(n_grid_dims):
    return pltpu.CompilerParams(dimension_semantics=("arbitrary",) * n_grid_dims,
                                vmem_limit_bytes=V7X_VMEM_LIMIT_BYTES)


def _resident(shape):
    return pl.BlockSpec(shape, lambda *_: (0,) * len(shape), pipeline_mode=pl.Buffered(1))


def _rms_norm(x, gain):
    ms = jnp.mean(x * x, axis=-1, keepdims=True)
    return x * lax.rsqrt(ms + EPS) * gain


def _silu(x):
    return x * (1.0 / (1.0 + jnp.exp(-x)))


def _dot(a, b):
    return jnp.dot(a, b, preferred_element_type=F32)


def _mod_kernel(c_ref, w_ref, b_ref, o_ref):
    c_act = _silu(c_ref[...]).astype(BF16)
    res = _dot(c_act, w_ref[...].astype(BF16)) + b_ref[...]
    d = o_ref.shape[2]
    for r in range(o_ref.shape[0]):
        o_ref[r] = res[:, r * d:(r + 1) * d]


def _mod(c, w_mod, b_mod, *, rows_per_step=3):
    batch, d = c.shape
    n = w_mod.shape[1]
    tn = rows_per_step * d
    return pl.pallas_call(
        _mod_kernel,
        grid=(n // tn,),
        in_specs=[pl.BlockSpec((batch, d), lambda j: (0, 0)),
                  pl.BlockSpec((d, tn), lambda j: (0, j)),
                  pl.BlockSpec((1, tn), lambda j: (0, j))],
        out_specs=pl.BlockSpec((rows_per_step, batch, d), lambda j: (j, 0, 0)),
        out_shape=jax.ShapeDtypeStruct((n // d, batch, d), F32),
        compiler_params=_params(1),
        name="mod",
    )(c, w_mod, b_mod.reshape(1, n))


def _mod_row(mod_ref, row, b):
    return mod_ref[row, pl.ds(b, 1), :]


def _ffn_kernel(*refs, mod_row, tiles_per_seq, mix_residual, final_norm):
    refs = list(refs)
    x_ref, mod_ref = refs[:2]
    del refs[:2]
    if mix_residual:
        oa_ref, ob_ref, wo_ref = refs[:3]
        del refs[:3]
    g_ref, wg_hbm, wu_hbm, wd_hbm = refs[:4]
    del refs[:4]
    if final_norm:
        gf_ref = refs.pop(0)
    (o_ref, wg_s, wu_s, wd_s, h_even, h_odd, xres_s, a_s,
     stage_gu, stage_d, sem_gu, sem_d) = refs
    n_w, _, fw = wg_s.shape
    tm = h_even.shape[0]
    n_sub = x_ref.shape[0] // tm
    t = pl.program_id(0)
    gu_slots, d_slots, d_rows = stage_gu.shape[0], stage_d.shape[0], stage_d.shape[1]
    n_d = wd_hbm.shape[0] // d_rows

    def gu_copies(c):
        cols = pl.ds(c * fw, fw)
        return [pltpu.make_async_copy(src.at[:, cols], stage_gu.at[c % gu_slots, k], sem_gu.at[c % gu_slots, k])
                for k, src in enumerate((wg_hbm, wu_hbm))]

    def d_copy(m):
        return pltpu.make_async_copy(wd_hbm.at[pl.ds(m * d_rows, d_rows), :], stage_d.at[m % d_slots],
                                     sem_d.at[m % d_slots])

    def stream_begin():
        for c in range(gu_slots):
            for cp in gu_copies(c):
                cp.start()
        for m in range(d_slots):
            d_copy(m).start()

    def stream_gate_up(c):
        for cp in gu_copies(c):
            cp.wait()
        wg_s[c] = stage_gu[c % gu_slots, 0].astype(BF16)
        wu_s[c] = stage_gu[c % gu_slots, 1].astype(BF16)
        if c + gu_slots < n_w:
            for cp in gu_copies(c + gu_slots):
                cp.start()

    def stream_down(m):
        d_copy(m).wait()
        wd_s[m * d_rows:(m + 1) * d_rows, :] = stage_d[m % d_slots].astype(BF16)
        if m + d_slots < n_d:
            d_copy(m + d_slots).start()

    def batch_of(sub):
        return (t * n_sub + sub) // tiles_per_seq

    def norm_stage(sub, h_next):
        rows = pl.ds(sub * tm, tm)
        b = batch_of(sub)
        x = x_ref[rows, :]
        if mix_residual:
            wa = oa_ref.shape[1]
            mix = _dot(oa_ref[rows, :], wo_ref[0:wa, :]) + _dot(ob_ref[rows, :], wo_ref[wa:, :])
            x = x + _mod_row(mod_ref, mod_row - 1, b) * mix
            xres_s[rows, :] = x
        gain = g_ref[...] * (1.0 + _mod_row(mod_ref, mod_row + 1, b))
        ms = jnp.mean(x * x, axis=-1, keepdims=True)
        h_next[...] = (x * lax.rsqrt(ms + EPS) * gain + _mod_row(mod_ref, mod_row, b)).astype(BF16)

    def matmul_stage(sub, h_cur, stream_weights):
        rows = pl.ds(sub * tm, tm)
        if stream_weights:
            stream_begin()
        for c in range(n_w):
            if stream_weights:
                stream_gate_up(c)
            g = _dot(h_cur[...], wg_s[c])
            u = _dot(h_cur[...], wu_s[c])
            a_s[:, c * fw:(c + 1) * fw] = (_silu(g) * u).astype(BF16)
            if stream_weights and c < n_d:
                stream_down(c)
        if stream_weights:
            for m in range(n_w, n_d):
                stream_down(m)
        y = _dot(a_s[...], wd_s[...])
        x = xres_s[rows, :] if mix_residual else x_ref[rows, :]
        out = x + FFN_RES_WEIGHT * _mod_row(mod_ref, mod_row + 2, batch_of(sub)) * y
        if final_norm:
            out = _rms_norm(out, gf_ref[...])
        o_ref[rows, :] = out

    def block(first):
        bufs = (h_even, h_odd)
        norm_stage(0, bufs[0])
        for sub in range(n_sub):
            if sub + 1 < n_sub:
                norm_stage(sub + 1, bufs[(sub + 1) % 2])
            matmul_stage(sub, bufs[sub % 2], stream_weights=first and sub == 0)

    @pl.when(t == 0)
    def _():
        block(first=True)

    @pl.when(t > 0)
    def _():
        block(first=False)


def _ffn(x2d, mod3, gain, wg, wu, wd, *, mod_row, seq, mix=None, final_gain=None, tm=512, tiles_per_step=2, fw=256,
         down_chunks=11, gu_slots=2, d_slots=2):
    m, d = x2d.shape
    f = wg.shape[1]
    bm = tm * tiles_per_step
    assert f % fw == 0 and seq % tm == 0 and m % bm == 0 and f % down_chunks == 0
    n_w, n_blocks, tiles_per_seq = f // fw, m // bm, seq // tm
    final_norm = final_gain is not None

    def block(s):
        return (s, 0)

    in_specs = [pl.BlockSpec((bm, d), block), _resident(mod3.shape)]
    args = [x2d, mod3]
    if mix is not None:
        out_a, out_b, w_o = mix
        wa, wb = out_a.shape[1], out_b.shape[1]
        in_specs += [pl.BlockSpec((bm, wa), block), pl.BlockSpec((bm, wb), block), _resident((wa + wb, d))]
        args += [out_a, out_b, w_o]
    hbm = pl.BlockSpec(memory_space=pl.ANY)
    in_specs += [_resident((1, d)), hbm, hbm, hbm]
    args += [gain.reshape(1, d), wg, wu, wd]
    if final_norm:
        in_specs.append(_resident((1, d)))
        args.append(final_gain.reshape(1, d))
    xres_rows = bm if mix is not None else 8
    return pl.pallas_call(
        functools.partial(_ffn_kernel, mod_row=mod_row, tiles_per_seq=tiles_per_seq,
                          mix_residual=mix is not None, final_norm=final_norm),
        grid=(n_blocks,),
        in_specs=in_specs,
        out_specs=pl.BlockSpec((bm, d), block),
        out_shape=jax.ShapeDtypeStruct((m, d), F32),
        scratch_shapes=[pltpu.VMEM((n_w, d, fw), BF16), pltpu.VMEM((n_w, d, fw), BF16), pltpu.VMEM((f, d), BF16),
                        pltpu.VMEM((tm, d), BF16), pltpu.VMEM((tm, d), BF16),
                        pltpu.VMEM((xres_rows, d), F32), pltpu.VMEM((tm, f), BF16),
                        pltpu.VMEM((gu_slots, 2, d, fw), F32),
                        pltpu.VMEM((d_slots, f // down_chunks, d), F32),
                        pltpu.SemaphoreType.DMA((gu_slots, 2)), pltpu.SemaphoreType.DMA((d_slots,))],
        compiler_params=_params(1),
        name="ffn_final" if final_norm else "ffn",
    )(*args)


_QA0, _KVA0, _QLAT0, _KVLAT0, _KR0, _KRS0, _PROJ_W = 0, 512, 768, 1024, 1152, 1216, 1280
_UQ_ROPE0 = MLA_HEADS * MLA_NOPE
_UQ_ROPES0 = _UQ_ROPE0 + MLA_HEADS * MLA_ROPE


def _mix_proj_kernel(x_ref, mod_ref, g_ref, win_ref, qn_ref, kvn_ref, wuq_ref, wukv_ref, cos_ref, sin_ref,
                     qa_ref, kva_ref, qm_ref, km_ref, vm_ref, win_s, *, blocks_per_seq, sub_rows):
    @pl.when(pl.program_id(0) == 0)
    def _():
        w = win_ref[...]
        half = MLA_ROPE // 2
        win_s[:, 0:_KRS0] = w.astype(BF16)
        win_s[:, _KRS0:_KRS0 + half] = w[:, _KR0 + half:_KRS0].astype(BF16)
        win_s[:, _KRS0 + half:_PROJ_W] = w[:, _KR0:_KR0 + half].astype(BF16)

    b = pl.program_id(0) // blocks_per_seq
    shift = _mod_row(mod_ref, 3, b)
    scale = _mod_row(mod_ref, 4, b)
    q_scale = MLA_QK ** -0.5 * LOG2E
    for r0 in range(0, x_ref.shape[0], sub_rows):
        rows = slice(r0, r0 + sub_rows)
        h = (_rms_norm(x_ref[rows, :], g_ref[...]) * (1.0 + scale) + shift).astype(BF16)
        proj = _dot(h, win_s[...])
        for hd in range(SWA_HEADS):
            qa_ref[0, hd, rows, :] = (proj[:, _QA0 + hd * SWA_HEAD_DIM:_QA0 + (hd + 1) * SWA_HEAD_DIM]
                                      * (SWA_HEAD_DIM ** -0.5 * LOG2E)).astype(BF16)
        kva_ref[rows, :] = proj[:, _KVA0:_QLAT0].astype(BF16)

        q_lat = _rms_norm(proj[:, _QLAT0:_KVLAT0], qn_ref[...]).astype(BF16)
        kv_lat = _rms_norm(proj[:, _KVLAT0:_KR0], kvn_ref[...]).astype(BF16)
        q_all = _dot(q_lat, wuq_ref[...])
        kv_all = _dot(kv_lat, wukv_ref[...])

        cos = cos_ref[rows, :]
        sin = sin_ref[rows, :]
        k_rope = (proj[:, _KR0:_KRS0] * cos + proj[:, _KRS0:_PROJ_W] * sin).astype(BF16)
        for hd in range(MLA_HEADS):
            q_nope = q_all[:, hd * MLA_NOPE:(hd + 1) * MLA_NOPE]
            q_rope = (q_all[:, _UQ_ROPE0 + hd * MLA_ROPE:_UQ_ROPE0 + (hd + 1) * MLA_ROPE] * cos
                      + q_all[:, _UQ_ROPES0 + hd * MLA_ROPE:_UQ_ROPES0 + (hd + 1) * MLA_ROPE] * sin)
            qm_ref[0, hd, rows, 0:MLA_NOPE] = (q_nope * q_scale).astype(BF16)
            qm_ref[0, hd, rows, MLA_NOPE:MLA_QK] = (q_rope * q_scale).astype(BF16)
            kv0 = hd * (MLA_NOPE + MLA_V)
            km_ref[0, hd, rows, 0:MLA_NOPE] = kv_all[:, kv0:kv0 + MLA_NOPE].astype(BF16)
            km_ref[0, hd, rows, MLA_NOPE:MLA_QK] = k_rope
            vm_ref[0, hd, rows, :] = kv_all[:, kv0 + MLA_NOPE:kv0 + MLA_NOPE + MLA_V].astype(BF16)


def _rope_tables(seq):
    inv = np.float32(ROPE_THETA) ** (-np.arange(0, MLA_ROPE, 2, dtype=np.float32) / np.float32(MLA_ROPE))
    ang = np.arange(seq, dtype=np.float32)[:, None] * inv[None, :].astype(np.float32)
    cos, sin = np.cos(ang).astype(np.float32), np.sin(ang).astype(np.float32)
    return np.concatenate([cos, cos], axis=-1), np.concatenate([-sin, sin], axis=-1)


def _swap_halves(w):
    half = w.shape[-1] // 2
    return jnp.concatenate([w[..., half:], w[..., :half]], axis=-1)


def _mix_proj(x2d, mod3, gain, w_in, q_norm, kv_norm, w_uq, w_ukv, *, batch, seq, tm=1024, sub_rows=512):
    m, d = x2d.shape
    tiles_per_seq = seq // tm
    assert seq % tm == 0 and tm % sub_rows == 0
    w_uq_h = w_uq.reshape(MLA_Q_RANK, MLA_HEADS, MLA_QK)
    uq_rope = w_uq_h[:, :, MLA_NOPE:]
    w_uq_r = jnp.concatenate([w_uq_h[:, :, :MLA_NOPE].reshape(MLA_Q_RANK, -1),
                              uq_rope.reshape(MLA_Q_RANK, -1),
                              _swap_halves(uq_rope).reshape(MLA_Q_RANK, -1)], axis=1).astype(BF16)
    w_ukv_b = w_ukv.astype(BF16)
    cos2, sin2 = (jnp.asarray(tab) for tab in _rope_tables(seq))

    def head_spec(width):
        return pl.BlockSpec((1, MLA_HEADS, tm, width),
                            lambda i: (i // tiles_per_seq, 0, i % tiles_per_seq, 0))

    def head_shape(width):
        return jax.ShapeDtypeStruct((batch, MLA_HEADS, seq, width), BF16)

    return pl.pallas_call(
        functools.partial(_mix_proj_kernel, blocks_per_seq=tiles_per_seq, sub_rows=sub_rows),
        grid=(m // tm,),
        in_specs=[pl.BlockSpec((tm, d), lambda i: (i, 0)),
                  _resident(mod3.shape),
                  _resident((1, d)),
                  _resident(w_in.shape),
                  _resident((1, MLA_Q_RANK)), _resident((1, MLA_KV_RANK)),
                  _resident(w_uq_r.shape), _resident(w_ukv_b.shape),
                  pl.BlockSpec((tm, MLA_ROPE), lambda i: (i % tiles_per_seq, 0)),
                  pl.BlockSpec((tm, MLA_ROPE), lambda i: (i % tiles_per_seq, 0))],
        out_specs=[pl.BlockSpec((1, SWA_HEADS, tm, SWA_HEAD_DIM),
                                lambda i: (i // tiles_per_seq, 0, i % tiles_per_seq, 0)),
                   pl.BlockSpec((tm, 2 * SWA_KV_HEADS * SWA_HEAD_DIM), lambda i: (i, 0)),
                   head_spec(MLA_QK), head_spec(MLA_QK), head_spec(MLA_V)],
        out_shape=[jax.ShapeDtypeStruct((batch, SWA_HEADS, seq, SWA_HEAD_DIM), BF16),
                   jax.ShapeDtypeStruct((m, 2 * SWA_KV_HEADS * SWA_HEAD_DIM), BF16),
                   head_shape(MLA_QK), head_shape(MLA_QK), head_shape(MLA_V)],
        scratch_shapes=[pltpu.VMEM((d, _PROJ_W), BF16)],
        compiler_params=_params(1),
        name="mix_proj",
    )(x2d, mod3, gain.reshape(1, d), w_in, q_norm.reshape(1, -1), kv_norm.reshape(1, -1),
      w_uq_r, w_ukv_b, cos2, sin2)


def _t5_bucket_table():
    qi = np.arange(WINDOW)[:, None]
    kj = np.arange(2 * WINDOW)[None, :]
    dist = qi + WINDOW - kj
    max_exact = NUM_BUCKETS // 2
    n = np.maximum(dist, 0)
    nf = np.maximum(n, 1).astype(np.float32)
    large = max_exact + (np.log(nf / np.float32(max_exact)) / np.float32(math.log(MAX_DISTANCE / max_exact))
                         * np.float32(NUM_BUCKETS - max_exact)).astype(np.int32)
    large = np.minimum(large, NUM_BUCKETS - 1)
    bucket = np.where(n < max_exact, n, large)
    band = (dist >= 0) & (dist < WINDOW)
    return np.where(band, bucket, -1).astype(np.int32)


def _swa_bias_kernel(rel_ref, bucket_ref, o_ref):
    bucket = bucket_ref[...]
    first_block_ok = lax.broadcasted_iota(jnp.int32, bucket.shape, 0) >= WINDOW
    for hd in range(SWA_HEADS):
        acc = jnp.full(bucket.shape, MASK_VALUE, F32)
        for b in range(NUM_BUCKETS):
            acc = jnp.where(bucket == b, rel_ref[b, hd] * LOG2E, acc)
        g, j = divmod(hd, SWA_GROUP)
        o_ref[0, g, :, j * WINDOW:(j + 1) * WINDOW] = jnp.where(first_block_ok, acc, MASK_VALUE)
        o_ref[1, g, :, j * WINDOW:(j + 1) * WINDOW] = acc


def _swa_bias(rel_bias):
    shape = (2, SWA_KV_HEADS, 2 * WINDOW, SWA_GROUP * WINDOW)
    return pl.pallas_call(
        _swa_bias_kernel,
        in_specs=[pl.BlockSpec(memory_space=pltpu.SMEM),
                  pl.BlockSpec((2 * WINDOW, WINDOW), lambda: (0, 0))],
        out_specs=pl.BlockSpec(shape, lambda: (0, 0, 0, 0)),
        out_shape=jax.ShapeDtypeStruct(shape, F32),
        name="swa_bias",
    )(rel_bias, jnp.asarray(_t5_bucket_table().T))


def _swa_kernel(sink_ref, q_ref, kv_ref, bias_ref, o_ref, kp_ref, vt_ref, *, blocks_per_iter):
    seq = kv_ref.shape[0]
    n_blocks = seq // WINDOW
    kdim = SWA_KV_HEADS * SWA_HEAD_DIM
    kp_ref[0:WINDOW, :] = jnp.zeros((WINDOW, kdim), BF16)
    kp_ref[WINDOW:, :] = kv_ref[:, 0:kdim]
    ones_rows = jnp.ones((SWA_ONES_ROWS, WINDOW), BF16)
    for g in range(SWA_KV_HEADS):
        vt_ref[0, g] = jnp.concatenate([jnp.zeros((SWA_HEAD_DIM, WINDOW), BF16), ones_rows], axis=0)
    for n in range(n_blocks):
        for g in range(SWA_KV_HEADS):
            v_blk = kv_ref[n * WINDOW:(n + 1) * WINDOW, kdim + g * SWA_HEAD_DIM:kdim + (g + 1) * SWA_HEAD_DIM]
            vt_ref[n + 1, g] = jnp.concatenate([v_blk.astype(F32).T.astype(BF16), ones_rows], axis=0)
    sink_rows = [[jnp.full((1, WINDOW), sink_ref[g * SWA_GROUP + j] * LOG2E, F32) for j in range(SWA_GROUP)]
                 for g in range(SWA_KV_HEADS)]

    def body(it, carry):
        units = [(b, g) for b in range(blocks_per_iter) for g in range(SWA_KV_HEADS)]
        blocks = [it * blocks_per_iter + b for b in range(blocks_per_iter)]
        q_starts = [pl.multiple_of(n * WINDOW, WINDOW) for n in blocks]
        tables = [jnp.minimum(n, 1) for n in blocks]
        sts, pts = {}, {}

        def scores(u):
            b, g = units[u]
            k = kp_ref[pl.ds(q_starts[b], 2 * WINDOW), g * SWA_HEAD_DIM:(g + 1) * SWA_HEAD_DIM]
            sts[u] = []
            for j in range(0, SWA_GROUP, 2):
                hd = g * SWA_GROUP + j
                q = jnp.concatenate([q_ref[0, hd, pl.ds(q_starts[b], WINDOW), :],
                                     q_ref[0, hd + 1, pl.ds(q_starts[b], WINDOW), :]], axis=0)
                st = lax.dot_general(k, q, NT_DIMS, preferred_element_type=F32)
                sts[u] += [st[:, 0:WINDOW], st[:, WINDOW:2 * WINDOW]]

        def softmax(u):
            b, g = units[u]
            pts[u] = []
            for j in range(SWA_GROUP):
                st = sts[u][j] + bias_ref[tables[b], g, :, j * WINDOW:(j + 1) * WINDOW]
                mx = jnp.maximum(jnp.max(st, axis=0, keepdims=True), sink_rows[g][j])
                pts[u].append((jnp.exp2(st - mx).astype(BF16), jnp.exp2(sink_rows[g][j] - mx)))

        def values(u):
            b, g = units[u]
            vt = jnp.concatenate([vt_ref[blocks[b], g], vt_ref[blocks[b] + 1, g]], axis=1)
            for j in range(0, SWA_GROUP, 2):
                pt = jnp.concatenate([pts[u][j][0], pts[u][j + 1][0]], axis=1)
                ot = _dot(vt, pt)
                halves = []
                for jj in range(2):
                    lanes = slice(jj * WINDOW, (jj + 1) * WINDOW)
                    denom = ot[SWA_HEAD_DIM:SWA_HEAD_DIM + 1, lanes] + pts[u][j + jj][1]
                    halves.append(ot[0:SWA_HEAD_DIM, lanes] * (1.0 / denom))
                pair = jnp.concatenate(halves, axis=0).T
                lane0 = (g * SWA_GROUP + j) * SWA_HEAD_DIM
                o_ref[pl.ds(q_starts[b], WINDOW), lane0:lane0 + 2 * SWA_HEAD_DIM] = pair.astype(BF16)

        for t in range(len(units) + 2):
            if t < len(units):
                scores(t)
            if 0 <= t - 1 < len(units):
                softmax(t - 1)
            if 0 <= t - 2 < len(units):
                values(t - 2)
        return carry

    lax.fori_loop(0, n_blocks // blocks_per_iter, body, 0)


def _swa(qa, kva, bias, sinks, *, batch, seq, blocks_per_iter=16):
    kvw = kva.shape[1]
    qw = SWA_HEADS * SWA_HEAD_DIM
    assert (seq // WINDOW) % blocks_per_iter == 0
    return pl.pallas_call(
        functools.partial(_swa_kernel, blocks_per_iter=blocks_per_iter),
        grid=(batch,),
        in_specs=[pl.BlockSpec(memory_space=pltpu.SMEM),
                  pl.BlockSpec((1, SWA_HEADS, seq, SWA_HEAD_DIM), lambda b: (b, 0, 0, 0)),
                  pl.BlockSpec((seq, kvw), lambda b: (b, 0)),
                  _resident(bias.shape)],
        out_specs=pl.BlockSpec((seq, qw), lambda b: (b, 0)),
        out_shape=jax.ShapeDtypeStruct((batch * seq, qw), BF16),
        scratch_shapes=[pltpu.VMEM((WINDOW + seq, SWA_KV_HEADS * SWA_HEAD_DIM), BF16),
                        pltpu.VMEM((1 + seq // WINDOW, SWA_KV_HEADS, SWA_HEAD_DIM + SWA_ONES_ROWS, WINDOW), BF16)],
        compiler_params=_params(1),
        name="swa",
    )(sinks, qa, kva, bias)


def _mla_kernel(q_ref, k_ref, v_ref, o_ref, vt_ref, *, tq):
    heads, seq, dv = v_ref.shape[1], v_ref.shape[2], v_ref.shape[3]
    n_tiles = seq // tq
    key = lax.broadcasted_iota(jnp.int32, (tq, tq), 0)
    qry = lax.broadcasted_iota(jnp.int32, (tq, tq), 1)
    causal = key <= qry
    for hd in range(heads):
        for c in range(seq // WINDOW):
            v_blk = v_ref[0, hd, c * WINDOW:(c + 1) * WINDOW, :]
            vt_ref[hd, 0:dv, c * WINDOW:(c + 1) * WINDOW] = v_blk.astype(F32).T.astype(BF16)
        vt_ref[hd, dv:, :] = jnp.ones((MLA_ONES_ROWS, seq), BF16)
    sts, mxs, pts, accs = {}, {}, {}, {}

    def scores(hd, i, j):
        q = q_ref[0, hd, i * tq:(i + 1) * tq, :]
        s = lax.dot_general(k_ref[0, hd, j * tq:(j + 1) * tq, :], q, NT_DIMS, preferred_element_type=F32)
        sts[hd, i, j] = jnp.where(causal, s, MASK_VALUE) if j == i else s

    def col_max(hd, i, j):
        m = jnp.max(sts[hd, i, j], axis=0, keepdims=True)
        mxs[hd, i] = m if j == 0 else jnp.maximum(mxs[hd, i], m)

    def probs(hd, i, j):
        pts[hd, i, j] = jnp.exp2(sts.pop((hd, i, j)) - mxs[hd, i]).astype(BF16)

    def values(hd, i, j):
        part = _dot(vt_ref[hd, :, j * tq:(j + 1) * tq], pts.pop((hd, i, j)))
        accs[hd, i] = part if j == 0 else accs[hd, i] + part
        if j == i:
            ot = accs.pop((hd, i))
            o = ot[0:dv, :] * (1.0 / ot[dv:dv + 1, :])
            o_ref[i * tq:(i + 1) * tq, hd * dv:(hd + 1) * dv] = o.T.astype(BF16)

    stages = (scores, col_max, probs, values)
    units = [(hd, i) for hd in range(heads) for i in range(n_tiles)]
    for t in range(len(units) + len(stages) - 1):
        for j in range(n_tiles):
            for lag, stage in enumerate(stages):
                if 0 <= t - lag < len(units):
                    hd, i = units[t - lag]
                    if j <= i:
                        stage(hd, i, j)


def _mla(qm, km, vm, *, tq=256):
    batch, heads, seq, qk = qm.shape
    dv = vm.shape[-1]
    return pl.pallas_call(
        functools.partial(_mla_kernel, tq=tq),
        grid=(batch,),
        in_specs=[pl.BlockSpec((1, heads, seq, qk), lambda b: (b, 0, 0, 0)),
                  pl.BlockSpec((1, heads, seq, qk), lambda b: (b, 0, 0, 0)),
                  pl.BlockSpec((1, heads, seq, dv), lambda b: (b, 0, 0, 0))],
        out_specs=pl.BlockSpec((seq, heads * dv), lambda b: (b, 0)),
        out_shape=jax.ShapeDtypeStruct((batch * seq, heads * dv), BF16),
        scratch_shapes=[pltpu.VMEM((heads, dv + MLA_ONES_ROWS, seq), BF16)],
        compiler_params=_params(1),
        name="mla",
    )(qm, km, vm)


def kernel(x, c, w_mod, b_mod, norm_ffn1, ffn1_gate, ffn1_up, ffn1_down, norm_mix, w_in, q_norm, kv_norm, w_uq, w_ukv, sinks, w_o, norm_ffn2, ffn2_gate, ffn2_up, ffn2_down, rel_bias, norm_final):
    batch, seq, d = x.shape
    depth = w_mod.shape[0]
    x2d = x.reshape(batch * seq, d)
    bias = _swa_bias(rel_bias)
    for l in range(depth):
        mod3 = _mod(c, w_mod[l], b_mod[l])
        x2d = _ffn(x2d, mod3, norm_ffn1[l], ffn1_gate[l], ffn1_up[l], ffn1_down[l], mod_row=0, seq=seq)
        qa, kva, qm, km, vm = _mix_proj(x2d, mod3, norm_mix[l], w_in[l], q_norm[l], kv_norm[l],
                                        w_uq[l], w_ukv[l], batch=batch, seq=seq)
        out_a = _swa(qa, kva, bias, sinks[l], batch=batch, seq=seq)
        out_b = _mla(qm, km, vm)
        last = l == depth - 1
        x2d = _ffn(x2d, mod3, norm_ffn2[l], ffn2_gate[l], ffn2_up[l], ffn2_down[l], mod_row=6, seq=seq,
                   mix=(out_a, out_b, w_o[l].astype(BF16)),
                   final_gain=norm_final if last else None)
    return x2d.reshape(batch, seq, d)
```

```python
import functools
import math

import jax
import jax.numpy as jnp
import numpy as np
from jax import lax
from jax.experimental import pallas as pl
from jax.experimental.pallas import tpu as pltpu

EPS = 1e-6
FFN_RES_WEIGHT = 0.5
N_MOD = 9

SWA_HEADS = 8
SWA_KV_HEADS = 2
SWA_HEAD_DIM = 64
SWA_GROUP = SWA_HEADS // SWA_KV_HEADS
WINDOW = 128

MLA_HEADS = 4
MLA_Q_RANK = 256
MLA_KV_RANK = 128
MLA_NOPE = 128
MLA_ROPE = 64
MLA_V = 128
MLA_QK = MLA_NOPE + MLA_ROPE
ROPE_THETA = 10000.0

NUM_BUCKETS = 32
MAX_DISTANCE = 128

SWA_ONES_ROWS = 16
MLA_ONES_ROWS = 16
LOG2E = math.log2(math.e)
MASK_VALUE = -1e30
V7X_VMEM_LIMIT_BYTES = 56 * 1024 * 1024

F32 = jnp.float32
BF16 = jnp.bfloat16
NT_DIMS = (((1,), (1,)), ((), ()))


def _params(n_grid_dims):
    return pltpu.CompilerParams(dimension_semantics=("arbitrary",) * n_grid_dims,
                                vmem_limit_bytes=V7X_VMEM_LIMIT_BYTES)


def _resident(shape):
    return pl.BlockSpec(shape, lambda *_: (0,) * len(shape), pipeline_mode=pl.Buffered(1))


def _rms_norm(x, gain):
    ms = jnp.mean(x * x, axis=-1, keepdims=True)
    return x * lax.rsqrt(ms + EPS) * gain


def _silu(x):
    return x * (1.0 / (1.0 + jnp.exp(-x)))


def _dot(a, b):
    return jnp.dot(a, b, preferred_element_type=F32)


def _mod_kernel(c_ref, w_ref, b_ref, o_ref):
    c_act = _silu(c_ref[...]).astype(BF16)
    res = _dot(c_act, w_ref[...].astype(BF16)) + b_ref[...]
    d = o_ref.shape[2]
    for r in range(o_ref.shape[0]):
        o_ref[r] = res[:, r * d:(r + 1) * d]


def _mod(c, w_mod, b_mod, *, rows_per_step=3):
    batch, d = c.shape
    n = w_mod.shape[1]
    tn = rows_per_step * d
    return pl.pallas_call(
        _mod_kernel,
        grid=(n // tn,),
        in_specs=[pl.BlockSpec((batch, d), lambda j: (0, 0)),
                  pl.BlockSpec((d, tn), lambda j: (0, j)),
                  pl.BlockSpec((1, tn), lambda j: (0, j))],
        out_specs=pl.BlockSpec((rows_per_step, batch, d), lambda j: (j, 0, 0)),
        out_shape=jax.ShapeDtypeStruct((n // d, batch, d), F32),
        compiler_params=_params(1),
        name="mod",
    )(c, w_mod, b_mod.reshape(1, n))


def _mod_row(mod_ref, row, b):
    return mod_ref[row, pl.ds(b, 1), :]


def _ffn_kernel(*refs, mod_row, tiles_per_seq, mix_residual, final_norm):
    refs = list(refs)
    x_ref, mod_ref = refs[:2]
    del refs[:2]
    if mix_residual:
        oa_ref, ob_ref, wo_ref = refs[:3]
        del refs[:3]
    g_ref, wg_hbm, wu_hbm, wd_hbm = refs[:4]
    del refs[:4]
    if final_norm:
        gf_ref = refs.pop(0)
    (o_ref, wg_s, wu_s, wd_s, h_even, h_odd, xres_s, a_s,
     stage_gu, stage_d, sem_gu, sem_d) = refs
    n_w, _, fw = wg_s.shape
    tm = h_even.shape[0]
    n_sub = x_ref.shape[0] // tm
    t = pl.program_id(0)
    gu_slots, d_slots, d_rows = stage_gu.shape[0], stage_d.shape[0], stage_d.shape[1]
    n_d = wd_hbm.shape[0] // d_rows

    def gu_copies(c):
        cols = pl.ds(c * fw, fw)
        return [pltpu.make_async_copy(src.at[:, cols], stage_gu.at[c % gu_slots, k], sem_gu.at[c % gu_slots, k])
                for k, src in enumerate((wg_hbm, wu_hbm))]

    def d_copy(m):
        return pltpu.make_async_copy(wd_hbm.at[pl.ds(m * d_rows, d_rows), :], stage_d.at[m % d_slots],
                                     sem_d.at[m % d_slots])

    def stream_begin():
        for c in range(gu_slots):
            for cp in gu_copies(c):
                cp.start()
        for m in range(d_slots):
            d_copy(m).start()

    def stream_gate_up(c):
        for cp in gu_copies(c):
            cp.wait()
        wg_s[c] = stage_gu[c % gu_slots, 0].astype(BF16)
        wu_s[c] = stage_gu[c % gu_slots, 1].astype(BF16)
        if c + gu_slots < n_w:
            for cp in gu_copies(c + gu_slots):
                cp.start()

    def stream_down(m):
        d_copy(m).wait()
        wd_s[m * d_rows:(m + 1) * d_rows, :] = stage_d[m % d_slots].astype(BF16)
        if m + d_slots < n_d:
            d_copy(m + d_slots).start()

    def batch_of(sub):
        return (t * n_sub + sub) // tiles_per_seq

    def norm_stage(sub, h_next):
        rows = pl.ds(sub * tm, tm)
        b = batch_of(sub)
        x = x_ref[rows, :]
        if mix_residual:
            wa = oa_ref.shape[1]
            mix = _dot(oa_ref[rows, :], wo_ref[0:wa, :]) + _dot(ob_ref[rows, :], wo_ref[wa:, :])
            x = x + _mod_row(mod_ref, mod_row - 1, b) * mix
            xres_s[rows, :] = x
        gain = g_ref[...] * (1.0 + _mod_row(mod_ref, mod_row + 1, b))
        ms = jnp.mean(x * x, axis=-1, keepdims=True)
        h_next[...] = (x * lax.rsqrt(ms + EPS) * gain + _mod_row(mod_ref, mod_row, b)).astype(BF16)

    def matmul_stage(sub, h_cur, stream_weights):
        rows = pl.ds(sub * tm, tm)
        if stream_weights:
            stream_begin()
        for c in range(n_w):
            if stream_weights:
                stream_gate_up(c)
            g = _dot(h_cur[...], wg_s[c])
            u = _dot(h_cur[...], wu_s[c])
            a_s[:, c * fw:(c + 1) * fw] = (_silu(g) * u).astype(BF16)
            if stream_weights and c < n_d:
                stream_down(c)
        if stream_weights:
            for m in range(n_w, n_d):
                stream_down(m)
        y = _dot(a_s[...], wd_s[...])
        x = xres_s[rows, :] if mix_residual else x_ref[rows, :]
        out = x + FFN_RES_WEIGHT * _mod_row(mod_ref, mod_row + 2, batch_of(sub)) * y
        if final_norm:
            out = _rms_norm(out, gf_ref[...])
        o_ref[rows, :] = out

    def block(first):
        bufs = (h_even, h_odd)
        norm_stage(0, bufs[0])
        for sub in range(n_sub):
            if sub + 1 < n_sub:
                norm_stage(sub + 1, bufs[(sub + 1) % 2])
            matmul_stage(sub, bufs[sub % 2], stream_weights=first and sub == 0)

    @pl.when(t == 0)
    def _():
        block(first=True)

    @pl.when(t > 0)
    def _():
        block(first=False)


def _ffn(x2d, mod3, gain, wg, wu, wd, *, mod_row, seq, mix=None, final_gain=None, tm=512, tiles_per_step=2, fw=256,
         down_chunks=11, gu_slots=2, d_slots=1):
    m, d = x2d.shape
    f = wg.shape[1]
    bm = tm * tiles_per_step
    assert f % fw == 0 and seq % tm == 0 and m % bm == 0 and f % down_chunks == 0
    n_w, n_blocks, tiles_per_seq = f // fw, m // bm, seq // tm
    final_norm = final_gain is not None

    def block(s):
        return (s, 0)

    in_specs = [pl.BlockSpec((bm, d), block), _resident(mod3.shape)]
    args = [x2d, mod3]
    if mix is not None:
        out_a, out_b, w_o = mix
        wa, wb = out_a.shape[1], out_b.shape[1]
        in_specs += [pl.BlockSpec((bm, wa), block), pl.BlockSpec((bm, wb), block), _resident((wa + wb, d))]
        args += [out_a, out_b, w_o]
    hbm = pl.BlockSpec(memory_space=pl.ANY)
    in_specs += [_resident((1, d)), hbm, hbm, hbm]
    args += [gain.reshape(1, d), wg, wu, wd]
    if final_norm:
        in_specs.append(_resident((1, d)))
        args.append(final_gain.reshape(1, d))
    xres_rows = bm if mix is not None else 8
    return pl.pallas_call(
        functools.partial(_ffn_kernel, mod_row=mod_row, tiles_per_seq=tiles_per_seq,
                          mix_residual=mix is not None, final_norm=final_norm),
        grid=(n_blocks,),
        in_specs=in_specs,
        out_specs=pl.BlockSpec((bm, d), block),
        out_shape=jax.ShapeDtypeStruct((m, d), F32),
        scratch_shapes=[pltpu.VMEM((n_w, d, fw), BF16), pltpu.VMEM((n_w, d, fw), BF16), pltpu.VMEM((f, d), BF16),
                        pltpu.VMEM((tm, d), BF16), pltpu.VMEM((tm, d), BF16),
                        pltpu.VMEM((xres_rows, d), F32), pltpu.VMEM((tm, f), BF16),
                        pltpu.VMEM((gu_slots, 2, d, fw), F32),
                        pltpu.VMEM((d_slots, f // down_chunks, d), F32),
                        pltpu.SemaphoreType.DMA((gu_slots, 2)), pltpu.SemaphoreType.DMA((d_slots,))],
        compiler_params=_params(1),
        name="ffn_final" if final_norm else "ffn",
    )(*args)


_QA0, _KVA0, _QLAT0, _KVLAT0, _KR0, _KRS0, _PROJ_W = 0, 512, 768, 1024, 1152, 1216, 1280
_UQ_ROPE0 = MLA_HEADS * MLA_NOPE
_UQ_ROPES0 = _UQ_ROPE0 + MLA_HEADS * MLA_ROPE


def _mix_proj_kernel(x_ref, mod_ref, g_ref, win_ref, qn_ref, kvn_ref, wuq_ref, wukv_ref, cos_ref, sin_ref,
                     qa_ref, kva_ref, qm_ref, km_ref, vm_ref, win_s, *, blocks_per_seq, sub_rows):
    @pl.when(pl.program_id(0) == 0)
    def _():
        w = win_ref[...]
        half = MLA_ROPE // 2
        win_s[:, 0:_KRS0] = w.astype(BF16)
        win_s[:, _KRS0:_KRS0 + half] = w[:, _KR0 + half:_KRS0].astype(BF16)
        win_s[:, _KRS0 + half:_PROJ_W] = w[:, _KR0:_KR0 + half].astype(BF16)

    b = pl.program_id(0) // blocks_per_seq
    shift = _mod_row(mod_ref, 3, b)
    scale = _mod_row(mod_ref, 4, b)
    q_scale = MLA_QK ** -0.5 * LOG2E
    for r0 in range(0, x_ref.shape[0], sub_rows):
        rows = slice(r0, r0 + sub_rows)
        h = (_rms_norm(x_ref[rows, :], g_ref[...]) * (1.0 + scale) + shift).astype(BF16)
        proj = _dot(h, win_s[...])
        for hd in range(SWA_HEADS):
            qa_ref[0, hd, rows, :] = (proj[:, _QA0 + hd * SWA_HEAD_DIM:_QA0 + (hd + 1) * SWA_HEAD_DIM]
                                      * (SWA_HEAD_DIM ** -0.5 * LOG2E)).astype(BF16)
        kva_ref[rows, :] = proj[:, _KVA0:_QLAT0].astype(BF16)

        q_lat = _rms_norm(proj[:, _QLAT0:_KVLAT0], qn_ref[...]).astype(BF16)
        kv_lat = _rms_norm(proj[:, _KVLAT0:_KR0], kvn_ref[...]).astype(BF16)
        q_all = _dot(q_lat, wuq_ref[...])
        kv_all = _dot(kv_lat, wukv_ref[...])

        cos = cos_ref[rows, :]
        sin = sin_ref[rows, :]
        k_rope = (proj[:, _KR0:_KRS0] * cos + proj[:, _KRS0:_PROJ_W] * sin).astype(BF16)
        for hd in range(MLA_HEADS):
            q_nope = q_all[:, hd * MLA_NOPE:(hd + 1) * MLA_NOPE]
            q_rope = (q_all[:, _UQ_ROPE0 + hd * MLA_ROPE:_UQ_ROPE0 + (hd + 1) * MLA_ROPE] * cos
                      + q_all[:, _UQ_ROPES0 + hd * MLA_ROPE:_UQ_ROPES0 + (hd + 1) * MLA_ROPE] * sin)
            qm_ref[0, hd, rows, 0:MLA_NOPE] = (q_nope * q_scale).astype(BF16)
            qm_ref[0, hd, rows, MLA_NOPE:MLA_QK] = (q_rope * q_scale).astype(BF16)
            kv0 = hd * (MLA_NOPE + MLA_V)
            km_ref[0, hd, rows, 0:MLA_NOPE] = kv_all[:, kv0:kv0 + MLA_NOPE].astype(BF16)
            km_ref[0, hd, rows, MLA_NOPE:MLA_QK] = k_rope
            vm_ref[0, hd, rows, :] = kv_all[:, kv0 + MLA_NOPE:kv0 + MLA_NOPE + MLA_V].astype(BF16)


def _rope_tables(seq):
    inv = np.float32(ROPE_THETA) ** (-np.arange(0, MLA_ROPE, 2, dtype=np.float32) / np.float32(MLA_ROPE))
    ang = np.arange(seq, dtype=np.float32)[:, None] * inv[None, :].astype(np.float32)
    cos, sin = np.cos(ang).astype(np.float32), np.sin(ang).astype(np.float32)
    return np.concatenate([cos, cos], axis=-1), np.concatenate([-sin, sin], axis=-1)


def _swap_halves(w):
    half = w.shape[-1] // 2
    return jnp.concatenate([w[..., half:], w[..., :half]], axis=-1)


def _mix_proj(x2d, mod3, gain, w_in, q_norm, kv_norm, w_uq, w_ukv, *, batch, seq, tm=1024, sub_rows=512):
    m, d = x2d.shape
    tiles_per_seq = seq // tm
    assert seq % tm == 0 and tm % sub_rows == 0
    w_uq_h = w_uq.reshape(MLA_Q_RANK, MLA_HEADS, MLA_QK)
    uq_rope = w_uq_h[:, :, MLA_NOPE:]
    w_uq_r = jnp.concatenate([w_uq_h[:, :, :MLA_NOPE].reshape(MLA_Q_RANK, -1),
                              uq_rope.reshape(MLA_Q_RANK, -1),
                              _swap_halves(uq_rope).reshape(MLA_Q_RANK, -1)], axis=1).astype(BF16)
    w_ukv_b = w_ukv.astype(BF16)
    cos2, sin2 = (jnp.asarray(tab) for tab in _rope_tables(seq))

    def head_spec(width):
        return pl.BlockSpec((1, MLA_HEADS, tm, width),
                            lambda i: (i // tiles_per_seq, 0, i % tiles_per_seq, 0))

    def head_shape(width):
        return jax.ShapeDtypeStruct((batch, MLA_HEADS, seq, width), BF16)

    return pl.pallas_call(
        functools.partial(_mix_proj_kernel, blocks_per_seq=tiles_per_seq, sub_rows=sub_rows),
        grid=(m // tm,),
        in_specs=[pl.BlockSpec((tm, d), lambda i: (i, 0)),
                  _resident(mod3.shape),
                  _resident((1, d)),
                  _resident(w_in.shape),
                  _resident((1, MLA_Q_RANK)), _resident((1, MLA_KV_RANK)),
                  _resident(w_uq_r.shape), _resident(w_ukv_b.shape),
                  pl.BlockSpec((tm, MLA_ROPE), lambda i: (i % tiles_per_seq, 0)),
                  pl.BlockSpec((tm, MLA_ROPE), lambda i: (i % tiles_per_seq, 0))],
        out_specs=[pl.BlockSpec((1, SWA_HEADS, tm, SWA_HEAD_DIM),
                                lambda i: (i // tiles_per_seq, 0, i % tiles_per_seq, 0)),
                   pl.BlockSpec((tm, 2 * SWA_KV_HEADS * SWA_HEAD_DIM), lambda i: (i, 0)),
                   head_spec(MLA_QK), head_spec(MLA_QK), head_spec(MLA_V)],
        out_shape=[jax.ShapeDtypeStruct((batch, SWA_HEADS, seq, SWA_HEAD_DIM), BF16),
                   jax.ShapeDtypeStruct((m, 2 * SWA_KV_HEADS * SWA_HEAD_DIM), BF16),
                   head_shape(MLA_QK), head_shape(MLA_QK), head_shape(MLA_V)],
        scratch_shapes=[pltpu.VMEM((d, _PROJ_W), BF16)],
        compiler_params=_params(1),
        name="mix_proj",
    )(x2d, mod3, gain.reshape(1, d), w_in, q_norm.reshape(1, -1), kv_norm.reshape(1, -1),
      w_uq_r, w_ukv_b, cos2, sin2)


def _t5_bucket_table():
    qi = np.arange(WINDOW)[:, None]
    kj = np.arange(2 * WINDOW)[None, :]
    dist = qi + WINDOW - kj
    max_exact = NUM_BUCKETS // 2
    n = np.maximum(dist, 0)
    nf = np.maximum(n, 1).astype(np.float32)
    large = max_exact + (np.log(nf / np.float32(max_exact)) / np.float32(math.log(MAX_DISTANCE / max_exact))
                         * np.float32(NUM_BUCKETS - max_exact)).astype(np.int32)
    large = np.minimum(large, NUM_BUCKETS - 1)
    bucket = np.where(n < max_exact, n, large)
    band = (dist >= 0) & (dist < WINDOW)
    return np.where(band, bucket, -1).astype(np.int32)


def _swa_bias_kernel(rel_ref, bucket_ref, o_ref):
    bucket = bucket_ref[...]
    first_block_ok = lax.broadcasted_iota(jnp.int32, bucket.shape, 0) >= WINDOW
    for hd in range(SWA_HEADS):
        acc = jnp.full(bucket.shape, MASK_VALUE, F32)
        for b in range(NUM_BUCKETS):
            acc = jnp.where(bucket == b, rel_ref[b, hd] * LOG2E, acc)
        g, j = divmod(hd, SWA_GROUP)
        o_ref[0, g, :, j * WINDOW:(j + 1) * WINDOW] = jnp.where(first_block_ok, acc, MASK_VALUE)
        o_ref[1, g, :, j * WINDOW:(j + 1) * WINDOW] = acc


def _swa_bias(rel_bias):
    shape = (2, SWA_KV_HEADS, 2 * WINDOW, SWA_GROUP * WINDOW)
    return pl.pallas_call(
        _swa_bias_kernel,
        in_specs=[pl.BlockSpec(memory_space=pltpu.SMEM),
                  pl.BlockSpec((2 * WINDOW, WINDOW), lambda: (0, 0))],
        out_specs=pl.BlockSpec(shape, lambda: (0, 0, 0, 0)),
        out_shape=jax.ShapeDtypeStruct(shape, F32),
        name="swa_bias",
    )(rel_bias, jnp.asarray(_t5_bucket_table().T))


def _swa_kernel(sink_ref, q_ref, kv_ref, bias_ref, o_ref, kp_ref, vt_ref, *, blocks_per_iter):
    seq = kv_ref.shape[0]
    n_blocks = seq // WINDOW
    kdim = SWA_KV_HEADS * SWA_HEAD_DIM
    kp_ref[0:WINDOW, :] = jnp.zeros((WINDOW, kdim), BF16)
    kp_ref[WINDOW:, :] = kv_ref[:, 0:kdim]
    ones_rows = jnp.ones((SWA_ONES_ROWS, WINDOW), BF16)
    for g in range(SWA_KV_HEADS):
        vt_ref[0, g] = jnp.concatenate([jnp.zeros((SWA_HEAD_DIM, WINDOW), BF16), ones_rows], axis=0)
    for n in range(n_blocks):
        for g in range(SWA_KV_HEADS):
            v_blk = kv_ref[n * WINDOW:(n + 1) * WINDOW, kdim + g * SWA_HEAD_DIM:kdim + (g + 1) * SWA_HEAD_DIM]
            vt_ref[n + 1, g] = jnp.concatenate([v_blk.astype(F32).T.astype(BF16), ones_rows], axis=0)
    sink_rows = [[jnp.full((1, WINDOW), sink_ref[g * SWA_GROUP + j] * LOG2E, F32) for j in range(SWA_GROUP)]
                 for g in range(SWA_KV_HEADS)]

    def body(it, carry):
        units = [(b, g) for b in range(blocks_per_iter) for g in range(SWA_KV_HEADS)]
        blocks = [it * blocks_per_iter + b for b in range(blocks_per_iter)]
        q_starts = [pl.multiple_of(n * WINDOW, WINDOW) for n in blocks]
        tables = [jnp.minimum(n, 1) for n in blocks]
        sts, pts = {}, {}

        def scores(u):
            b, g = units[u]
            k = kp_ref[pl.ds(q_starts[b], 2 * WINDOW), g * SWA_HEAD_DIM:(g + 1) * SWA_HEAD_DIM]
            sts[u] = []
            for j in range(0, SWA_GROUP, 2):
                hd = g * SWA_GROUP + j
                q = jnp.concatenate([q_ref[0, hd, pl.ds(q_starts[b], WINDOW), :],
                                     q_ref[0, hd + 1, pl.ds(q_starts[b], WINDOW), :]], axis=0)
                st = lax.dot_general(k, q, NT_DIMS, preferred_element_type=F32)
                sts[u] += [st[:, 0:WINDOW], st[:, WINDOW:2 * WINDOW]]

        def softmax(u):
            b, g = units[u]
            pts[u] = []
            for j in range(SWA_GROUP):
                st = sts[u][j] + bias_ref[tables[b], g, :, j * WINDOW:(j + 1) * WINDOW]
                mx = jnp.maximum(jnp.max(st, axis=0, keepdims=True), sink_rows[g][j])
                pts[u].append((jnp.exp2(st - mx).astype(BF16), jnp.exp2(sink_rows[g][j] - mx)))

        def values(u):
            b, g = units[u]
            vt = jnp.concatenate([vt_ref[blocks[b], g], vt_ref[blocks[b] + 1, g]], axis=1)
            for j in range(0, SWA_GROUP, 2):
                pt = jnp.concatenate([pts[u][j][0], pts[u][j + 1][0]], axis=1)
                ot = _dot(vt, pt)
                halves = []
                for jj in range(2):
                    lanes = slice(jj * WINDOW, (jj + 1) * WINDOW)
                    denom = ot[SWA_HEAD_DIM:SWA_HEAD_DIM + 1, lanes] + pts[u][j + jj][1]
                    halves.append(ot[0:SWA_HEAD_DIM, lanes] * (1.0 / denom))
                pair = jnp.concatenate(halves, axis=0).T
                lane0 = (g * SWA_GROUP + j) * SWA_HEAD_DIM
                o_ref[pl.ds(q_starts[b], WINDOW), lane0:lane0 + 2 * SWA_HEAD_DIM] = pair.astype(BF16)

        for t in range(len(units) + 2):
            if t < len(units):
                scores(t)
            if 0 <= t - 1 < len(units):
                softmax(t - 1)
            if 0 <= t - 2 < len(units):
                values(t - 2)
        return carry

    lax.fori_loop(0, n_blocks // blocks_per_iter, body, 0)


def _swa(qa, kva, bias, sinks, *, batch, seq, blocks_per_iter=16):
    kvw = kva.shape[1]
    qw = SWA_HEADS * SWA_HEAD_DIM
    assert (seq // WINDOW) % blocks_per_iter == 0
    return pl.pallas_call(
        functools.partial(_swa_kernel, blocks_per_iter=blocks_per_iter),
        grid=(batch,),
        in_specs=[pl.BlockSpec(memory_space=pltpu.SMEM),
                  pl.BlockSpec((1, SWA_HEADS, seq, SWA_HEAD_DIM), lambda b: (b, 0, 0, 0)),
                  pl.BlockSpec((seq, kvw), lambda b: (b, 0)),
                  _resident(bias.shape)],
        out_specs=pl.BlockSpec((seq, qw), lambda b: (b, 0)),
        out_shape=jax.ShapeDtypeStruct((batch * seq, qw), BF16),
        scratch_shapes=[pltpu.VMEM((WINDOW + seq, SWA_KV_HEADS * SWA_HEAD_DIM), BF16),
                        pltpu.VMEM((1 + seq // WINDOW, SWA_KV_HEADS, SWA_HEAD_DIM + SWA_ONES_ROWS, WINDOW), BF16)],
        compiler_params=_params(1),
        name="swa",
    )(sinks, qa, kva, bias)


def _mla_kernel(q_ref, k_ref, v_ref, o_ref, vt_ref, *, tq):
    heads, seq, dv = v_ref.shape[1], v_ref.shape[2], v_ref.shape[3]
    n_tiles = seq // tq
    key = lax.broadcasted_iota(jnp.int32, (tq, tq), 0)
    qry = lax.broadcasted_iota(jnp.int32, (tq, tq), 1)
    causal = key <= qry
    for hd in range(heads):
        for c in range(seq // WINDOW):
            v_blk = v_ref[0, hd, c * WINDOW:(c + 1) * WINDOW, :]
            vt_ref[hd, 0:dv, c * WINDOW:(c + 1) * WINDOW] = v_blk.astype(F32).T.astype(BF16)
        vt_ref[hd, dv:, :] = jnp.ones((MLA_ONES_ROWS, seq), BF16)
    sts, mxs, pts, accs = {}, {}, {}, {}

    def scores(hd, i, j):
        q = q_ref[0, hd, i * tq:(i + 1) * tq, :]
        s = lax.dot_general(k_ref[0, hd, j * tq:(j + 1) * tq, :], q, NT_DIMS, preferred_element_type=F32)
        sts[hd, i, j] = jnp.where(causal, s, MASK_VALUE) if j == i else s

    def col_max(hd, i, j):
        m = jnp.max(sts[hd, i, j], axis=0, keepdims=True)
        mxs[hd, i] = m if j == 0 else jnp.maximum(mxs[hd, i], m)

    def probs(hd, i, j):
        pts[hd, i, j] = jnp.exp2(sts.pop((hd, i, j)) - mxs[hd, i]).astype(BF16)

    def values(hd, i, j):
        part = _dot(vt_ref[hd, :, j * tq:(j + 1) * tq], pts.pop((hd, i, j)))
        accs[hd, i] = part if j == 0 else accs[hd, i] + part
        if j == i:
            ot = accs.pop((hd, i))
            o = ot[0:dv, :] * (1.0 / ot[dv:dv + 1, :])
            o_ref[i * tq:(i + 1) * tq, hd * dv:(hd + 1) * dv] = o.T.astype(BF16)

    stages = (scores, col_max, probs, values)
    units = [(hd, i) for hd in range(heads) for i in range(n_tiles)]
    for t in range(len(units) + len(stages) - 1):
        for j in range(n_tiles):
            for lag, stage in enumerate(stages):
                if 0 <= t - lag < len(units):
                    hd, i = units[t - lag]
                    if j <= i:
                        stage(hd, i, j)


def _mla(qm, km, vm, *, tq=256):
    batch, heads, seq, qk = qm.shape
    dv = vm.shape[-1]
    return pl.pallas_call(
        functools.partial(_mla_kernel, tq=tq),
        grid=(batch,),
        in_specs=[pl.BlockSpec((1, heads, seq, qk), lambda b: (b, 0, 0, 0)),
                  pl.BlockSpec((1, heads, seq, qk), lambda b: (b, 0, 0, 0)),
                  pl.BlockSpec((1, heads, seq, dv), lambda b: (b, 0, 0, 0))],
        out_specs=pl.BlockSpec((seq, heads * dv), lambda b: (b, 0)),
        out_shape=jax.ShapeDtypeStruct((batch * seq, heads * dv), BF16),
        scratch_shapes=[pltpu.VMEM((heads, dv + MLA_ONES_ROWS, seq), BF16)],
        compiler_params=_params(1),
        name="mla",
    )(qm, km, vm)


def kernel(x, c, w_mod, b_mod, norm_ffn1, ffn1_gate, ffn1_up, ffn1_down, norm_mix, w_in, q_norm, kv_norm, w_uq, w_ukv, sinks, w_o, norm_ffn2, ffn2_gate, ffn2_up, ffn2_down, rel_bias, norm_final):
    batch, seq, d = x.shape
    depth = w_mod.shape[0]
    x2d = x.reshape(batch * seq, d)
    bias = _swa_bias(rel_bias)
    for l in range(depth):
        mod3 = _mod(c, w_mod[l], b_mod[l])
        x2d = _ffn(x2d, mod3, norm_ffn1[l], ffn1_gate[l], ffn1_up[l], ffn1_down[l], mod_row=0, seq=seq)
        qa, kva, qm, km, vm = _mix_proj(x2d, mod3, norm_mix[l], w_in[l], q_norm[l], kv_norm[l],
                                        w_uq[l], w_ukv[l], batch=batch, seq=seq)
        out_a = _swa(qa, kva, bias, sinks[l], batch=batch, seq=seq)
        out_b = _mla(qm, km, vm)
        last = l == depth - 1
        x2d = _ffn(x2d, mod3, norm_ffn2[l], ffn2_gate[l], ffn2_up[l], ffn2_down[l], mod_row=6, seq=seq,
                   mix=(out_a, out_b, w_o[l].astype(BF16)),
                   final_gain=norm_final if last else None)
    return x2d.reshape(batch, seq, d)
```

```python
import functools
import math

import jax
import jax.numpy as jnp
import numpy as np
from jax import lax
from jax.experimental import pallas as pl
from jax.experimental.pallas import tpu as pltpu

EPS = 1e-6
FFN_RES_WEIGHT = 0.5
N_MOD = 9

SWA_HEADS = 8
SWA_KV_HEADS = 2
SWA_HEAD_DIM = 64
SWA_GROUP = SWA_HEADS // SWA_KV_HEADS
WINDOW = 128

MLA_HEADS = 4
MLA_Q_RANK = 256
MLA_KV_RANK = 128
MLA_NOPE = 128
MLA_ROPE = 64
MLA_V = 128
MLA_QK = MLA_NOPE + MLA_ROPE
ROPE_THETA = 10000.0

NUM_BUCKETS = 32
MAX_DISTANCE = 128

SWA_ONES_ROWS = 16
MLA_ONES_ROWS = 16
LOG2E = math.log2(math.e)
MASK_VALUE = -1e30
V7X_VMEM_LIMIT_BYTES = 56 * 1024 * 1024

F32 = jnp.float32
BF16 = jnp.bfloat16
NT_DIMS = (((1,), (1,)), ((), ()))


def _params(n_grid_dims):
    return pltpu.CompilerParams(dimension_semantics=("arbitrary",) * n_grid_dims,
                                vmem_limit_bytes=V7X_VMEM_LIMIT_BYTES)


def _resident(shape):
    return pl.BlockSpec(shape, lambda *_: (0,) * len(shape), pipeline_mode=pl.Buffered(1))


def _rms_norm(x, gain):
    ms = jnp.mean(x * x, axis=-1, keepdims=True)
    return x * lax.rsqrt(ms + EPS) * gain


def _silu(x):
    return x * (1.0 / (1.0 + jnp.exp(-x)))


def _dot(a, b):
    return jnp.dot(a, b, preferred_element_type=F32)


def _mod_kernel(c_ref, w_ref, b_ref, o_ref):
    c_act = _silu(c_ref[...]).astype(BF16)
    res = _dot(c_act, w_ref[...].astype(BF16)) + b_ref[...]
    d = o_ref.shape[2]
    for r in range(o_ref.shape[0]):
        o_ref[r] = res[:, r * d:(r + 1) * d]


def _mod(c, w_mod, b_mod, *, rows_per_step=3):
    batch, d = c.shape
    n = w_mod.shape[1]
    tn = rows_per_step * d
    return pl.pallas_call(
        _mod_kernel,
        grid=(n // tn,),
        in_specs=[pl.BlockSpec((batch, d), lambda j: (0, 0)),
                  pl.BlockSpec((d, tn), lambda j: (0, j)),
                  pl.BlockSpec((1, tn), lambda j: (0, j))],
        out_specs=pl.BlockSpec((rows_per_step, batch, d), lambda j: (j, 0, 0)),
        out_shape=jax.ShapeDtypeStruct((n // d, batch, d), F32),
        compiler_params=_params(1),
        name="mod",
    )(c, w_mod, b_mod.reshape(1, n))


def _mod_row(mod_ref, row, b):
    return mod_ref[row, pl.ds(b, 1), :]


def _ffn_kernel(*refs, mod_row, tiles_per_seq, mix_residual, final_norm, stream_first_tile):
    refs = list(refs)
    x_ref, mod_ref = refs[:2]
    del refs[:2]
    if mix_residual:
        oa_ref, ob_ref, wo_ref = refs[:3]
        del refs[:3]
    g_ref, wg_hbm, wu_hbm, wd_hbm = refs[:4]
    del refs[:4]
    if final_norm:
        gf_ref = refs.pop(0)
    (o_ref, wg_s, wu_s, wd_s, h_even, h_odd, xres_s, a_s,
     stage_gu, stage_d, sem_gu, sem_d) = refs
    n_w, _, fw = wg_s.shape
    tm = h_even.shape[0]
    n_sub = x_ref.shape[0] // tm
    lead_steps = 0 if stream_first_tile else 1
    t = pl.program_id(0) - lead_steps
    gu_slots, d_slots, d_rows = stage_gu.shape[0], stage_d.shape[0], stage_d.shape[1]
    n_d = wd_hbm.shape[0] // d_rows

    def weight_stage():
        gu_rows = stage_gu.shape[1]
        gu_chunks = [(src, dst, k) for k in range(wg_hbm.shape[0] // gu_rows)
                     for src, dst in ((wg_hbm, wg_s), (wu_hbm, wu_s))]

        def gu_copy(n):
            src, _, k = gu_chunks[n]
            return pltpu.make_async_copy(src.at[pl.ds(k * gu_rows, gu_rows), :], stage_gu.at[n % gu_slots],
                                         sem_gu.at[n % gu_slots])

        for n in range(gu_slots):
            gu_copy(n).start()
        for m in range(d_slots):
            d_copy(m).start()
        for n in range(len(gu_chunks)):
            gu_copy(n).wait()
            _, dst, k = gu_chunks[n]
            for c in range(n_w):
                dst[c, k * gu_rows:(k + 1) * gu_rows, :] = stage_gu[n % gu_slots, :, c * fw:(c + 1) * fw].astype(BF16)
            if n + gu_slots < len(gu_chunks):
                gu_copy(n + gu_slots).start()
            if n % 2 == 1:
                stream_down(n // 2)

    def gu_copies(c):
        cols = pl.ds(c * fw, fw)
        return [pltpu.make_async_copy(src.at[:, cols], stage_gu.at[c % gu_slots, k], sem_gu.at[c % gu_slots, k])
                for k, src in enumerate((wg_hbm, wu_hbm))]

    def d_copy(m):
        return pltpu.make_async_copy(wd_hbm.at[pl.ds(m * d_rows, d_rows), :], stage_d.at[m % d_slots],
                                     sem_d.at[m % d_slots])

    def stream_begin():
        for c in range(gu_slots):
            for cp in gu_copies(c):
                cp.start()
        for m in range(d_slots):
            d_copy(m).start()

    def stream_gate_up(c):
        for cp in gu_copies(c):
            cp.wait()
        wg_s[c] = stage_gu[c % gu_slots, 0].astype(BF16)
        wu_s[c] = stage_gu[c % gu_slots, 1].astype(BF16)
        if c + gu_slots < n_w:
            for cp in gu_copies(c + gu_slots):
                cp.start()

    def stream_down(m):
        d_copy(m).wait()
        wd_s[m * d_rows:(m + 1) * d_rows, :] = stage_d[m % d_slots].astype(BF16)
        if m + d_slots < n_d:
            d_copy(m + d_slots).start()

    def batch_of(sub):
        return (t * n_sub + sub) // tiles_per_seq

    def norm_stage(sub, h_next):
        rows = pl.ds(sub * tm, tm)
        b = batch_of(sub)
        x = x_ref[rows, :]
        if mix_residual:
            wa = oa_ref.shape[1]
            mix = _dot(oa_ref[rows, :], wo_ref[0:wa, :]) + _dot(ob_ref[rows, :], wo_ref[wa:, :])
            x = x + _mod_row(mod_ref, mod_row - 1, b) * mix
            xres_s[rows, :] = x
        gain = g_ref[...] * (1.0 + _mod_row(mod_ref, mod_row + 1, b))
        ms = jnp.mean(x * x, axis=-1, keepdims=True)
        h_next[...] = (x * lax.rsqrt(ms + EPS) * gain + _mod_row(mod_ref, mod_row, b)).astype(BF16)

    def matmul_stage(sub, h_cur, stream_weights):
        rows = pl.ds(sub * tm, tm)
        if stream_weights:
            stream_begin()
        for c in range(n_w):
            if stream_weights:
                stream_gate_up(c)
            g = _dot(h_cur[...], wg_s[c])
            u = _dot(h_cur[...], wu_s[c])
            a_s[:, c * fw:(c + 1) * fw] = (_silu(g) * u).astype(BF16)
            if stream_weights and c < n_d:
                stream_down(c)
        if stream_weights:
            for m in range(n_w, n_d):
                stream_down(m)
        y = _dot(a_s[...], wd_s[...])
        x = xres_s[rows, :] if mix_residual else x_ref[rows, :]
        out = x + FFN_RES_WEIGHT * _mod_row(mod_ref, mod_row + 2, batch_of(sub)) * y
        if final_norm:
            out = _rms_norm(out, gf_ref[...])
        o_ref[rows, :] = out

    def block(first):
        bufs = (h_even, h_odd)
        norm_stage(0, bufs[0])
        for sub in range(n_sub):
            if sub + 1 < n_sub:
                norm_stage(sub + 1, bufs[(sub + 1) % 2])
            matmul_stage(sub, bufs[sub % 2], stream_weights=first and sub == 0)

    if stream_first_tile:
        @pl.when(t == 0)
        def _():
            block(first=True)

        @pl.when(t > 0)
        def _():
            block(first=False)
    else:
        @pl.when(t == -1)
        def _():
            weight_stage()

        @pl.when(t >= 0)
        def _():
            block(first=False)


def _ffn(x2d, mod3, gain, wg, wu, wd, *, mod_row, seq, stream_first_tile, mix=None, final_gain=None, tm=512,
         tiles_per_step=2, fw=256, stage_chunks=8):
    m, d = x2d.shape
    f = wg.shape[1]
    bm = tm * tiles_per_step
    assert f % fw == 0 and seq % tm == 0 and m % bm == 0 and d % stage_chunks == 0 and f % stage_chunks == 0
    n_w, n_blocks, tiles_per_seq = f // fw, m // bm, seq // tm
    final_norm = final_gain is not None
    lead_steps = 0 if stream_first_tile else 1
    if stream_first_tile:
        stage = [pltpu.VMEM((2, 2, d, fw), F32), pltpu.VMEM((1, f // n_w, d), F32),
                 pltpu.SemaphoreType.DMA((2, 2)), pltpu.SemaphoreType.DMA((1,))]
    else:
        stage = [pltpu.VMEM((2, d // stage_chunks, f), F32), pltpu.VMEM((2, f // stage_chunks, d), F32),
                 pltpu.SemaphoreType.DMA((2,)), pltpu.SemaphoreType.DMA((2,))]

    def block(s):
        return (jnp.maximum(s - lead_steps, 0), 0)

    in_specs = [pl.BlockSpec((bm, d), block), _resident(mod3.shape)]
    args = [x2d, mod3]
    if mix is not None:
        out_a, out_b, w_o = mix
        wa, wb = out_a.shape[1], out_b.shape[1]
        in_specs += [pl.BlockSpec((bm, wa), block), pl.BlockSpec((bm, wb), block), _resident((wa + wb, d))]
        args += [out_a, out_b, w_o]
    hbm = pl.BlockSpec(memory_space=pl.ANY)
    in_specs += [_resident((1, d)), hbm, hbm, hbm]
    args += [gain.reshape(1, d), wg, wu, wd]
    if final_norm:
        in_specs.append(_resident((1, d)))
        args.append(final_gain.reshape(1, d))
    xres_rows = bm if mix is not None else 8
    return pl.pallas_call(
        functools.partial(_ffn_kernel, mod_row=mod_row, tiles_per_seq=tiles_per_seq, mix_residual=mix is not None,
                          final_norm=final_norm, stream_first_tile=stream_first_tile),
        grid=(n_blocks + lead_steps,),
        in_specs=in_specs,
        out_specs=pl.BlockSpec((bm, d), block),
        out_shape=jax.ShapeDtypeStruct((m, d), F32),
        scratch_shapes=[pltpu.VMEM((n_w, d, fw), BF16), pltpu.VMEM((n_w, d, fw), BF16), pltpu.VMEM((f, d), BF16),
                        pltpu.VMEM((tm, d), BF16), pltpu.VMEM((tm, d), BF16),
                        pltpu.VMEM((xres_rows, d), F32), pltpu.VMEM((tm, f), BF16)] + stage,
        compiler_params=_params(1),
        name="ffn_final" if final_norm else "ffn",
    )(*args)


_QA0, _KVA0, _QLAT0, _KVLAT0, _KR0, _KRS0, _PROJ_W = 0, 512, 768, 1024, 1152, 1216, 1280
_UQ_ROPE0 = MLA_HEADS * MLA_NOPE
_UQ_ROPES0 = _UQ_ROPE0 + MLA_HEADS * MLA_ROPE


def _mix_proj_kernel(x_ref, mod_ref, g_ref, win_ref, qn_ref, kvn_ref, wuq_ref, wukv_ref, cos_ref, sin_ref,
                     qa_ref, kva_ref, qm_ref, km_ref, vm_ref, win_s, *, blocks_per_seq, sub_rows):
    @pl.when(pl.program_id(0) == 0)
    def _():
        w = win_ref[...]
        half = MLA_ROPE // 2
        win_s[:, 0:_KRS0] = w.astype(BF16)
        win_s[:, _KRS0:_KRS0 + half] = w[:, _KR0 + half:_KRS0].astype(BF16)
        win_s[:, _KRS0 + half:_PROJ_W] = w[:, _KR0:_KR0 + half].astype(BF16)

    b = pl.program_id(0) // blocks_per_seq
    shift = _mod_row(mod_ref, 3, b)
    scale = _mod_row(mod_ref, 4, b)
    q_scale = MLA_QK ** -0.5 * LOG2E
    for r0 in range(0, x_ref.shape[0], sub_rows):
        rows = slice(r0, r0 + sub_rows)
        h = (_rms_norm(x_ref[rows, :], g_ref[...]) * (1.0 + scale) + shift).astype(BF16)
        proj = _dot(h, win_s[...])
        for hd in range(SWA_HEADS):
            qa_ref[0, hd, rows, :] = (proj[:, _QA0 + hd * SWA_HEAD_DIM:_QA0 + (hd + 1) * SWA_HEAD_DIM]
                                      * (SWA_HEAD_DIM ** -0.5 * LOG2E)).astype(BF16)
        kva_ref[rows, :] = proj[:, _KVA0:_QLAT0].astype(BF16)

        q_lat = _rms_norm(proj[:, _QLAT0:_KVLAT0], qn_ref[...]).astype(BF16)
        kv_lat = _rms_norm(proj[:, _KVLAT0:_KR0], kvn_ref[...]).astype(BF16)
        q_all = _dot(q_lat, wuq_ref[...])
        kv_all = _dot(kv_lat, wukv_ref[...])

        cos = cos_ref[rows, :]
        sin = sin_ref[rows, :]
        k_rope = (proj[:, _KR0:_KRS0] * cos + proj[:, _KRS0:_PROJ_W] * sin).astype(BF16)
        for hd in range(MLA_HEADS):
            q_nope = q_all[:, hd * MLA_NOPE:(hd + 1) * MLA_NOPE]
            q_rope = (q_all[:, _UQ_ROPE0 + hd * MLA_ROPE:_UQ_ROPE0 + (hd + 1) * MLA_ROPE] * cos
                      + q_all[:, _UQ_ROPES0 + hd * MLA_ROPE:_UQ_ROPES0 + (hd + 1) * MLA_ROPE] * sin)
            qm_ref[0, hd, rows, 0:MLA_NOPE] = (q_nope * q_scale).astype(BF16)
            qm_ref[0, hd, rows, MLA_NOPE:MLA_QK] = (q_rope * q_scale).astype(BF16)
            kv0 = hd * (MLA_NOPE + MLA_V)
            km_ref[0, hd, rows, 0:MLA_NOPE] = kv_all[:, kv0:kv0 + MLA_NOPE].astype(BF16)
            km_ref[0, hd, rows, MLA_NOPE:MLA_QK] = k_rope
            vm_ref[0, hd, rows, :] = kv_all[:, kv0 + MLA_NOPE:kv0 + MLA_NOPE + MLA_V].astype(BF16)


def _rope_tables(seq):
    inv = np.float32(ROPE_THETA) ** (-np.arange(0, MLA_ROPE, 2, dtype=np.float32) / np.float32(MLA_ROPE))
    ang = np.arange(seq, dtype=np.float32)[:, None] * inv[None, :].astype(np.float32)
    cos, sin = np.cos(ang).astype(np.float32), np.sin(ang).astype(np.float32)
    return np.concatenate([cos, cos], axis=-1), np.concatenate([-sin, sin], axis=-1)


def _swap_halves(w):
    half = w.shape[-1] // 2
    return jnp.concatenate([w[..., half:], w[..., :half]], axis=-1)


def _mix_proj(x2d, mod3, gain, w_in, q_norm, kv_norm, w_uq, w_ukv, *, batch, seq, tm=1024, sub_rows=512):
    m, d = x2d.shape
    tiles_per_seq = seq // tm
    assert seq % tm == 0 and tm % sub_rows == 0
    w_uq_h = w_uq.reshape(MLA_Q_RANK, MLA_HEADS, MLA_QK)
    uq_rope = w_uq_h[:, :, MLA_NOPE:]
    w_uq_r = jnp.concatenate([w_uq_h[:, :, :MLA_NOPE].reshape(MLA_Q_RANK, -1),
                              uq_rope.reshape(MLA_Q_RANK, -1),
                              _swap_halves(uq_rope).reshape(MLA_Q_RANK, -1)], axis=1).astype(BF16)
    w_ukv_b = w_ukv.astype(BF16)
    cos2, sin2 = (jnp.asarray(tab) for tab in _rope_tables(seq))

    def head_spec(width):
        return pl.BlockSpec((1, MLA_HEADS, tm, width),
                            lambda i: (i // tiles_per_seq, 0, i % tiles_per_seq, 0))

    def head_shape(width):
        return jax.ShapeDtypeStruct((batch, MLA_HEADS, seq, width), BF16)

    return pl.pallas_call(
        functools.partial(_mix_proj_kernel, blocks_per_seq=tiles_per_seq, sub_rows=sub_rows),
        grid=(m // tm,),
        in_specs=[pl.BlockSpec((tm, d), lambda i: (i, 0)),
                  _resident(mod3.shape),
                  _resident((1, d)),
                  _resident(w_in.shape),
                  _resident((1, MLA_Q_RANK)), _resident((1, MLA_KV_RANK)),
                  _resident(w_uq_r.shape), _resident(w_ukv_b.shape),
                  pl.BlockSpec((tm, MLA_ROPE), lambda i: (i % tiles_per_seq, 0)),
                  pl.BlockSpec((tm, MLA_ROPE), lambda i: (i % tiles_per_seq, 0))],
        out_specs=[pl.BlockSpec((1, SWA_HEADS, tm, SWA_HEAD_DIM),
                                lambda i: (i // tiles_per_seq, 0, i % tiles_per_seq, 0)),
                   pl.BlockSpec((tm, 2 * SWA_KV_HEADS * SWA_HEAD_DIM), lambda i: (i, 0)),
                   head_spec(MLA_QK), head_spec(MLA_QK), head_spec(MLA_V)],
        out_shape=[jax.ShapeDtypeStruct((batch, SWA_HEADS, seq, SWA_HEAD_DIM), BF16),
                   jax.ShapeDtypeStruct((m, 2 * SWA_KV_HEADS * SWA_HEAD_DIM), BF16),
                   head_shape(MLA_QK), head_shape(MLA_QK), head_shape(MLA_V)],
        scratch_shapes=[pltpu.VMEM((d, _PROJ_W), BF16)],
        compiler_params=_params(1),
        name="mix_proj",
    )(x2d, mod3, gain.reshape(1, d), w_in, q_norm.reshape(1, -1), kv_norm.reshape(1, -1),
      w_uq_r, w_ukv_b, cos2, sin2)


def _t5_bucket_table():
    qi = np.arange(WINDOW)[:, None]
    kj = np.arange(2 * WINDOW)[None, :]
    dist = qi + WINDOW - kj
    max_exact = NUM_BUCKETS // 2
    n = np.maximum(dist, 0)
    nf = np.maximum(n, 1).astype(np.float32)
    large = max_exact + (np.log(nf / np.float32(max_exact)) / np.float32(math.log(MAX_DISTANCE / max_exact))
                         * np.float32(NUM_BUCKETS - max_exact)).astype(np.int32)
    large = np.minimum(large, NUM_BUCKETS - 1)
    bucket = np.where(n < max_exact, n, large)
    band = (dist >= 0) & (dist < WINDOW)
    return np.where(band, bucket, -1).astype(np.int32)


def _swa_bias_kernel(rel_ref, bucket_ref, o_ref):
    bucket = bucket_ref[...]
    first_block_ok = lax.broadcasted_iota(jnp.int32, bucket.shape, 0) >= WINDOW
    for hd in range(SWA_HEADS):
        acc = jnp.full(bucket.shape, MASK_VALUE, F32)
        for b in range(NUM_BUCKETS):
            acc = jnp.where(bucket == b, rel_ref[b, hd] * LOG2E, acc)
        g, j = divmod(hd, SWA_GROUP)
        o_ref[0, g, :, j * WINDOW:(j + 1) * WINDOW] = jnp.where(first_block_ok, acc, MASK_VALUE)
        o_ref[1, g, :, j * WINDOW:(j + 1) * WINDOW] = acc


def _swa_bias(rel_bias):
    shape = (2, SWA_KV_HEADS, 2 * WINDOW, SWA_GROUP * WINDOW)
    return pl.pallas_call(
        _swa_bias_kernel,
        in_specs=[pl.BlockSpec(memory_space=pltpu.SMEM),
                  pl.BlockSpec((2 * WINDOW, WINDOW), lambda: (0, 0))],
        out_specs=pl.BlockSpec(shape, lambda: (0, 0, 0, 0)),
        out_shape=jax.ShapeDtypeStruct(shape, F32),
        name="swa_bias",
    )(rel_bias, jnp.asarray(_t5_bucket_table().T))


def _swa_kernel(sink_ref, q_ref, kv_ref, bias_ref, o_ref, kp_ref, vt_ref, *, blocks_per_iter):
    seq = kv_ref.shape[0]
    n_blocks = seq // WINDOW
    kdim = SWA_KV_HEADS * SWA_HEAD_DIM
    kp_ref[0:WINDOW, :] = jnp.zeros((WINDOW, kdim), BF16)
    kp_ref[WINDOW:, :] = kv_ref[:, 0:kdim]
    ones_rows = jnp.ones((SWA_ONES_ROWS, WINDOW), BF16)
    for g in range(SWA_KV_HEADS):
        vt_ref[0, g] = jnp.concatenate([jnp.zeros((SWA_HEAD_DIM, WINDOW), BF16), ones_rows], axis=0)
    for n in range(n_blocks):
        for g in range(SWA_KV_HEADS):
            v_blk = kv_ref[n * WINDOW:(n + 1) * WINDOW, kdim + g * SWA_HEAD_DIM:kdim + (g + 1) * SWA_HEAD_DIM]
            vt_ref[n + 1, g] = jnp.concatenate([v_blk.astype(F32).T.astype(BF16), ones_rows], axis=0)
    sink_rows = [[jnp.full((1, WINDOW), sink_ref[g * SWA_GROUP + j] * LOG2E, F32) for j in range(SWA_GROUP)]
                 for g in range(SWA_KV_HEADS)]

    def body(it, carry):
        units = [(b, g) for b in range(blocks_per_iter) for g in range(SWA_KV_HEADS)]
        blocks = [it * blocks_per_iter + b for b in range(blocks_per_iter)]
        q_starts = [pl.multiple_of(n * WINDOW, WINDOW) for n in blocks]
        tables = [jnp.minimum(n, 1) for n in blocks]
        sts, pts = {}, {}

        def scores(u):
            b, g = units[u]
            k = kp_ref[pl.ds(q_starts[b], 2 * WINDOW), g * SWA_HEAD_DIM:(g + 1) * SWA_HEAD_DIM]
            sts[u] = []
            for j in range(0, SWA_GROUP, 2):
                hd = g * SWA_GROUP + j
                q = jnp.concatenate([q_ref[0, hd, pl.ds(q_starts[b], WINDOW), :],
                                     q_ref[0, hd + 1, pl.ds(q_starts[b], WINDOW), :]], axis=0)
                st = lax.dot_general(k, q, NT_DIMS, preferred_element_type=F32)
                sts[u] += [st[:, 0:WINDOW], st[:, WINDOW:2 * WINDOW]]

        def softmax(u):
            b, g = units[u]
            pts[u] = []
            for j in range(SWA_GROUP):
                st = sts[u][j] + bias_ref[tables[b], g, :, j * WINDOW:(j + 1) * WINDOW]
                mx = jnp.maximum(jnp.max(st, axis=0, keepdims=True), sink_rows[g][j])
                pts[u].append((jnp.exp2(st - mx).astype(BF16), jnp.exp2(sink_rows[g][j] - mx)))

        def values(u):
            b, g = units[u]
            vt = jnp.concatenate([vt_ref[blocks[b], g], vt_ref[blocks[b] + 1, g]], axis=1)
            for j in range(0, SWA_GROUP, 2):
                pt = jnp.concatenate([pts[u][j][0], pts[u][j + 1][0]], axis=1)
                ot = _dot(vt, pt)
                halves = []
                for jj in range(2):
                    lanes = slice(jj * WINDOW, (jj + 1) * WINDOW)
                    denom = ot[SWA_HEAD_DIM:SWA_HEAD_DIM + 1, lanes] + pts[u][j + jj][1]
                    halves.append(ot[0:SWA_HEAD_DIM, lanes] * (1.0 / denom))
                pair = jnp.concatenate(halves, axis=0).T
                lane0 = (g * SWA_GROUP + j) * SWA_HEAD_DIM
                o_ref[pl.ds(q_starts[b], WINDOW), lane0:lane0 + 2 * SWA_HEAD_DIM] = pair.astype(BF16)

        for t in range(len(units) + 2):
            if t < len(units):
                scores(t)
            if 0 <= t - 1 < len(units):
                softmax(t - 1)
            if 0 <= t - 2 < len(units):
                values(t - 2)
        return carry

    lax.fori_loop(0, n_blocks // blocks_per_iter, body, 0)


def _swa(qa, kva, bias, sinks, *, batch, seq, blocks_per_iter=16):
    kvw = kva.shape[1]
    qw = SWA_HEADS * SWA_HEAD_DIM
    assert (seq // WINDOW) % blocks_per_iter == 0
    return pl.pallas_call(
        functools.partial(_swa_kernel, blocks_per_iter=blocks_per_iter),
        grid=(batch,),
        in_specs=[pl.BlockSpec(memory_space=pltpu.SMEM),
                  pl.BlockSpec((1, SWA_HEADS, seq, SWA_HEAD_DIM), lambda b: (b, 0, 0, 0)),
                  pl.BlockSpec((seq, kvw), lambda b: (b, 0)),
                  _resident(bias.shape)],
        out_specs=pl.BlockSpec((seq, qw), lambda b: (b, 0)),
        out_shape=jax.ShapeDtypeStruct((batch * seq, qw), BF16),
        scratch_shapes=[pltpu.VMEM((WINDOW + seq, SWA_KV_HEADS * SWA_HEAD_DIM), BF16),
                        pltpu.VMEM((1 + seq // WINDOW, SWA_KV_HEADS, SWA_HEAD_DIM + SWA_ONES_ROWS, WINDOW), BF16)],
        compiler_params=_params(1),
        name="swa",
    )(sinks, qa, kva, bias)


def _mla_kernel(q_ref, k_ref, v_ref, o_ref, vt_ref, *, tq):
    heads, seq, dv = v_ref.shape[1], v_ref.shape[2], v_ref.shape[3]
    n_tiles = seq // tq
    key = lax.broadcasted_iota(jnp.int32, (tq, tq), 0)
    qry = lax.broadcasted_iota(jnp.int32, (tq, tq), 1)
    causal = key <= qry
    for hd in range(heads):
        for c in range(seq // WINDOW):
            v_blk = v_ref[0, hd, c * WINDOW:(c + 1) * WINDOW, :]
            vt_ref[hd, 0:dv, c * WINDOW:(c + 1) * WINDOW] = v_blk.astype(F32).T.astype(BF16)
        vt_ref[hd, dv:, :] = jnp.ones((MLA_ONES_ROWS, seq), BF16)
    sts, mxs, pts, accs = {}, {}, {}, {}

    def scores(hd, i, j):
        q = q_ref[0, hd, i * tq:(i + 1) * tq, :]
        s = lax.dot_general(k_ref[0, hd, j * tq:(j + 1) * tq, :], q, NT_DIMS, preferred_element_type=F32)
        sts[hd, i, j] = jnp.where(causal, s, MASK_VALUE) if j == i else s

    def col_max(hd, i, j):
        m = jnp.max(sts[hd, i, j], axis=0, keepdims=True)
        mxs[hd, i] = m if j == 0 else jnp.maximum(mxs[hd, i], m)

    def probs(hd, i, j):
        pts[hd, i, j] = jnp.exp2(sts.pop((hd, i, j)) - mxs[hd, i]).astype(BF16)

    def values(hd, i, j):
        part = _dot(vt_ref[hd, :, j * tq:(j + 1) * tq], pts.pop((hd, i, j)))
        accs[hd, i] = part if j == 0 else accs[hd, i] + part
        if j == i:
            ot = accs.pop((hd, i))
            o = ot[0:dv, :] * (1.0 / ot[dv:dv + 1, :])
            o_ref[i * tq:(i + 1) * tq, hd * dv:(hd + 1) * dv] = o.T.astype(BF16)

    stages = (scores, col_max, probs, values)
    units = [(hd, i) for hd in range(heads) for i in range(n_tiles)]
    for t in range(len(units) + len(stages) - 1):
        for j in range(n_tiles):
            for lag, stage in enumerate(stages):
                if 0 <= t - lag < len(units):
                    hd, i = units[t - lag]
                    if j <= i:
                        stage(hd, i, j)


def _mla(qm, km, vm, *, tq=256):
    batch, heads, seq, qk = qm.shape
    dv = vm.shape[-1]
    return pl.pallas_call(
        functools.partial(_mla_kernel, tq=tq),
        grid=(batch,),
        in_specs=[pl.BlockSpec((1, heads, seq, qk), lambda b: (b, 0, 0, 0)),
                  pl.BlockSpec((1, heads, seq, qk), lambda b: (b, 0, 0, 0)),
                  pl.BlockSpec((1, heads, seq, dv), lambda b: (b, 0, 0, 0))],
        out_specs=pl.BlockSpec((seq, heads * dv), lambda b: (b, 0)),
        out_shape=jax.ShapeDtypeStruct((batch * seq, heads * dv), BF16),
        scratch_shapes=[pltpu.VMEM((heads, dv + MLA_ONES_ROWS, seq), BF16)],
        compiler_params=_params(1),
        name="mla",
    )(qm, km, vm)


def kernel(x, c, w_mod, b_mod, norm_ffn1, ffn1_gate, ffn1_up, ffn1_down, norm_mix, w_in, q_norm, kv_norm, w_uq, w_ukv, sinks, w_o, norm_ffn2, ffn2_gate, ffn2_up, ffn2_down, rel_bias, norm_final):
    batch, seq, d = x.shape
    depth = w_mod.shape[0]
    x2d = x.reshape(batch * seq, d)
    bias = _swa_bias(rel_bias)
    for l in range(depth):
        mod3 = _mod(c, w_mod[l], b_mod[l])
        x2d = _ffn(x2d, mod3, norm_ffn1[l], ffn1_gate[l], ffn1_up[l], ffn1_down[l], mod_row=0, seq=seq,
                   stream_first_tile=True)
        qa, kva, qm, km, vm = _mix_proj(x2d, mod3, norm_mix[l], w_in[l], q_norm[l], kv_norm[l],
                                        w_uq[l], w_ukv[l], batch=batch, seq=seq)
        out_a = _swa(qa, kva, bias, sinks[l], batch=batch, seq=seq)
        out_b = _mla(qm, km, vm)
        last = l == depth - 1
        x2d = _ffn(x2d, mod3, norm_ffn2[l], ffn2_gate[l], ffn2_up[l], ffn2_down[l], mod_row=6, seq=seq,
                   stream_first_tile=False, mix=(out_a, out_b, w_o[l].astype(BF16)),
                   final_gain=norm_final if last else None)
    return x2d.reshape(batch, seq, d)
```

```python
import functools
import math

import jax
import jax.numpy as jnp
import numpy as np
from jax import lax
from jax.experimental import pallas as pl
from jax.experimental.pallas import tpu as pltpu

EPS = 1e-6
FFN_RES_WEIGHT = 0.5
N_MOD = 9

SWA_HEADS = 8
SWA_KV_HEADS = 2
SWA_HEAD_DIM = 64
SWA_GROUP = SWA_HEADS // SWA_KV_HEADS
WINDOW = 128

MLA_HEADS = 4
MLA_Q_RANK = 256
MLA_KV_RANK = 128
MLA_NOPE = 128
MLA_ROPE = 64
MLA_V = 128
MLA_QK = MLA_NOPE + MLA_ROPE
ROPE_THETA = 10000.0

NUM_BUCKETS = 32
MAX_DISTANCE = 128

SWA_ONES_ROWS = 16
MLA_ONES_ROWS = 16
LOG2E = math.log2(math.e)
MASK_VALUE = -1e30
V7X_VMEM_LIMIT_BYTES = 56 * 1024 * 1024

F32 = jnp.float32
BF16 = jnp.bfloat16
NT_DIMS = (((1,), (1,)), ((), ()))


def _params(n_grid_dims):
    return pltpu.CompilerParams(dimension_semantics=("arbitrary",) * n_grid_dims,
                                vmem_limit_bytes=V7X_VMEM_LIMIT_BYTES)


def _resident(shape):
    return pl.BlockSpec(shape, lambda *_: (0,) * len(shape), pipeline_mode=pl.Buffered(1))


def _rms_norm(x, gain):
    ms = jnp.mean(x * x, axis=-1, keepdims=True)
    return x * lax.rsqrt(ms + EPS) * gain


def _silu(x):
    return x * (1.0 / (1.0 + jnp.exp(-x)))


def _dot(a, b):
    return jnp.dot(a, b, preferred_element_type=F32)


def _mod_kernel(c_ref, w_ref, b_ref, o_ref, acc_ref):
    k = pl.program_id(0)

    @pl.when(k == 0)
    def _():
        acc_ref[...] = jnp.broadcast_to(b_ref[...], acc_ref.shape)

    c_act = _silu(c_ref[...]).astype(BF16)
    acc_ref[...] += _dot(c_act, w_ref[...].astype(BF16))

    @pl.when(k == pl.num_programs(0) - 1)
    def _():
        d = o_ref.shape[2]
        for r in range(o_ref.shape[0]):
            o_ref[r] = acc_ref[:, r * d:(r + 1) * d]


def _mod(c, w_mod, b_mod, *, k_rows=128):
    batch, d = c.shape
    n = w_mod.shape[1]
    return pl.pallas_call(
        _mod_kernel,
        grid=(d // k_rows,),
        in_specs=[pl.BlockSpec((batch, k_rows), lambda k: (0, k)),
                  pl.BlockSpec((k_rows, n), lambda k: (k, 0)),
                  _resident((1, n))],
        out_specs=pl.BlockSpec((n // d, batch, d), lambda k: (0, 0, 0)),
        out_shape=jax.ShapeDtypeStruct((n // d, batch, d), F32),
        scratch_shapes=[pltpu.VMEM((batch, n), F32)],
        compiler_params=_params(1),
        name="mod",
    )(c, w_mod, b_mod.reshape(1, n))


def _mod_row(mod_ref, row, b):
    return mod_ref[row, pl.ds(b, 1), :]


def _ffn_kernel(*refs, mod_row, tiles_per_seq, mix_residual, final_norm, stream_first_tile):
    refs = list(refs)
    x_ref, mod_ref = refs[:2]
    del refs[:2]
    if mix_residual:
        oa_ref, ob_ref, wo_ref = refs[:3]
        del refs[:3]
    g_ref, wg_hbm, wu_hbm, wd_hbm = refs[:4]
    del refs[:4]
    if final_norm:
        gf_ref = refs.pop(0)
    (o_ref, wg_s, wu_s, wd_s, h_even, h_odd, xres_s, a_s,
     stage_gu, stage_d, sem_gu, sem_d) = refs
    n_w, _, fw = wg_s.shape
    tm = h_even.shape[0]
    n_sub = x_ref.shape[0] // tm
    lead_steps = 0 if stream_first_tile else 1
    t = pl.program_id(0) - lead_steps
    gu_slots, d_slots, d_rows = stage_gu.shape[0], stage_d.shape[0], stage_d.shape[1]
    n_d = wd_hbm.shape[0] // d_rows

    def weight_stage():
        gu_rows = stage_gu.shape[1]
        gu_chunks = [(src, dst, k) for k in range(wg_hbm.shape[0] // gu_rows)
                     for src, dst in ((wg_hbm, wg_s), (wu_hbm, wu_s))]

        def gu_copy(n):
            src, _, k = gu_chunks[n]
            return pltpu.make_async_copy(src.at[pl.ds(k * gu_rows, gu_rows), :], stage_gu.at[n % gu_slots],
                                         sem_gu.at[n % gu_slots])

        for n in range(gu_slots):
            gu_copy(n).start()
        for m in range(d_slots):
            d_copy(m).start()
        for n in range(len(gu_chunks)):
            gu_copy(n).wait()
            _, dst, k = gu_chunks[n]
            for c in range(n_w):
                dst[c, k * gu_rows:(k + 1) * gu_rows, :] = stage_gu[n % gu_slots, :, c * fw:(c + 1) * fw].astype(BF16)
            if n + gu_slots < len(gu_chunks):
                gu_copy(n + gu_slots).start()
            if n % 2 == 1:
                stream_down(n // 2)

    def gu_copies(c):
        cols = pl.ds(c * fw, fw)
        return [pltpu.make_async_copy(src.at[:, cols], stage_gu.at[c % gu_slots, k], sem_gu.at[c % gu_slots, k])
                for k, src in enumerate((wg_hbm, wu_hbm))]

    def d_copy(m):
        return pltpu.make_async_copy(wd_hbm.at[pl.ds(m * d_rows, d_rows), :], stage_d.at[m % d_slots],
                                     sem_d.at[m % d_slots])

    def stream_begin():
        for c in range(gu_slots):
            for cp in gu_copies(c):
                cp.start()
        for m in range(d_slots):
            d_copy(m).start()

    def stream_gate_up(c):
        for cp in gu_copies(c):
            cp.wait()
        wg_s[c] = stage_gu[c % gu_slots, 0].astype(BF16)
        wu_s[c] = stage_gu[c % gu_slots, 1].astype(BF16)
        if c + gu_slots < n_w:
            for cp in gu_copies(c + gu_slots):
                cp.start()

    def stream_down(m):
        d_copy(m).wait()
        wd_s[m * d_rows:(m + 1) * d_rows, :] = stage_d[m % d_slots].astype(BF16)
        if m + d_slots < n_d:
            d_copy(m + d_slots).start()

    def batch_of(sub):
        return (t * n_sub + sub) // tiles_per_seq

    def norm_stage(sub, h_next):
        rows = pl.ds(sub * tm, tm)
        b = batch_of(sub)
        x = x_ref[rows, :]
        if mix_residual:
            wa = oa_ref.shape[1]
            mix = _dot(oa_ref[rows, :], wo_ref[0:wa, :]) + _dot(ob_ref[rows, :], wo_ref[wa:, :])
            x = x + _mod_row(mod_ref, mod_row - 1, b) * mix
            xres_s[rows, :] = x
        gain = g_ref[...] * (1.0 + _mod_row(mod_ref, mod_row + 1, b))
        ms = jnp.mean(x * x, axis=-1, keepdims=True)
        h_next[...] = (x * lax.rsqrt(ms + EPS) * gain + _mod_row(mod_ref, mod_row, b)).astype(BF16)

    def matmul_stage(sub, h_cur, stream_weights):
        rows = pl.ds(sub * tm, tm)
        if stream_weights:
            stream_begin()
        for c in range(n_w):
            if stream_weights:
                stream_gate_up(c)
            g = _dot(h_cur[...], wg_s[c])
            u = _dot(h_cur[...], wu_s[c])
            a_s[:, c * fw:(c + 1) * fw] = (_silu(g) * u).astype(BF16)
            if stream_weights and c < n_d:
                stream_down(c)
        if stream_weights:
            for m in range(n_w, n_d):
                stream_down(m)
        y = _dot(a_s[...], wd_s[...])
        x = xres_s[rows, :] if mix_residual else x_ref[rows, :]
        out = x + FFN_RES_WEIGHT * _mod_row(mod_ref, mod_row + 2, batch_of(sub)) * y
        if final_norm:
            out = _rms_norm(out, gf_ref[...])
        o_ref[rows, :] = out

    def block(first):
        bufs = (h_even, h_odd)
        norm_stage(0, bufs[0])
        for sub in range(n_sub):
            if sub + 1 < n_sub:
                norm_stage(sub + 1, bufs[(sub + 1) % 2])
            matmul_stage(sub, bufs[sub % 2], stream_weights=first and sub == 0)

    if stream_first_tile:
        @pl.when(t == 0)
        def _():
            block(first=True)

        @pl.when(t > 0)
        def _():
            block(first=False)
    else:
        @pl.when(t == -1)
        def _():
            weight_stage()

        @pl.when(t >= 0)
        def _():
            block(first=False)


def _ffn(x2d, mod3, gain, wg, wu, wd, *, mod_row, seq, stream_first_tile, mix=None, final_gain=None, tm=512,
         tiles_per_step=2, fw=256, stage_chunks=8):
    m, d = x2d.shape
    f = wg.shape[1]
    bm = tm * tiles_per_step
    assert f % fw == 0 and seq % tm == 0 and m % bm == 0 and d % stage_chunks == 0 and f % stage_chunks == 0
    n_w, n_blocks, tiles_per_seq = f // fw, m // bm, seq // tm
    final_norm = final_gain is not None
    lead_steps = 0 if stream_first_tile else 1
    if stream_first_tile:
        stage = [pltpu.VMEM((2, 2, d, fw), F32), pltpu.VMEM((1, f // n_w, d), F32),
                 pltpu.SemaphoreType.DMA((2, 2)), pltpu.SemaphoreType.DMA((1,))]
    else:
        stage = [pltpu.VMEM((2, d // stage_chunks, f), F32), pltpu.VMEM((2, f // stage_chunks, d), F32),
                 pltpu.SemaphoreType.DMA((2,)), pltpu.SemaphoreType.DMA((2,))]

    def block(s):
        return (jnp.maximum(s - lead_steps, 0), 0)

    in_specs = [pl.BlockSpec((bm, d), block), _resident(mod3.shape)]
    args = [x2d, mod3]
    if mix is not None:
        out_a, out_b, w_o = mix
        wa, wb = out_a.shape[1], out_b.shape[1]
        in_specs += [pl.BlockSpec((bm, wa), block), pl.BlockSpec((bm, wb), block), _resident((wa + wb, d))]
        args += [out_a, out_b, w_o]
    hbm = pl.BlockSpec(memory_space=pl.ANY)
    in_specs += [_resident((1, d)), hbm, hbm, hbm]
    args += [gain.reshape(1, d), wg, wu, wd]
    if final_norm:
        in_specs.append(_resident((1, d)))
        args.append(final_gain.reshape(1, d))
    xres_rows = bm if mix is not None else 8
    return pl.pallas_call(
        functools.partial(_ffn_kernel, mod_row=mod_row, tiles_per_seq=tiles_per_seq, mix_residual=mix is not None,
                          final_norm=final_norm, stream_first_tile=stream_first_tile),
        grid=(n_blocks + lead_steps,),
        in_specs=in_specs,
        out_specs=pl.BlockSpec((bm, d), block),
        out_shape=jax.ShapeDtypeStruct((m, d), F32),
        scratch_shapes=[pltpu.VMEM((n_w, d, fw), BF16), pltpu.VMEM((n_w, d, fw), BF16), pltpu.VMEM((f, d), BF16),
                        pltpu.VMEM((tm, d), BF16), pltpu.VMEM((tm, d), BF16),
                        pltpu.VMEM((xres_rows, d), F32), pltpu.VMEM((tm, f), BF16)] + stage,
        compiler_params=_params(1),
        name="ffn_final" if final_norm else "ffn",
    )(*args)


_QA0, _KVA0, _QLAT0, _KVLAT0, _KR0, _KRS0, _PROJ_W = 0, 512, 768, 1024, 1152, 1216, 1280
_UQ_ROPE0 = MLA_HEADS * MLA_NOPE
_UQ_ROPES0 = _UQ_ROPE0 + MLA_HEADS * MLA_ROPE


def _mix_proj_kernel(x_ref, mod_ref, g_ref, win_ref, qn_ref, kvn_ref, wuq_ref, wukv_ref, cos_ref, sin_ref,
                     qa_ref, kva_ref, qm_ref, km_ref, vm_ref, win_s, *, blocks_per_seq, sub_rows):
    @pl.when(pl.program_id(0) == 0)
    def _():
        w = win_ref[...]
        half = MLA_ROPE // 2
        win_s[:, 0:_KRS0] = w.astype(BF16)
        win_s[:, _KRS0:_KRS0 + half] = w[:, _KR0 + half:_KRS0].astype(BF16)
        win_s[:, _KRS0 + half:_PROJ_W] = w[:, _KR0:_KR0 + half].astype(BF16)

    b = pl.program_id(0) // blocks_per_seq
    shift = _mod_row(mod_ref, 3, b)
    scale = _mod_row(mod_ref, 4, b)
    q_scale = MLA_QK ** -0.5 * LOG2E
    for r0 in range(0, x_ref.shape[0], sub_rows):
        rows = slice(r0, r0 + sub_rows)
        h = (_rms_norm(x_ref[rows, :], g_ref[...]) * (1.0 + scale) + shift).astype(BF16)
        proj = _dot(h, win_s[...])
        for hd in range(SWA_HEADS):
            qa_ref[0, hd, rows, :] = (proj[:, _QA0 + hd * SWA_HEAD_DIM:_QA0 + (hd + 1) * SWA_HEAD_DIM]
                                      * (SWA_HEAD_DIM ** -0.5 * LOG2E)).astype(BF16)
        kva_ref[rows, :] = proj[:, _KVA0:_QLAT0].astype(BF16)

        q_lat = _rms_norm(proj[:, _QLAT0:_KVLAT0], qn_ref[...]).astype(BF16)
        kv_lat = _rms_norm(proj[:, _KVLAT0:_KR0], kvn_ref[...]).astype(BF16)
        q_all = _dot(q_lat, wuq_ref[...])
        kv_all = _dot(kv_lat, wukv_ref[...])

        cos = cos_ref[rows, :]
        sin = sin_ref[rows, :]
        k_rope = (proj[:, _KR0:_KRS0] * cos + proj[:, _KRS0:_PROJ_W] * sin).astype(BF16)
        for hd in range(MLA_HEADS):
            q_nope = q_all[:, hd * MLA_NOPE:(hd + 1) * MLA_NOPE]
            q_rope = (q_all[:, _UQ_ROPE0 + hd * MLA_ROPE:_UQ_ROPE0 + (hd + 1) * MLA_ROPE] * cos
                      + q_all[:, _UQ_ROPES0 + hd * MLA_ROPE:_UQ_ROPES0 + (hd + 1) * MLA_ROPE] * sin)
            qm_ref[0, hd, rows, 0:MLA_NOPE] = (q_nope * q_scale).astype(BF16)
            qm_ref[0, hd, rows, MLA_NOPE:MLA_QK] = (q_rope * q_scale).astype(BF16)
            kv0 = hd * (MLA_NOPE + MLA_V)
            km_ref[0, hd, rows, 0:MLA_NOPE] = kv_all[:, kv0:kv0 + MLA_NOPE].astype(BF16)
            km_ref[0, hd, rows, MLA_NOPE:MLA_QK] = k_rope
            vm_ref[0, hd, rows, :] = kv_all[:, kv0 + MLA_NOPE:kv0 + MLA_NOPE + MLA_V].astype(BF16)


def _rope_tables(seq):
    inv = np.float32(ROPE_THETA) ** (-np.arange(0, MLA_ROPE, 2, dtype=np.float32) / np.float32(MLA_ROPE))
    ang = np.arange(seq, dtype=np.float32)[:, None] * inv[None, :].astype(np.float32)
    cos, sin = np.cos(ang).astype(np.float32), np.sin(ang).astype(np.float32)
    return np.concatenate([cos, cos], axis=-1), np.concatenate([-sin, sin], axis=-1)


def _swap_halves(w):
    half = w.shape[-1] // 2
    return jnp.concatenate([w[..., half:], w[..., :half]], axis=-1)


def _mix_proj(x2d, mod3, gain, w_in, q_norm, kv_norm, w_uq, w_ukv, *, batch, seq, tm=1024, sub_rows=512):
    m, d = x2d.shape
    tiles_per_seq = seq // tm
    assert seq % tm == 0 and tm % sub_rows == 0
    w_uq_h = w_uq.reshape(MLA_Q_RANK, MLA_HEADS, MLA_QK)
    uq_rope = w_uq_h[:, :, MLA_NOPE:]
    w_uq_r = jnp.concatenate([w_uq_h[:, :, :MLA_NOPE].reshape(MLA_Q_RANK, -1),
                              uq_rope.reshape(MLA_Q_RANK, -1),
                              _swap_halves(uq_rope).reshape(MLA_Q_RANK, -1)], axis=1).astype(BF16)
    w_ukv_b = w_ukv.astype(BF16)
    cos2, sin2 = (jnp.asarray(tab) for tab in _rope_tables(seq))

    def head_spec(width):
        return pl.BlockSpec((1, MLA_HEADS, tm, width),
                            lambda i: (i // tiles_per_seq, 0, i % tiles_per_seq, 0))

    def head_shape(width):
        return jax.ShapeDtypeStruct((batch, MLA_HEADS, seq, width), BF16)

    return pl.pallas_call(
        functools.partial(_mix_proj_kernel, blocks_per_seq=tiles_per_seq, sub_rows=sub_rows),
        grid=(m // tm,),
        in_specs=[pl.BlockSpec((tm, d), lambda i: (i, 0)),
                  _resident(mod3.shape),
                  _resident((1, d)),
                  _resident(w_in.shape),
                  _resident((1, MLA_Q_RANK)), _resident((1, MLA_KV_RANK)),
                  _resident(w_uq_r.shape), _resident(w_ukv_b.shape),
                  pl.BlockSpec((tm, MLA_ROPE), lambda i: (i % tiles_per_seq, 0)),
                  pl.BlockSpec((tm, MLA_ROPE), lambda i: (i % tiles_per_seq, 0))],
        out_specs=[pl.BlockSpec((1, SWA_HEADS, tm, SWA_HEAD_DIM),
                                lambda i: (i // tiles_per_seq, 0, i % tiles_per_seq, 0)),
                   pl.BlockSpec((tm, 2 * SWA_KV_HEADS * SWA_HEAD_DIM), lambda i: (i, 0)),
                   head_spec(MLA_QK), head_spec(MLA_QK), head_spec(MLA_V)],
        out_shape=[jax.ShapeDtypeStruct((batch, SWA_HEADS, seq, SWA_HEAD_DIM), BF16),
                   jax.ShapeDtypeStruct((m, 2 * SWA_KV_HEADS * SWA_HEAD_DIM), BF16),
                   head_shape(MLA_QK), head_shape(MLA_QK), head_shape(MLA_V)],
        scratch_shapes=[pltpu.VMEM((d, _PROJ_W), BF16)],
        compiler_params=_params(1),
        name="mix_proj",
    )(x2d, mod3, gain.reshape(1, d), w_in, q_norm.reshape(1, -1), kv_norm.reshape(1, -1),
      w_uq_r, w_ukv_b, cos2, sin2)


def _t5_bucket_table():
    qi = np.arange(WINDOW)[:, None]
    kj = np.arange(2 * WINDOW)[None, :]
    dist = qi + WINDOW - kj
    max_exact = NUM_BUCKETS // 2
    n = np.maximum(dist, 0)
    nf = np.maximum(n, 1).astype(np.float32)
    large = max_exact + (np.log(nf / np.float32(max_exact)) / np.float32(math.log(MAX_DISTANCE / max_exact))
                         * np.float32(NUM_BUCKETS - max_exact)).astype(np.int32)
    large = np.minimum(large, NUM_BUCKETS - 1)
    bucket = np.where(n < max_exact, n, large)
    band = (dist >= 0) & (dist < WINDOW)
    return np.where(band, bucket, -1).astype(np.int32)


def _swa_bias_kernel(rel_ref, bucket_ref, o_ref):
    bucket = bucket_ref[...]
    first_block_ok = lax.broadcasted_iota(jnp.int32, bucket.shape, 0) >= WINDOW
    for hd in range(SWA_HEADS):
        acc = jnp.full(bucket.shape, MASK_VALUE, F32)
        for b in range(NUM_BUCKETS):
            acc = jnp.where(bucket == b, rel_ref[b, hd] * LOG2E, acc)
        g, j = divmod(hd, SWA_GROUP)
        o_ref[0, g, :, j * WINDOW:(j + 1) * WINDOW] = jnp.where(first_block_ok, acc, MASK_VALUE)
        o_ref[1, g, :, j * WINDOW:(j + 1) * WINDOW] = acc


def _swa_bias(rel_bias):
    shape = (2, SWA_KV_HEADS, 2 * WINDOW, SWA_GROUP * WINDOW)
    return pl.pallas_call(
        _swa_bias_kernel,
        in_specs=[pl.BlockSpec(memory_space=pltpu.SMEM),
                  pl.BlockSpec((2 * WINDOW, WINDOW), lambda: (0, 0))],
        out_specs=pl.BlockSpec(shape, lambda: (0, 0, 0, 0)),
        out_shape=jax.ShapeDtypeStruct(shape, F32),
        name="swa_bias",
    )(rel_bias, jnp.asarray(_t5_bucket_table().T))


def _swa_kernel(sink_ref, q_ref, kv_ref, bias_ref, o_ref, kp_ref, vt_ref, *, blocks_per_iter):
    seq = kv_ref.shape[0]
    n_blocks = seq // WINDOW
    kdim = SWA_KV_HEADS * SWA_HEAD_DIM
    kp_ref[0:WINDOW, :] = jnp.zeros((WINDOW, kdim), BF16)
    kp_ref[WINDOW:, :] = kv_ref[:, 0:kdim]
    ones_rows = jnp.ones((SWA_ONES_ROWS, WINDOW), BF16)
    for g in range(SWA_KV_HEADS):
        vt_ref[0, g] = jnp.concatenate([jnp.zeros((SWA_HEAD_DIM, WINDOW), BF16), ones_rows], axis=0)
    for n in range(n_blocks):
        for g in range(SWA_KV_HEADS):
            v_blk = kv_ref[n * WINDOW:(n + 1) * WINDOW, kdim + g * SWA_HEAD_DIM:kdim + (g + 1) * SWA_HEAD_DIM]
            vt_ref[n + 1, g] = jnp.concatenate([v_blk.astype(F32).T.astype(BF16), ones_rows], axis=0)
    sink_rows = [[jnp.full((1, WINDOW), sink_ref[g * SWA_GROUP + j] * LOG2E, F32) for j in range(SWA_GROUP)]
                 for g in range(SWA_KV_HEADS)]

    def body(it, carry):
        units = [(b, g) for b in range(blocks_per_iter) for g in range(SWA_KV_HEADS)]
        blocks = [it * blocks_per_iter + b for b in range(blocks_per_iter)]
        q_starts = [pl.multiple_of(n * WINDOW, WINDOW) for n in blocks]
        tables = [jnp.minimum(n, 1) for n in blocks]
        sts, pts = {}, {}

        def scores(u):
            b, g = units[u]
            k = kp_ref[pl.ds(q_starts[b], 2 * WINDOW), g * SWA_HEAD_DIM:(g + 1) * SWA_HEAD_DIM]
            sts[u] = []
            for j in range(0, SWA_GROUP, 2):
                hd = g * SWA_GROUP + j
                q = jnp.concatenate([q_ref[0, hd, pl.ds(q_starts[b], WINDOW), :],
                                     q_ref[0, hd + 1, pl.ds(q_starts[b], WINDOW), :]], axis=0)
                st = lax.dot_general(k, q, NT_DIMS, preferred_element_type=F32)
                sts[u] += [st[:, 0:WINDOW], st[:, WINDOW:2 * WINDOW]]

        def softmax(u):
            b, g = units[u]
            pts[u] = []
            for j in range(SWA_GROUP):
                st = sts[u][j] + bias_ref[tables[b], g, :, j * WINDOW:(j + 1) * WINDOW]
                mx = jnp.maximum(jnp.max(st, axis=0, keepdims=True), sink_rows[g][j])
                pts[u].append((jnp.exp2(st - mx).astype(BF16), jnp.exp2(sink_rows[g][j] - mx)))

        def values(u):
            b, g = units[u]
            vt = jnp.concatenate([vt_ref[blocks[b], g], vt_ref[blocks[b] + 1, g]], axis=1)
            for j in range(0, SWA_GROUP, 2):
                pt = jnp.concatenate([pts[u][j][0], pts[u][j + 1][0]], axis=1)
                ot = _dot(vt, pt)
                halves = []
                for jj in range(2):
                    lanes = slice(jj * WINDOW, (jj + 1) * WINDOW)
                    denom = ot[SWA_HEAD_DIM:SWA_HEAD_DIM + 1, lanes] + pts[u][j + jj][1]
                    halves.append(ot[0:SWA_HEAD_DIM, lanes] * (1.0 / denom))
                pair = jnp.concatenate(halves, axis=0).T
                lane0 = (g * SWA_GROUP + j) * SWA_HEAD_DIM
                o_ref[pl.ds(q_starts[b], WINDOW), lane0:lane0 + 2 * SWA_HEAD_DIM] = pair.astype(BF16)

        for t in range(len(units) + 2):
            if t < len(units):
                scores(t)
            if 0 <= t - 1 < len(units):
                softmax(t - 1)
            if 0 <= t - 2 < len(units):
                values(t - 2)
        return carry

    lax.fori_loop(0, n_blocks // blocks_per_iter, body, 0)


def _swa(qa, kva, bias, sinks, *, batch, seq, blocks_per_iter=16):
    kvw = kva.shape[1]
    qw = SWA_HEADS * SWA_HEAD_DIM
    assert (seq // WINDOW) % blocks_per_iter == 0
    return pl.pallas_call(
        functools.partial(_swa_kernel, blocks_per_iter=blocks_per_iter),
        grid=(batch,),
        in_specs=[pl.BlockSpec(memory_space=pltpu.SMEM),
                  pl.BlockSpec((1, SWA_HEADS, seq, SWA_HEAD_DIM), lambda b: (b, 0, 0, 0)),
                  pl.BlockSpec((seq, kvw), lambda b: (b, 0)),
                  _resident(bias.shape)],
        out_specs=pl.BlockSpec((seq, qw), lambda b: (b, 0)),
        out_shape=jax.ShapeDtypeStruct((batch * seq, qw), BF16),
        scratch_shapes=[pltpu.VMEM((WINDOW + seq, SWA_KV_HEADS * SWA_HEAD_DIM), BF16),
                        pltpu.VMEM((1 + seq // WINDOW, SWA_KV_HEADS, SWA_HEAD_DIM + SWA_ONES_ROWS, WINDOW), BF16)],
        compiler_params=_params(1),
        name="swa",
    )(sinks, qa, kva, bias)


def _mla_kernel(q_ref, k_ref, v_ref, o_ref, vt_ref, *, tq):
    heads, seq, dv = v_ref.shape[1], v_ref.shape[2], v_ref.shape[3]
    n_tiles = seq // tq
    key = lax.broadcasted_iota(jnp.int32, (tq, tq), 0)
    qry = lax.broadcasted_iota(jnp.int32, (tq, tq), 1)
    causal = key <= qry
    for hd in range(heads):
        for c in range(seq // WINDOW):
            v_blk = v_ref[0, hd, c * WINDOW:(c + 1) * WINDOW, :]
            vt_ref[hd, 0:dv, c * WINDOW:(c + 1) * WINDOW] = v_blk.astype(F32).T.astype(BF16)
        vt_ref[hd, dv:, :] = jnp.ones((MLA_ONES_ROWS, seq), BF16)
    sts, mxs, pts, accs = {}, {}, {}, {}

    def scores(hd, i, j):
        q = q_ref[0, hd, i * tq:(i + 1) * tq, :]
        s = lax.dot_general(k_ref[0, hd, j * tq:(j + 1) * tq, :], q, NT_DIMS, preferred_element_type=F32)
        sts[hd, i, j] = jnp.where(causal, s, MASK_VALUE) if j == i else s

    def col_max(hd, i, j):
        m = jnp.max(sts[hd, i, j], axis=0, keepdims=True)
        mxs[hd, i] = m if j == 0 else jnp.maximum(mxs[hd, i], m)

    def probs(hd, i, j):
        pts[hd, i, j] = jnp.exp2(sts.pop((hd, i, j)) - mxs[hd, i]).astype(BF16)

    def values(hd, i, j):
        part = _dot(vt_ref[hd, :, j * tq:(j + 1) * tq], pts.pop((hd, i, j)))
        accs[hd, i] = part if j == 0 else accs[hd, i] + part
        if j == i:
            ot = accs.pop((hd, i))
            o = ot[0:dv, :] * (1.0 / ot[dv:dv + 1, :])
            o_ref[i * tq:(i + 1) * tq, hd * dv:(hd + 1) * dv] = o.T.astype(BF16)

    stages = (scores, col_max, probs, values)
    units = [(hd, i) for hd in range(heads) for i in range(n_tiles)]
    for t in range(len(units) + len(stages) - 1):
        for j in range(n_tiles):
            for lag, stage in enumerate(stages):
                if 0 <= t - lag < len(units):
                    hd, i = units[t - lag]
                    if j <= i:
                        stage(hd, i, j)


def _mla(qm, km, vm, *, tq=256):
    batch, heads, seq, qk = qm.shape
    dv = vm.shape[-1]
    return pl.pallas_call(
        functools.partial(_mla_kernel, tq=tq),
        grid=(batch,),
        in_specs=[pl.BlockSpec((1, heads, seq, qk), lambda b: (b, 0, 0, 0)),
                  pl.BlockSpec((1, heads, seq, qk), lambda b: (b, 0, 0, 0)),
                  pl.BlockSpec((1, heads, seq, dv), lambda b: (b, 0, 0, 0))],
        out_specs=pl.BlockSpec((seq, heads * dv), lambda b: (b, 0)),
        out_shape=jax.ShapeDtypeStruct((batch * seq, heads * dv), BF16),
        scratch_shapes=[pltpu.VMEM((heads, dv + MLA_ONES_ROWS, seq), BF16)],
        compiler_params=_params(1),
        name="mla",
    )(qm, km, vm)


def kernel(x, c, w_mod, b_mod, norm_ffn1, ffn1_gate, ffn1_up, ffn1_down, norm_mix, w_in, q_norm, kv_norm, w_uq, w_ukv, sinks, w_o, norm_ffn2, ffn2_gate, ffn2_up, ffn2_down, rel_bias, norm_final):
    batch, seq, d = x.shape
    depth = w_mod.shape[0]
    x2d = x.reshape(batch * seq, d)
    bias = _swa_bias(rel_bias)
    for l in range(depth):
        mod3 = _mod(c, w_mod[l], b_mod[l])
        x2d = _ffn(x2d, mod3, norm_ffn1[l], ffn1_gate[l], ffn1_up[l], ffn1_down[l], mod_row=0, seq=seq,
                   stream_first_tile=True)
        qa, kva, qm, km, vm = _mix_proj(x2d, mod3, norm_mix[l], w_in[l], q_norm[l], kv_norm[l],
                                        w_uq[l], w_ukv[l], batch=batch, seq=seq)
        out_a = _swa(qa, kva, bias, sinks[l], batch=batch, seq=seq)
        out_b = _mla(qm, km, vm)
        last = l == depth - 1
        x2d = _ffn(x2d, mod3, norm_ffn2[l], ffn2_gate[l], ffn2_up[l], ffn2_down[l], mod_row=6, seq=seq,
                   stream_first_tile=False, mix=(out_a, out_b, w_o[l].astype(BF16)),
                   final_gain=norm_final if last else None)
    return x2d.reshape(batch, seq, d)
```

```python
import functools
import math

import jax
import jax.numpy as jnp
import numpy as np
from jax import lax
from jax.experimental import pallas as pl
from jax.experimental.pallas import tpu as pltpu

EPS = 1e-6
FFN_RES_WEIGHT = 0.5
N_MOD = 9

SWA_HEADS = 8
SWA_KV_HEADS = 2
SWA_HEAD_DIM = 64
SWA_GROUP = SWA_HEADS // SWA_KV_HEADS
WINDOW = 128

MLA_HEADS = 4
MLA_Q_RANK = 256
MLA_KV_RANK = 128
MLA_NOPE = 128
MLA_ROPE = 64
MLA_V = 128
MLA_QK = MLA_NOPE + MLA_ROPE
MLA_QK_PAD = 256
SWA_Q_PAD = 128
ROPE_THETA = 10000.0

NUM_BUCKETS = 32
MAX_DISTANCE = 128

SWA_ONES_ROWS = 16
MLA_ONES_ROWS = 16
LOG2E = math.log2(math.e)
MASK_VALUE = -1e30
V7X_VMEM_LIMIT_BYTES = 56 * 1024 * 1024

F32 = jnp.float32
BF16 = jnp.bfloat16
NT_DIMS = (((1,), (1,)), ((), ()))


def _params(n_grid_dims):
    return pltpu.CompilerParams(dimension_semantics=("arbitrary",) * n_grid_dims,
                                vmem_limit_bytes=V7X_VMEM_LIMIT_BYTES)


def _resident(shape):
    return pl.BlockSpec(shape, lambda *_: (0,) * len(shape), pipeline_mode=pl.Buffered(1))


def _rms_norm(x, gain):
    ms = jnp.mean(x * x, axis=-1, keepdims=True)
    return x * lax.rsqrt(ms + EPS) * gain


def _silu(x):
    return x * (1.0 / (1.0 + jnp.exp(-x)))


def _dot(a, b):
    return jnp.dot(a, b, preferred_element_type=F32)


def _mod_kernel(c_ref, w_ref, b_ref, o_ref):
    c_act = _silu(c_ref[...]).astype(BF16)
    res = _dot(c_act, w_ref[...].astype(BF16)) + b_ref[...]
    d = o_ref.shape[2]
    for r in range(o_ref.shape[0]):
        o_ref[r] = res[:, r * d:(r + 1) * d]


def _mod(c, w_mod, b_mod, *, rows_per_step=3):
    batch, d = c.shape
    n = w_mod.shape[1]
    tn = rows_per_step * d
    return pl.pallas_call(
        _mod_kernel,
        grid=(n // tn,),
        in_specs=[pl.BlockSpec((batch, d), lambda j: (0, 0)),
                  pl.BlockSpec((d, tn), lambda j: (0, j)),
                  pl.BlockSpec((1, tn), lambda j: (0, j))],
        out_specs=pl.BlockSpec((rows_per_step, batch, d), lambda j: (j, 0, 0)),
        out_shape=jax.ShapeDtypeStruct((n // d, batch, d), F32),
        compiler_params=_params(1),
        name="mod",
    )(c, w_mod, b_mod.reshape(1, n))


def _mod_row(mod_ref, row, b):
    return mod_ref[row, pl.ds(b, 1), :]


def _ffn_kernel(*refs, mod_row, tiles_per_seq, mix_residual, final_norm, stream_first_tile):
    refs = list(refs)
    x_ref, mod_ref = refs[:2]
    del refs[:2]
    if mix_residual:
        oa_ref, ob_ref, wo_ref = refs[:3]
        del refs[:3]
    g_ref, wg_hbm, wu_hbm, wd_hbm = refs[:4]
    del refs[:4]
    if final_norm:
        gf_ref = refs.pop(0)
    (o_ref, wg_s, wu_s, wd_s, h_even, h_odd, xres_s, a_s,
     stage_gu, stage_d, sem_gu, sem_d) = refs
    n_w, _, fw = wg_s.shape
    tm = h_even.shape[0]
    n_sub = x_ref.shape[0] // tm
    lead_steps = 0 if stream_first_tile else 1
    t = pl.program_id(0) - lead_steps
    gu_slots, d_slots, d_rows = stage_gu.shape[0], stage_d.shape[0], stage_d.shape[1]
    n_d = wd_hbm.shape[0] // d_rows

    def weight_stage():
        gu_rows = stage_gu.shape[1]
        gu_chunks = [(src, dst, k) for k in range(wg_hbm.shape[0] // gu_rows)
                     for src, dst in ((wg_hbm, wg_s), (wu_hbm, wu_s))]

        def gu_copy(n):
            src, _, k = gu_chunks[n]
            return pltpu.make_async_copy(src.at[pl.ds(k * gu_rows, gu_rows), :], stage_gu.at[n % gu_slots],
                                         sem_gu.at[n % gu_slots])

        for n in range(gu_slots):
            gu_copy(n).start()
        for m in range(d_slots):
            d_copy(m).start()
        for n in range(len(gu_chunks)):
            gu_copy(n).wait()
            _, dst, k = gu_chunks[n]
            for c in range(n_w):
                dst[c, k * gu_rows:(k + 1) * gu_rows, :] = stage_gu[n % gu_slots, :, c * fw:(c + 1) * fw].astype(BF16)
            if n + gu_slots < len(gu_chunks):
                gu_copy(n + gu_slots).start()
            if n % 2 == 1:
                stream_down(n // 2)

    def gu_copies(c):
        cols = pl.ds(c * fw, fw)
        return [pltpu.make_async_copy(src.at[:, cols], stage_gu.at[c % gu_slots, k], sem_gu.at[c % gu_slots, k])
                for k, src in enumerate((wg_hbm, wu_hbm))]

    def d_copy(m):
        return pltpu.make_async_copy(wd_hbm.at[pl.ds(m * d_rows, d_rows), :], stage_d.at[m % d_slots],
                                     sem_d.at[m % d_slots])

    def stream_begin():
        for c in range(gu_slots):
            for cp in gu_copies(c):
                cp.start()
        for m in range(d_slots):
            d_copy(m).start()

    def stream_gate_up(c):
        for cp in gu_copies(c):
            cp.wait()
        wg_s[c] = stage_gu[c % gu_slots, 0].astype(BF16)
        wu_s[c] = stage_gu[c % gu_slots, 1].astype(BF16)
        if c + gu_slots < n_w:
            for cp in gu_copies(c + gu_slots):
                cp.start()

    def stream_down(m):
        d_copy(m).wait()
        wd_s[m * d_rows:(m + 1) * d_rows, :] = stage_d[m % d_slots].astype(BF16)
        if m + d_slots < n_d:
            d_copy(m + d_slots).start()

    def batch_of(sub):
        return (t * n_sub + sub) // tiles_per_seq

    def norm_stage(sub, h_next):
        rows = pl.ds(sub * tm, tm)
        b = batch_of(sub)
        x = x_ref[rows, :]
        if mix_residual:
            wa = oa_ref.shape[1]
            mix = _dot(oa_ref[rows, :], wo_ref[0:wa, :]) + _dot(ob_ref[rows, :], wo_ref[wa:, :])
            x = x + _mod_row(mod_ref, mod_row - 1, b) * mix
            xres_s[rows, :] = x
        gain = g_ref[...] * (1.0 + _mod_row(mod_ref, mod_row + 1, b))
        ms = jnp.mean(x * x, axis=-1, keepdims=True)
        h_next[...] = (x * lax.rsqrt(ms + EPS) * gain + _mod_row(mod_ref, mod_row, b)).astype(BF16)

    def matmul_stage(sub, h_cur, stream_weights):
        rows = pl.ds(sub * tm, tm)
        if stream_weights:
            stream_begin()
        for c in range(n_w):
            if stream_weights:
                stream_gate_up(c)
            g = _dot(h_cur[...], wg_s[c])
            u = _dot(h_cur[...], wu_s[c])
            a_s[:, c * fw:(c + 1) * fw] = (_silu(g) * u).astype(BF16)
            if stream_weights and c < n_d:
                stream_down(c)
        if stream_weights:
            for m in range(n_w, n_d):
                stream_down(m)
        y = _dot(a_s[...], wd_s[...])
        x = xres_s[rows, :] if mix_residual else x_ref[rows, :]
        out = x + FFN_RES_WEIGHT * _mod_row(mod_ref, mod_row + 2, batch_of(sub)) * y
        if final_norm:
            out = _rms_norm(out, gf_ref[...])
        o_ref[rows, :] = out

    def block(first):
        bufs = (h_even, h_odd)
        norm_stage(0, bufs[0])
        for sub in range(n_sub):
            if sub + 1 < n_sub:
                norm_stage(sub + 1, bufs[(sub + 1) % 2])
            matmul_stage(sub, bufs[sub % 2], stream_weights=first and sub == 0)

    if stream_first_tile:
        @pl.when(t == 0)
        def _():
            block(first=True)

        @pl.when(t > 0)
        def _():
            block(first=False)
    else:
        @pl.when(t == -1)
        def _():
            weight_stage()

        @pl.when(t >= 0)
        def _():
            block(first=False)


def _ffn(x2d, mod3, gain, wg, wu, wd, *, mod_row, seq, stream_first_tile, mix=None, final_gain=None, tm=512,
         tiles_per_step=2, fw=256, stage_chunks=8):
    m, d = x2d.shape
    f = wg.shape[1]
    bm = tm * tiles_per_step
    assert f % fw == 0 and seq % tm == 0 and m % bm == 0 and d % stage_chunks == 0 and f % stage_chunks == 0
    n_w, n_blocks, tiles_per_seq = f // fw, m // bm, seq // tm
    final_norm = final_gain is not None
    lead_steps = 0 if stream_first_tile else 1
    if stream_first_tile:
        stage = [pltpu.VMEM((2, 2, d, fw), F32), pltpu.VMEM((1, f // n_w, d), F32),
                 pltpu.SemaphoreType.DMA((2, 2)), pltpu.SemaphoreType.DMA((1,))]
    else:
        stage = [pltpu.VMEM((2, d // stage_chunks, f), F32), pltpu.VMEM((2, f // stage_chunks, d), F32),
                 pltpu.SemaphoreType.DMA((2,)), pltpu.SemaphoreType.DMA((2,))]

    def block(s):
        return (jnp.maximum(s - lead_steps, 0), 0)

    in_specs = [pl.BlockSpec((bm, d), block), _resident(mod3.shape)]
    args = [x2d, mod3]
    if mix is not None:
        out_a, out_b, w_o = mix
        wa, wb = out_a.shape[1], out_b.shape[1]
        in_specs += [pl.BlockSpec((bm, wa), block), pl.BlockSpec((bm, wb), block), _resident((wa + wb, d))]
        args += [out_a, out_b, w_o]
    hbm = pl.BlockSpec(memory_space=pl.ANY)
    in_specs += [_resident((1, d)), hbm, hbm, hbm]
    args += [gain.reshape(1, d), wg, wu, wd]
    if final_norm:
        in_specs.append(_resident((1, d)))
        args.append(final_gain.reshape(1, d))
    xres_rows = bm if mix is not None else 8
    return pl.pallas_call(
        functools.partial(_ffn_kernel, mod_row=mod_row, tiles_per_seq=tiles_per_seq, mix_residual=mix is not None,
                          final_norm=final_norm, stream_first_tile=stream_first_tile),
        grid=(n_blocks + lead_steps,),
        in_specs=in_specs,
        out_specs=pl.BlockSpec((bm, d), block),
        out_shape=jax.ShapeDtypeStruct((m, d), F32),
        scratch_shapes=[pltpu.VMEM((n_w, d, fw), BF16), pltpu.VMEM((n_w, d, fw), BF16), pltpu.VMEM((f, d), BF16),
                        pltpu.VMEM((tm, d), BF16), pltpu.VMEM((tm, d), BF16),
                        pltpu.VMEM((xres_rows, d), F32), pltpu.VMEM((tm, f), BF16)] + stage,
        compiler_params=_params(1),
        name="ffn_final" if final_norm else "ffn",
    )(*args)


_QA0, _KVA0, _QLAT0, _KVLAT0, _KR0, _KRS0, _PROJ_W = 0, 512, 768, 1024, 1152, 1216, 1280
_UQ_ROPE0 = MLA_HEADS * MLA_NOPE
_UQ_ROPES0 = _UQ_ROPE0 + MLA_HEADS * MLA_ROPE


def _mix_proj_kernel(x_ref, mod_ref, g_ref, win_ref, qn_ref, kvn_ref, wuq_ref, wukv_ref, cos_ref, sin_ref,
                     qa_ref, kva_ref, qm_ref, km_ref, vm_ref, win_s, *, blocks_per_seq, sub_rows):
    @pl.when(pl.program_id(0) == 0)
    def _():
        w = win_ref[...]
        half = MLA_ROPE // 2
        win_s[:, 0:_KRS0] = w.astype(BF16)
        win_s[:, _KRS0:_KRS0 + half] = w[:, _KR0 + half:_KRS0].astype(BF16)
        win_s[:, _KRS0 + half:_PROJ_W] = w[:, _KR0:_KR0 + half].astype(BF16)

    b = pl.program_id(0) // blocks_per_seq
    shift = _mod_row(mod_ref, 3, b)
    scale = _mod_row(mod_ref, 4, b)
    q_scale = MLA_QK ** -0.5 * LOG2E
    for r0 in range(0, x_ref.shape[0], sub_rows):
        rows = slice(r0, r0 + sub_rows)
        h = (_rms_norm(x_ref[rows, :], g_ref[...]) * (1.0 + scale) + shift).astype(BF16)
        proj = _dot(h, win_s[...])
        for hd in range(SWA_HEADS):
            qa_ref[0, hd, rows, 0:SWA_HEAD_DIM] = (proj[:, _QA0 + hd * SWA_HEAD_DIM:_QA0 + (hd + 1) * SWA_HEAD_DIM]
                                                   * (SWA_HEAD_DIM ** -0.5 * LOG2E)).astype(BF16)
            qa_ref[0, hd, rows, SWA_HEAD_DIM:] = jnp.zeros((sub_rows, SWA_Q_PAD - SWA_HEAD_DIM), BF16)
        kva_ref[rows, :] = proj[:, _KVA0:_QLAT0].astype(BF16)

        q_lat = _rms_norm(proj[:, _QLAT0:_KVLAT0], qn_ref[...]).astype(BF16)
        kv_lat = _rms_norm(proj[:, _KVLAT0:_KR0], kvn_ref[...]).astype(BF16)
        q_all = _dot(q_lat, wuq_ref[...])
        kv_all = _dot(kv_lat, wukv_ref[...])

        cos = cos_ref[rows, :]
        sin = sin_ref[rows, :]
        k_rope = (proj[:, _KR0:_KRS0] * cos + proj[:, _KRS0:_PROJ_W] * sin).astype(BF16)
        for hd in range(MLA_HEADS):
            q_nope = q_all[:, hd * MLA_NOPE:(hd + 1) * MLA_NOPE]
            q_rope = (q_all[:, _UQ_ROPE0 + hd * MLA_ROPE:_UQ_ROPE0 + (hd + 1) * MLA_ROPE] * cos
                      + q_all[:, _UQ_ROPES0 + hd * MLA_ROPE:_UQ_ROPES0 + (hd + 1) * MLA_ROPE] * sin)
            qm_ref[0, hd, rows, 0:MLA_NOPE] = (q_nope * q_scale).astype(BF16)
            qm_ref[0, hd, rows, MLA_NOPE:MLA_QK] = (q_rope * q_scale).astype(BF16)
            kv0 = hd * (MLA_NOPE + MLA_V)
            km_ref[0, hd, rows, 0:MLA_NOPE] = kv_all[:, kv0:kv0 + MLA_NOPE].astype(BF16)
            km_ref[0, hd, rows, MLA_NOPE:MLA_QK] = k_rope
            qm_ref[0, hd, rows, MLA_QK:] = jnp.zeros((sub_rows, MLA_QK_PAD - MLA_QK), BF16)
            km_ref[0, hd, rows, MLA_QK:] = jnp.zeros((sub_rows, MLA_QK_PAD - MLA_QK), BF16)
            vm_ref[0, hd, rows, :] = kv_all[:, kv0 + MLA_NOPE:kv0 + MLA_NOPE + MLA_V].astype(BF16)


def _rope_tables(seq):
    inv = np.float32(ROPE_THETA) ** (-np.arange(0, MLA_ROPE, 2, dtype=np.float32) / np.float32(MLA_ROPE))
    ang = np.arange(seq, dtype=np.float32)[:, None] * inv[None, :].astype(np.float32)
    cos, sin = np.cos(ang).astype(np.float32), np.sin(ang).astype(np.float32)
    return np.concatenate([cos, cos], axis=-1), np.concatenate([-sin, sin], axis=-1)


def _swap_halves(w):
    half = w.shape[-1] // 2
    return jnp.concatenate([w[..., half:], w[..., :half]], axis=-1)


def _mix_proj(x2d, mod3, gain, w_in, q_norm, kv_norm, w_uq, w_ukv, *, batch, seq, tm=1024, sub_rows=512):
    m, d = x2d.shape
    tiles_per_seq = seq // tm
    assert seq % tm == 0 and tm % sub_rows == 0
    w_uq_h = w_uq.reshape(MLA_Q_RANK, MLA_HEADS, MLA_QK)
    uq_rope = w_uq_h[:, :, MLA_NOPE:]
    w_uq_r = jnp.concatenate([w_uq_h[:, :, :MLA_NOPE].reshape(MLA_Q_RANK, -1),
                              uq_rope.reshape(MLA_Q_RANK, -1),
                              _swap_halves(uq_rope).reshape(MLA_Q_RANK, -1)], axis=1).astype(BF16)
    w_ukv_b = w_ukv.astype(BF16)
    cos2, sin2 = (jnp.asarray(tab) for tab in _rope_tables(seq))

    def head_spec(width):
        return pl.BlockSpec((1, MLA_HEADS, tm, width),
                            lambda i: (i // tiles_per_seq, 0, i % tiles_per_seq, 0))

    def head_shape(width):
        return jax.ShapeDtypeStruct((batch, MLA_HEADS, seq, width), BF16)

    return pl.pallas_call(
        functools.partial(_mix_proj_kernel, blocks_per_seq=tiles_per_seq, sub_rows=sub_rows),
        grid=(m // tm,),
        in_specs=[pl.BlockSpec((tm, d), lambda i: (i, 0)),
                  _resident(mod3.shape),
                  _resident((1, d)),
                  _resident(w_in.shape),
                  _resident((1, MLA_Q_RANK)), _resident((1, MLA_KV_RANK)),
                  _resident(w_uq_r.shape), _resident(w_ukv_b.shape),
                  pl.BlockSpec((tm, MLA_ROPE), lambda i: (i % tiles_per_seq, 0)),
                  pl.BlockSpec((tm, MLA_ROPE), lambda i: (i % tiles_per_seq, 0))],
        out_specs=[pl.BlockSpec((1, SWA_HEADS, tm, SWA_Q_PAD),
                                lambda i: (i // tiles_per_seq, 0, i % tiles_per_seq, 0)),
                   pl.BlockSpec((tm, 2 * SWA_KV_HEADS * SWA_HEAD_DIM), lambda i: (i, 0)),
                   head_spec(MLA_QK_PAD), head_spec(MLA_QK_PAD), head_spec(MLA_V)],
        out_shape=[jax.ShapeDtypeStruct((batch, SWA_HEADS, seq, SWA_Q_PAD), BF16),
                   jax.ShapeDtypeStruct((m, 2 * SWA_KV_HEADS * SWA_HEAD_DIM), BF16),
                   head_shape(MLA_QK_PAD), head_shape(MLA_QK_PAD), head_shape(MLA_V)],
        scratch_shapes=[pltpu.VMEM((d, _PROJ_W), BF16)],
        compiler_params=_params(1),
        name="mix_proj",
    )(x2d, mod3, gain.reshape(1, d), w_in, q_norm.reshape(1, -1), kv_norm.reshape(1, -1),
      w_uq_r, w_ukv_b, cos2, sin2)


def _t5_bucket_table():
    qi = np.arange(WINDOW)[:, None]
    kj = np.arange(2 * WINDOW)[None, :]
    dist = qi + WINDOW - kj
    max_exact = NUM_BUCKETS // 2
    n = np.maximum(dist, 0)
    nf = np.maximum(n, 1).astype(np.float32)
    large = max_exact + (np.log(nf / np.float32(max_exact)) / np.float32(math.log(MAX_DISTANCE / max_exact))
                         * np.float32(NUM_BUCKETS - max_exact)).astype(np.int32)
    large = np.minimum(large, NUM_BUCKETS - 1)
    bucket = np.where(n < max_exact, n, large)
    band = (dist >= 0) & (dist < WINDOW)
    return np.where(band, bucket, -1).astype(np.int32)


def _swa_bias_kernel(rel_ref, bucket_ref, o_ref):
    bucket = bucket_ref[...]
    first_block_ok = lax.broadcasted_iota(jnp.int32, bucket.shape, 0) >= WINDOW
    for hd in range(SWA_HEADS):
        acc = jnp.full(bucket.shape, MASK_VALUE, F32)
        for b in range(NUM_BUCKETS):
            acc = jnp.where(bucket == b, rel_ref[b, hd] * LOG2E, acc)
        g, j = divmod(hd, SWA_GROUP)
        o_ref[0, g, :, j * WINDOW:(j + 1) * WINDOW] = jnp.where(first_block_ok, acc, MASK_VALUE)
        o_ref[1, g, :, j * WINDOW:(j + 1) * WINDOW] = acc


def _swa_bias(rel_bias):
    shape = (2, SWA_KV_HEADS, 2 * WINDOW, SWA_GROUP * WINDOW)
    return pl.pallas_call(
        _swa_bias_kernel,
        in_specs=[pl.BlockSpec(memory_space=pltpu.SMEM),
                  pl.BlockSpec((2 * WINDOW, WINDOW), lambda: (0, 0))],
        out_specs=pl.BlockSpec(shape, lambda: (0, 0, 0, 0)),
        out_shape=jax.ShapeDtypeStruct(shape, F32),
        name="swa_bias",
    )(rel_bias, jnp.asarray(_t5_bucket_table().T))


def _swa_kernel(sink_ref, q_ref, kv_ref, bias_ref, o_ref, kp_ref, vt_ref, *, blocks_per_iter):
    seq = kv_ref.shape[0]
    n_blocks = seq // WINDOW
    kdim = SWA_KV_HEADS * SWA_HEAD_DIM
    kp_ref[0:WINDOW, :] = jnp.zeros((WINDOW, kdim), BF16)
    kp_ref[WINDOW:, :] = kv_ref[:, 0:kdim]
    ones_rows = jnp.ones((SWA_ONES_ROWS, WINDOW), BF16)
    for g in range(SWA_KV_HEADS):
        vt_ref[0, g] = jnp.concatenate([jnp.zeros((SWA_HEAD_DIM, WINDOW), BF16), ones_rows], axis=0)
    for n in range(n_blocks):
        for g in range(SWA_KV_HEADS):
            v_blk = kv_ref[n * WINDOW:(n + 1) * WINDOW, kdim + g * SWA_HEAD_DIM:kdim + (g + 1) * SWA_HEAD_DIM]
            vt_ref[n + 1, g] = jnp.concatenate([v_blk.astype(F32).T.astype(BF16), ones_rows], axis=0)
    sink_rows = [[jnp.full((1, WINDOW), sink_ref[g * SWA_GROUP + j] * LOG2E, F32) for j in range(SWA_GROUP)]
                 for g in range(SWA_KV_HEADS)]

    def body(it, carry):
        units = [(b, g) for b in range(blocks_per_iter) for g in range(SWA_KV_HEADS)]
        blocks = [it * blocks_per_iter + b for b in range(blocks_per_iter)]
        q_starts = [pl.multiple_of(n * WINDOW, WINDOW) for n in blocks]
        tables = [jnp.minimum(n, 1) for n in blocks]
        sts, pts = {}, {}

        def scores(u):
            b, g = units[u]
            k = kp_ref[pl.ds(q_starts[b], 2 * WINDOW), g * SWA_HEAD_DIM:(g + 1) * SWA_HEAD_DIM]
            sts[u] = []
            for j in range(0, SWA_GROUP, 2):
                hd = g * SWA_GROUP + j
                q = jnp.concatenate([q_ref[0, hd, pl.ds(q_starts[b], WINDOW), 0:SWA_HEAD_DIM],
                                     q_ref[0, hd + 1, pl.ds(q_starts[b], WINDOW), 0:SWA_HEAD_DIM]], axis=0)
                st = lax.dot_general(k, q, NT_DIMS, preferred_element_type=F32)
                sts[u] += [st[:, 0:WINDOW], st[:, WINDOW:2 * WINDOW]]

        def softmax(u):
            b, g = units[u]
            pts[u] = []
            for j in range(SWA_GROUP):
                st = sts[u][j] + bias_ref[tables[b], g, :, j * WINDOW:(j + 1) * WINDOW]
                mx = jnp.maximum(jnp.max(st, axis=0, keepdims=True), sink_rows[g][j])
                pts[u].append((jnp.exp2(st - mx).astype(BF16), jnp.exp2(sink_rows[g][j] - mx)))

        def values(u):
            b, g = units[u]
            vt = jnp.concatenate([vt_ref[blocks[b], g], vt_ref[blocks[b] + 1, g]], axis=1)
            for j in range(0, SWA_GROUP, 2):
                pt = jnp.concatenate([pts[u][j][0], pts[u][j + 1][0]], axis=1)
                ot = _dot(vt, pt)
                halves = []
                for jj in range(2):
                    lanes = slice(jj * WINDOW, (jj + 1) * WINDOW)
                    denom = ot[SWA_HEAD_DIM:SWA_HEAD_DIM + 1, lanes] + pts[u][j + jj][1]
                    halves.append(ot[0:SWA_HEAD_DIM, lanes] * (1.0 / denom))
                pair = jnp.concatenate(halves, axis=0).T
                lane0 = (g * SWA_GROUP + j) * SWA_HEAD_DIM
                o_ref[pl.ds(q_starts[b], WINDOW), lane0:lane0 + 2 * SWA_HEAD_DIM] = pair.astype(BF16)

        for t in range(len(units) + 2):
            if t < len(units):
                scores(t)
            if 0 <= t - 1 < len(units):
                softmax(t - 1)
            if 0 <= t - 2 < len(units):
                values(t - 2)
        return carry

    lax.fori_loop(0, n_blocks // blocks_per_iter, body, 0)


def _swa(qa, kva, bias, sinks, *, batch, seq, blocks_per_iter=16):
    kvw = kva.shape[1]
    qw = SWA_HEADS * SWA_HEAD_DIM
    assert (seq // WINDOW) % blocks_per_iter == 0
    return pl.pallas_call(
        functools.partial(_swa_kernel, blocks_per_iter=blocks_per_iter),
        grid=(batch,),
        in_specs=[pl.BlockSpec(memory_space=pltpu.SMEM),
                  pl.BlockSpec((1, SWA_HEADS, seq, SWA_Q_PAD), lambda b: (b, 0, 0, 0)),
                  pl.BlockSpec((seq, kvw), lambda b: (b, 0)),
                  _resident(bias.shape)],
        out_specs=pl.BlockSpec((seq, qw), lambda b: (b, 0)),
        out_shape=jax.ShapeDtypeStruct((batch * seq, qw), BF16),
        scratch_shapes=[pltpu.VMEM((WINDOW + seq, SWA_KV_HEADS * SWA_HEAD_DIM), BF16),
                        pltpu.VMEM((1 + seq // WINDOW, SWA_KV_HEADS, SWA_HEAD_DIM + SWA_ONES_ROWS, WINDOW), BF16)],
        compiler_params=_params(1),
        name="swa",
    )(sinks, qa, kva, bias)


def _mla_kernel(q_ref, k_ref, v_ref, o_ref, vt_ref, *, tq):
    heads, seq, dv = v_ref.shape[1], v_ref.shape[2], v_ref.shape[3]
    n_tiles = seq // tq
    key = lax.broadcasted_iota(jnp.int32, (tq, tq), 0)
    qry = lax.broadcasted_iota(jnp.int32, (tq, tq), 1)
    causal = key <= qry
    for hd in range(heads):
        for c in range(seq // WINDOW):
            v_blk = v_ref[0, hd, c * WINDOW:(c + 1) * WINDOW, :]
            vt_ref[hd, 0:dv, c * WINDOW:(c + 1) * WINDOW] = v_blk.astype(F32).T.astype(BF16)
        vt_ref[hd, dv:, :] = jnp.ones((MLA_ONES_ROWS, seq), BF16)
    sts, mxs, pts, accs = {}, {}, {}, {}

    def scores(hd, i, j):
        q = q_ref[0, hd, i * tq:(i + 1) * tq, :]
        s = lax.dot_general(k_ref[0, hd, j * tq:(j + 1) * tq, :], q, NT_DIMS, preferred_element_type=F32)
        sts[hd, i, j] = jnp.where(causal, s, MASK_VALUE) if j == i else s

    def col_max(hd, i, j):
        m = jnp.max(sts[hd, i, j], axis=0, keepdims=True)
        mxs[hd, i] = m if j == 0 else jnp.maximum(mxs[hd, i], m)

    def probs(hd, i, j):
        pts[hd, i, j] = jnp.exp2(sts.pop((hd, i, j)) - mxs[hd, i]).astype(BF16)

    def values(hd, i, j):
        part = _dot(vt_ref[hd, :, j * tq:(j + 1) * tq], pts.pop((hd, i, j)))
        accs[hd, i] = part if j == 0 else accs[hd, i] + part
        if j == i:
            ot = accs.pop((hd, i))
            o = ot[0:dv, :] * (1.0 / ot[dv:dv + 1, :])
            o_ref[i * tq:(i + 1) * tq, hd * dv:(hd + 1) * dv] = o.T.astype(BF16)

    stages = (scores, col_max, probs, values)
    units = [(hd, i) for hd in range(heads) for i in range(n_tiles)]
    for t in range(len(units) + len(stages) - 1):
        for j in range(n_tiles):
            for lag, stage in enumerate(stages):
                if 0 <= t - lag < len(units):
                    hd, i = units[t - lag]
                    if j <= i:
                        stage(hd, i, j)


def _mla(qm, km, vm, *, tq=256):
    batch, heads, seq, qk = qm.shape
    dv = vm.shape[-1]
    return pl.pallas_call(
        functools.partial(_mla_kernel, tq=tq),
        grid=(batch,),
        in_specs=[pl.BlockSpec((1, heads, seq, qk), lambda b: (b, 0, 0, 0)),
                  pl.BlockSpec((1, heads, seq, qk), lambda b: (b, 0, 0, 0)),
                  pl.BlockSpec((1, heads, seq, dv), lambda b: (b, 0, 0, 0))],
        out_specs=pl.BlockSpec((seq, heads * dv), lambda b: (b, 0)),
        out_shape=jax.ShapeDtypeStruct((batch * seq, heads * dv), BF16),
        scratch_shapes=[pltpu.VMEM((heads, dv + MLA_ONES_ROWS, seq), BF16)],
        compiler_params=_params(1),
        name="mla",
    )(qm, km, vm)


def kernel(x, c, w_mod, b_mod, norm_ffn1, ffn1_gate, ffn1_up, ffn1_down, norm_mix, w_in, q_norm, kv_norm, w_uq, w_ukv, sinks, w_o, norm_ffn2, ffn2_gate, ffn2_up, ffn2_down, rel_bias, norm_final):
    batch, seq, d = x.shape
    depth = w_mod.shape[0]
    x2d = x.reshape(batch * seq, d)
    bias = _swa_bias(rel_bias)
    for l in range(depth):
        mod3 = _mod(c, w_mod[l], b_mod[l])
        x2d = _ffn(x2d, mod3, norm_ffn1[l], ffn1_gate[l], ffn1_up[l], ffn1_down[l], mod_row=0, seq=seq,
                   stream_first_tile=True)
        qa, kva, qm, km, vm = _mix_proj(x2d, mod3, norm_mix[l], w_in[l], q_norm[l], kv_norm[l],
                                        w_uq[l], w_ukv[l], batch=batch, seq=seq)
        out_a = _swa(qa, kva, bias, sinks[l], batch=batch, seq=seq)
        out_b = _mla(qm, km, vm)
        last = l == depth - 1
        x2d = _ffn(x2d, mod3, norm_ffn2[l], ffn2_gate[l], ffn2_up[l], ffn2_down[l], mod_row=6, seq=seq,
                   stream_first_tile=False, mix=(out_a, out_b, w_o[l].astype(BF16)),
                   final_gain=norm_final if last else None)
    return x2d.reshape(batch, seq, d)
```

```python
import functools
import math

import jax
import jax.numpy as jnp
import numpy as np
from jax import lax
from jax.experimental import pallas as pl
from jax.experimental.pallas import tpu as pltpu

EPS = 1e-6
FFN_RES_WEIGHT = 0.5
N_MOD = 9

SWA_HEADS = 8
SWA_KV_HEADS = 2
SWA_HEAD_DIM = 64
SWA_GROUP = SWA_HEADS // SWA_KV_HEADS
WINDOW = 128

MLA_HEADS = 4
MLA_Q_RANK = 256
MLA_KV_RANK = 128
MLA_NOPE = 128
MLA_ROPE = 64
MLA_V = 128
MLA_QK = MLA_NOPE + MLA_ROPE
ROPE_THETA = 10000.0

NUM_BUCKETS = 32
MAX_DISTANCE = 128

SWA_ONES_ROWS = 16
MLA_ONES_ROWS = 16
LOG2E = math.log2(math.e)
MASK_VALUE = -1e30
V7X_VMEM_LIMIT_BYTES = 56 * 1024 * 1024

F32 = jnp.float32
BF16 = jnp.bfloat16
NT_DIMS = (((1,), (1,)), ((), ()))


def _params(n_grid_dims):
    return pltpu.CompilerParams(dimension_semantics=("arbitrary",) * n_grid_dims,
                                vmem_limit_bytes=V7X_VMEM_LIMIT_BYTES)


def _resident(shape):
    return pl.BlockSpec(shape, lambda *_: (0,) * len(shape), pipeline_mode=pl.Buffered(1))


def _rms_norm(x, gain):
    ms = jnp.mean(x * x, axis=-1, keepdims=True)
    return x * lax.rsqrt(ms + EPS) * gain


def _silu(x):
    return x * (1.0 / (1.0 + jnp.exp(-x)))


def _dot(a, b):
    return jnp.dot(a, b, preferred_element_type=F32)


def _mod_kernel(c_ref, w_ref, b_ref, o_ref):
    c_act = _silu(c_ref[...]).astype(BF16)
    res = _dot(c_act, w_ref[...].astype(BF16)) + b_ref[...]
    d = o_ref.shape[2]
    for r in range(o_ref.shape[0]):
        o_ref[r] = res[:, r * d:(r + 1) * d]


def _mod(c, w_mod, b_mod, *, rows_per_step=3):
    batch, d = c.shape
    n = w_mod.shape[1]
    tn = rows_per_step * d
    return pl.pallas_call(
        _mod_kernel,
        grid=(n // tn,),
        in_specs=[pl.BlockSpec((batch, d), lambda j: (0, 0)),
                  pl.BlockSpec((d, tn), lambda j: (0, j)),
                  pl.BlockSpec((1, tn), lambda j: (0, j))],
        out_specs=pl.BlockSpec((rows_per_step, batch, d), lambda j: (j, 0, 0)),
        out_shape=jax.ShapeDtypeStruct((n // d, batch, d), F32),
        compiler_params=_params(1),
        name="mod",
    )(c, w_mod, b_mod.reshape(1, n))


def _mod_row(mod_ref, row, b):
    return mod_ref[row, pl.ds(b, 1), :]


def _ffn_kernel(*refs, mod_row, tiles_per_seq, mix_residual, final_norm, stream_first_tile):
    refs = list(refs)
    x_ref, mod_ref = refs[:2]
    del refs[:2]
    if mix_residual:
        oa_ref, ob_ref, wo_ref = refs[:3]
        del refs[:3]
    g_ref, wg_hbm, wu_hbm, wd_hbm = refs[:4]
    del refs[:4]
    if final_norm:
        gf_ref = refs.pop(0)
    (o_ref, wg_s, wu_s, wd_s, h_even, h_odd, xres_s, a_s,
     stage_gu, stage_d, sem_gu, sem_d) = refs
    n_w, _, fw = wg_s.shape
    tm = h_even.shape[0]
    n_sub = x_ref.shape[0] // tm
    lead_steps = 0 if stream_first_tile else 1
    t = pl.program_id(0) - lead_steps
    gu_slots, d_slots, d_rows = stage_gu.shape[0], stage_d.shape[0], stage_d.shape[1]
    n_d = wd_hbm.shape[0] // d_rows

    def weight_stage():
        gu_rows = stage_gu.shape[1]
        gu_chunks = [(src, dst, k) for k in range(wg_hbm.shape[0] // gu_rows)
                     for src, dst in ((wg_hbm, wg_s), (wu_hbm, wu_s))]

        def gu_copy(n):
            src, _, k = gu_chunks[n]
            return pltpu.make_async_copy(src.at[pl.ds(k * gu_rows, gu_rows), :], stage_gu.at[n % gu_slots],
                                         sem_gu.at[n % gu_slots])

        for n in range(gu_slots):
            gu_copy(n).start()
        for m in range(d_slots):
            d_copy(m).start()
        for n in range(len(gu_chunks)):
            gu_copy(n).wait()
            _, dst, k = gu_chunks[n]
            for c in range(n_w):
                dst[c, k * gu_rows:(k + 1) * gu_rows, :] = stage_gu[n % gu_slots, :, c * fw:(c + 1) * fw].astype(BF16)
            if n + gu_slots < len(gu_chunks):
                gu_copy(n + gu_slots).start()
            if n % 2 == 1:
                stream_down(n // 2)

    def gu_copies(c):
        cols = pl.ds(c * fw, fw)
        return [pltpu.make_async_copy(src.at[:, cols], stage_gu.at[c % gu_slots, k], sem_gu.at[c % gu_slots, k])
                for k, src in enumerate((wg_hbm, wu_hbm))]

    def d_copy(m):
        return pltpu.make_async_copy(wd_hbm.at[pl.ds(m * d_rows, d_rows), :], stage_d.at[m % d_slots],
                                     sem_d.at[m % d_slots])

    def stream_begin():
        for c in range(gu_slots):
            for cp in gu_copies(c):
                cp.start()
        for m in range(d_slots):
            d_copy(m).start()

    def stream_gate_up(c):
        for cp in gu_copies(c):
            cp.wait()
        wg_s[c] = stage_gu[c % gu_slots, 0].astype(BF16)
        wu_s[c] = stage_gu[c % gu_slots, 1].astype(BF16)
        if c + gu_slots < n_w:
            for cp in gu_copies(c + gu_slots):
                cp.start()

    def stream_down(m):
        d_copy(m).wait()
        wd_s[m * d_rows:(m + 1) * d_rows, :] = stage_d[m % d_slots].astype(BF16)
        if m + d_slots < n_d:
            d_copy(m + d_slots).start()

    def batch_of(sub):
        return (t * n_sub + sub) // tiles_per_seq

    def norm_stage(sub, h_next):
        rows = pl.ds(sub * tm, tm)
        b = batch_of(sub)
        x = x_ref[rows, :]
        if mix_residual:
            wa = oa_ref.shape[1]
            mix = _dot(oa_ref[rows, :], wo_ref[0:wa, :]) + _dot(ob_ref[rows, :], wo_ref[wa:, :])
            x = x + _mod_row(mod_ref, mod_row - 1, b) * mix
            xres_s[rows, :] = x
        gain = g_ref[...] * (1.0 + _mod_row(mod_ref, mod_row + 1, b))
        ms = jnp.mean(x * x, axis=-1, keepdims=True)
        h_next[...] = (x * lax.rsqrt(ms + EPS) * gain + _mod_row(mod_ref, mod_row, b)).astype(BF16)

    def matmul_stage(sub, h_cur, stream_weights):
        rows = pl.ds(sub * tm, tm)
        if stream_weights:
            stream_begin()
        for c in range(n_w):
            if stream_weights:
                stream_gate_up(c)
            g = _dot(h_cur[...], wg_s[c])
            u = _dot(h_cur[...], wu_s[c])
            a_s[:, c * fw:(c + 1) * fw] = (_silu(g) * u).astype(BF16)
            if stream_weights and c < n_d:
                stream_down(c)
        if stream_weights:
            for m in range(n_w, n_d):
                stream_down(m)
        y = _dot(a_s[...], wd_s[...])
        x = xres_s[rows, :] if mix_residual else x_ref[rows, :]
        out = x + FFN_RES_WEIGHT * _mod_row(mod_ref, mod_row + 2, batch_of(sub)) * y
        if final_norm:
            out = _rms_norm(out, gf_ref[...])
        o_ref[rows, :] = out

    def block(first):
        bufs = (h_even, h_odd)
        norm_stage(0, bufs[0])
        for sub in range(n_sub):
            if sub + 1 < n_sub:
                norm_stage(sub + 1, bufs[(sub + 1) % 2])
            matmul_stage(sub, bufs[sub % 2], stream_weights=first and sub == 0)

    if stream_first_tile:
        @pl.when(t == 0)
        def _():
            block(first=True)

        @pl.when(t > 0)
        def _():
            block(first=False)
    else:
        @pl.when(t == -1)
        def _():
            weight_stage()

        @pl.when(t >= 0)
        def _():
            block(first=False)


def _ffn(x2d, mod3, gain, wg, wu, wd, *, mod_row, seq, stream_first_tile, mix=None, final_gain=None, tm=512,
         tiles_per_step=2, fw=256, stage_chunks=8):
    m, d = x2d.shape
    f = wg.shape[1]
    bm = tm * tiles_per_step
    assert f % fw == 0 and seq % tm == 0 and m % bm == 0 and d % stage_chunks == 0 and f % stage_chunks == 0
    n_w, n_blocks, tiles_per_seq = f // fw, m // bm, seq // tm
    final_norm = final_gain is not None
    lead_steps = 0 if stream_first_tile else 1
    if stream_first_tile:
        stage = [pltpu.VMEM((2, 2, d, fw), F32), pltpu.VMEM((1, f // n_w, d), F32),
                 pltpu.SemaphoreType.DMA((2, 2)), pltpu.SemaphoreType.DMA((1,))]
    else:
        stage = [pltpu.VMEM((2, d // stage_chunks, f), F32), pltpu.VMEM((2, f // stage_chunks, d), F32),
                 pltpu.SemaphoreType.DMA((2,)), pltpu.SemaphoreType.DMA((2,))]

    def block(s):
        return (jnp.maximum(s - lead_steps, 0), 0)

    in_specs = [pl.BlockSpec((bm, d), block), _resident(mod3.shape)]
    args = [x2d, mod3]
    if mix is not None:
        out_a, out_b, w_o = mix
        wa, wb = out_a.shape[1], out_b.shape[1]
        in_specs += [pl.BlockSpec((bm, wa), block), pl.BlockSpec((bm, wb), block), _resident((wa + wb, d))]
        args += [out_a, out_b, w_o]
    hbm = pl.BlockSpec(memory_space=pl.ANY)
    in_specs += [_resident((1, d)), hbm, hbm, hbm]
    args += [gain.reshape(1, d), wg, wu, wd]
    if final_norm:
        in_specs.append(_resident((1, d)))
        args.append(final_gain.reshape(1, d))
    xres_rows = bm if mix is not None else 8
    return pl.pallas_call(
        functools.partial(_ffn_kernel, mod_row=mod_row, tiles_per_seq=tiles_per_seq, mix_residual=mix is not None,
                          final_norm=final_norm, stream_first_tile=stream_first_tile),
        grid=(n_blocks + lead_steps,),
        in_specs=in_specs,
        out_specs=pl.BlockSpec((bm, d), block),
        out_shape=jax.ShapeDtypeStruct((m, d), F32),
        scratch_shapes=[pltpu.VMEM((n_w, d, fw), BF16), pltpu.VMEM((n_w, d, fw), BF16), pltpu.VMEM((f, d), BF16),
                        pltpu.VMEM((tm, d), BF16), pltpu.VMEM((tm, d), BF16),
                        pltpu.VMEM((xres_rows, d), F32), pltpu.VMEM((tm, f), BF16)] + stage,
        compiler_params=_params(1),
        name="ffn_final" if final_norm else "ffn",
    )(*args)


_QA0, _KVA0, _QLAT0, _KVLAT0, _KR0, _KRS0, _PROJ_W = 0, 512, 768, 1024, 1152, 1216, 1280
_UQ_ROPE0 = MLA_HEADS * MLA_NOPE
_UQ_ROPES0 = _UQ_ROPE0 + MLA_HEADS * MLA_ROPE


def _mix_proj_kernel(x_ref, mod_ref, g_ref, win_ref, qn_ref, kvn_ref, wuq_ref, wukv_ref, cos_ref, sin_ref,
                     qa_ref, kva_ref, qm_ref, km_ref, vm_ref, win_s, *, blocks_per_seq, sub_rows):
    @pl.when(pl.program_id(0) == 0)
    def _():
        w = win_ref[...]
        half = MLA_ROPE // 2
        win_s[:, 0:_KRS0] = w.astype(BF16)
        win_s[:, _KRS0:_KRS0 + half] = w[:, _KR0 + half:_KRS0].astype(BF16)
        win_s[:, _KRS0 + half:_PROJ_W] = w[:, _KR0:_KR0 + half].astype(BF16)

    b = pl.program_id(0) // blocks_per_seq
    shift = _mod_row(mod_ref, 3, b)
    scale = _mod_row(mod_ref, 4, b)
    q_scale = MLA_QK ** -0.5 * LOG2E
    for r0 in range(0, x_ref.shape[0], sub_rows):
        rows = slice(r0, r0 + sub_rows)
        h = (_rms_norm(x_ref[rows, :], g_ref[...]) * (1.0 + scale) + shift).astype(BF16)
        proj = _dot(h, win_s[...])
        for hd in range(SWA_HEADS):
            qa_ref[0, hd, rows, :] = (proj[:, _QA0 + hd * SWA_HEAD_DIM:_QA0 + (hd + 1) * SWA_HEAD_DIM]
                                      * (SWA_HEAD_DIM ** -0.5 * LOG2E)).astype(BF16)
        kva_ref[rows, :] = proj[:, _KVA0:_QLAT0].astype(BF16)

        q_lat = _rms_norm(proj[:, _QLAT0:_KVLAT0], qn_ref[...]).astype(BF16)
        kv_lat = _rms_norm(proj[:, _KVLAT0:_KR0], kvn_ref[...]).astype(BF16)
        q_all = _dot(q_lat, wuq_ref[...])
        kv_all = _dot(kv_lat, wukv_ref[...])

        cos = cos_ref[rows, :]
        sin = sin_ref[rows, :]
        k_rope = (proj[:, _KR0:_KRS0] * cos + proj[:, _KRS0:_PROJ_W] * sin).astype(BF16)
        for hd in range(MLA_HEADS):
            q_nope = q_all[:, hd * MLA_NOPE:(hd + 1) * MLA_NOPE]
            q_rope = (q_all[:, _UQ_ROPE0 + hd * MLA_ROPE:_UQ_ROPE0 + (hd + 1) * MLA_ROPE] * cos
                      + q_all[:, _UQ_ROPES0 + hd * MLA_ROPE:_UQ_ROPES0 + (hd + 1) * MLA_ROPE] * sin)
            qm_ref[0, hd, rows, 0:MLA_NOPE] = (q_nope * q_scale).astype(BF16)
            qm_ref[0, hd, rows, MLA_NOPE:MLA_QK] = (q_rope * q_scale).astype(BF16)
            kv0 = hd * (MLA_NOPE + MLA_V)
            km_ref[0, hd, rows, 0:MLA_NOPE] = kv_all[:, kv0:kv0 + MLA_NOPE].astype(BF16)
            km_ref[0, hd, rows, MLA_NOPE:MLA_QK] = k_rope
            vm_ref[0, hd, rows, :] = kv_all[:, kv0 + MLA_NOPE:kv0 + MLA_NOPE + MLA_V].astype(BF16)


def _rope_tables(seq):
    inv = np.float32(ROPE_THETA) ** (-np.arange(0, MLA_ROPE, 2, dtype=np.float32) / np.float32(MLA_ROPE))
    ang = np.arange(seq, dtype=np.float32)[:, None] * inv[None, :].astype(np.float32)
    cos, sin = np.cos(ang).astype(np.float32), np.sin(ang).astype(np.float32)
    return np.concatenate([cos, cos], axis=-1), np.concatenate([-sin, sin], axis=-1)


def _swap_halves(w):
    half = w.shape[-1] // 2
    return jnp.concatenate([w[..., half:], w[..., :half]], axis=-1)


def _mix_proj(x2d, mod3, gain, w_in, q_norm, kv_norm, w_uq, w_ukv, *, batch, seq, tm=1024, sub_rows=512):
    m, d = x2d.shape
    tiles_per_seq = seq // tm
    assert seq % tm == 0 and tm % sub_rows == 0
    w_uq_h = w_uq.reshape(MLA_Q_RANK, MLA_HEADS, MLA_QK)
    uq_rope = w_uq_h[:, :, MLA_NOPE:]
    w_uq_r = jnp.concatenate([w_uq_h[:, :, :MLA_NOPE].reshape(MLA_Q_RANK, -1),
                              uq_rope.reshape(MLA_Q_RANK, -1),
                              _swap_halves(uq_rope).reshape(MLA_Q_RANK, -1)], axis=1).astype(BF16)
    w_ukv_b = w_ukv.astype(BF16)
    cos2, sin2 = (jnp.asarray(tab) for tab in _rope_tables(seq))

    def head_spec(width):
        return pl.BlockSpec((1, MLA_HEADS, tm, width),
                            lambda i: (i // tiles_per_seq, 0, i % tiles_per_seq, 0))

    def head_shape(width):
        return jax.ShapeDtypeStruct((batch, MLA_HEADS, seq, width), BF16)

    return pl.pallas_call(
        functools.partial(_mix_proj_kernel, blocks_per_seq=tiles_per_seq, sub_rows=sub_rows),
        grid=(m // tm,),
        in_specs=[pl.BlockSpec((tm, d), lambda i: (i, 0)),
                  _resident(mod3.shape),
                  _resident((1, d)),
                  _resident(w_in.shape),
                  _resident((1, MLA_Q_RANK)), _resident((1, MLA_KV_RANK)),
                  _resident(w_uq_r.shape), _resident(w_ukv_b.shape),
                  pl.BlockSpec((tm, MLA_ROPE), lambda i: (i % tiles_per_seq, 0)),
                  pl.BlockSpec((tm, MLA_ROPE), lambda i: (i % tiles_per_seq, 0))],
        out_specs=[pl.BlockSpec((1, SWA_HEADS, tm, SWA_HEAD_DIM),
                                lambda i: (i // tiles_per_seq, 0, i % tiles_per_seq, 0)),
                   pl.BlockSpec((tm, 2 * SWA_KV_HEADS * SWA_HEAD_DIM), lambda i: (i, 0)),
                   head_spec(MLA_QK), head_spec(MLA_QK), head_spec(MLA_V)],
        out_shape=[jax.ShapeDtypeStruct((batch, SWA_HEADS, seq, SWA_HEAD_DIM), BF16),
                   jax.ShapeDtypeStruct((m, 2 * SWA_KV_HEADS * SWA_HEAD_DIM), BF16),
                   head_shape(MLA_QK), head_shape(MLA_QK), head_shape(MLA_V)],
        scratch_shapes=[pltpu.VMEM((d, _PROJ_W), BF16)],
        compiler_params=_params(1),
        name="mix_proj",
    )(x2d, mod3, gain.reshape(1, d), w_in, q_norm.reshape(1, -1), kv_norm.reshape(1, -1),
      w_uq_r, w_ukv_b, cos2, sin2)


def _t5_bucket_table():
    qi = np.arange(WINDOW)[:, None]
    kj = np.arange(2 * WINDOW)[None, :]
    dist = qi + WINDOW - kj
    max_exact = NUM_BUCKETS // 2
    n = np.maximum(dist, 0)
    nf = np.maximum(n, 1).astype(np.float32)
    large = max_exact + (np.log(nf / np.float32(max_exact)) / np.float32(math.log(MAX_DISTANCE / max_exact))
                         * np.float32(NUM_BUCKETS - max_exact)).astype(np.int32)
    large = np.minimum(large, NUM_BUCKETS - 1)
    bucket = np.where(n < max_exact, n, large)
    band = (dist >= 0) & (dist < WINDOW)
    return np.where(band, bucket, -1).astype(np.int32)


def _swa_bias_kernel(rel_ref, bucket_ref, o_ref):
    bucket = bucket_ref[...]
    first_block_ok = lax.broadcasted_iota(jnp.int32, bucket.shape, 0) >= WINDOW
    for hd in range(SWA_HEADS):
        acc = jnp.full(bucket.shape, MASK_VALUE, F32)
        for b in range(NUM_BUCKETS):
            acc = jnp.where(bucket == b, rel_ref[b, hd] * LOG2E, acc)
        g, j = divmod(hd, SWA_GROUP)
        o_ref[0, g, :, j * WINDOW:(j + 1) * WINDOW] = jnp.where(first_block_ok, acc, MASK_VALUE)
        o_ref[1, g, :, j * WINDOW:(j + 1) * WINDOW] = acc


def _swa_bias(rel_bias):
    shape = (2, SWA_KV_HEADS, 2 * WINDOW, SWA_GROUP * WINDOW)
    return pl.pallas_call(
        _swa_bias_kernel,
        in_specs=[pl.BlockSpec(memory_space=pltpu.SMEM),
                  pl.BlockSpec((2 * WINDOW, WINDOW), lambda: (0, 0))],
        out_specs=pl.BlockSpec(shape, lambda: (0, 0, 0, 0)),
        out_shape=jax.ShapeDtypeStruct(shape, F32),
        name="swa_bias",
    )(rel_bias, jnp.asarray(_t5_bucket_table().T))


def _swa_kernel(sink_ref, q_ref, kv_ref, bias_ref, o_ref, kp_ref, vt_ref, *, blocks_per_iter):
    seq = kv_ref.shape[0]
    n_blocks = seq // WINDOW
    kdim = SWA_KV_HEADS * SWA_HEAD_DIM
    kp_ref[0:WINDOW, :] = jnp.zeros((WINDOW, kdim), BF16)
    kp_ref[WINDOW:, :] = kv_ref[:, 0:kdim]
    ones_rows = jnp.ones((SWA_ONES_ROWS, WINDOW), BF16)
    for g in range(SWA_KV_HEADS):
        vt_ref[0, g] = jnp.concatenate([jnp.zeros((SWA_HEAD_DIM, WINDOW), BF16), ones_rows], axis=0)
    for n in range(n_blocks):
        for g in range(SWA_KV_HEADS):
            v_blk = kv_ref[n * WINDOW:(n + 1) * WINDOW, kdim + g * SWA_HEAD_DIM:kdim + (g + 1) * SWA_HEAD_DIM]
            vt_ref[n + 1, g] = jnp.concatenate([v_blk.astype(F32).T.astype(BF16), ones_rows], axis=0)
    sink_rows = [[jnp.full((1, WINDOW), sink_ref[g * SWA_GROUP + j] * LOG2E, F32) for j in range(SWA_GROUP)]
                 for g in range(SWA_KV_HEADS)]

    def body(it, carry):
        units = [(b, g) for b in range(blocks_per_iter) for g in range(SWA_KV_HEADS)]
        blocks = [it * blocks_per_iter + b for b in range(blocks_per_iter)]
        q_starts = [pl.multiple_of(n * WINDOW, WINDOW) for n in blocks]
        tables = [jnp.minimum(n, 1) for n in blocks]
        sts, pts = {}, {}

        def scores(u):
            b, g = units[u]
            k = kp_ref[pl.ds(q_starts[b], 2 * WINDOW), g * SWA_HEAD_DIM:(g + 1) * SWA_HEAD_DIM]
            sts[u] = []
            for j in range(0, SWA_GROUP, 2):
                hd = g * SWA_GROUP + j
                q = jnp.concatenate([q_ref[0, hd, pl.ds(q_starts[b], WINDOW), :],
                                     q_ref[0, hd + 1, pl.ds(q_starts[b], WINDOW), :]], axis=0)
                st = lax.dot_general(k, q, NT_DIMS, preferred_element_type=F32)
                sts[u] += [st[:, 0:WINDOW], st[:, WINDOW:2 * WINDOW]]

        def softmax(u):
            b, g = units[u]
            pts[u] = []
            for j in range(SWA_GROUP):
                st = sts[u][j] + bias_ref[tables[b], g, :, j * WINDOW:(j + 1) * WINDOW]
                mx = jnp.maximum(jnp.max(st, axis=0, keepdims=True), sink_rows[g][j])
                pts[u].append((jnp.exp2(st - mx).astype(BF16), jnp.exp2(sink_rows[g][j] - mx)))

        def values(u):
            b, g = units[u]
            vt = jnp.concatenate([vt_ref[blocks[b], g], vt_ref[blocks[b] + 1, g]], axis=1)
            for j in range(0, SWA_GROUP, 2):
                pt = jnp.concatenate([pts[u][j][0], pts[u][j + 1][0]], axis=1)
                ot = _dot(vt, pt)
                halves = []
                for jj in range(2):
                    lanes = slice(jj * WINDOW, (jj + 1) * WINDOW)
                    denom = ot[SWA_HEAD_DIM:SWA_HEAD_DIM + 1, lanes] + pts[u][j + jj][1]
                    halves.append(ot[0:SWA_HEAD_DIM, lanes] * (1.0 / denom))
                pair = jnp.concatenate(halves, axis=0).T
                lane0 = (g * SWA_GROUP + j) * SWA_HEAD_DIM
                o_ref[pl.ds(q_starts[b], WINDOW), lane0:lane0 + 2 * SWA_HEAD_DIM] = pair.astype(BF16)

        for t in range(len(units) + 2):
            if t < len(units):
                scores(t)
            if 0 <= t - 1 < len(units):
                softmax(t - 1)
            if 0 <= t - 2 < len(units):
                values(t - 2)
        return carry

    lax.fori_loop(0, n_blocks // blocks_per_iter, body, 0)


def _swa(qa, kva, bias, sinks, *, batch, seq, blocks_per_iter=16):
    kvw = kva.shape[1]
    qw = SWA_HEADS * SWA_HEAD_DIM
    assert (seq // WINDOW) % blocks_per_iter == 0
    return pl.pallas_call(
        functools.partial(_swa_kernel, blocks_per_iter=blocks_per_iter),
        grid=(batch,),
        in_specs=[pl.BlockSpec(memory_space=pltpu.SMEM),
                  pl.BlockSpec((1, SWA_HEADS, seq, SWA_HEAD_DIM), lambda b: (b, 0, 0, 0)),
                  pl.BlockSpec((seq, kvw), lambda b: (b, 0)),
                  _resident(bias.shape)],
        out_specs=pl.BlockSpec((seq, qw), lambda b: (b, 0)),
        out_shape=jax.ShapeDtypeStruct((batch * seq, qw), BF16),
        scratch_shapes=[pltpu.VMEM((WINDOW + seq, SWA_KV_HEADS * SWA_HEAD_DIM), BF16),
                        pltpu.VMEM((1 + seq // WINDOW, SWA_KV_HEADS, SWA_HEAD_DIM + SWA_ONES_ROWS, WINDOW), BF16)],
        compiler_params=_params(1),
        name="swa",
    )(sinks, qa, kva, bias)


def _mla_kernel(q_ref, k_ref, v_ref, o_ref, vt_ref, *, tq):
    heads, seq, dv = v_ref.shape[1], v_ref.shape[2], v_ref.shape[3]
    n_tiles = seq // tq
    key = lax.broadcasted_iota(jnp.int32, (tq, tq), 0)
    qry = lax.broadcasted_iota(jnp.int32, (tq, tq), 1)
    causal = key <= qry
    for hd in range(heads):
        for c in range(seq // WINDOW):
            v_blk = v_ref[0, hd, c * WINDOW:(c + 1) * WINDOW, :]
            vt_ref[hd, 0:dv, c * WINDOW:(c + 1) * WINDOW] = v_blk.astype(F32).T.astype(BF16)
        vt_ref[hd, dv:, :] = jnp.ones((MLA_ONES_ROWS, seq), BF16)
    sts, mxs, pts, accs = {}, {}, {}, {}

    def scores(hd, i, j):
        q = q_ref[0, hd, i * tq:(i + 1) * tq, :]
        s = lax.dot_general(k_ref[0, hd, j * tq:(j + 1) * tq, :], q, NT_DIMS, preferred_element_type=F32)
        sts[hd, i, j] = jnp.where(causal, s, MASK_VALUE) if j == i else s

    def col_max(hd, i, j):
        m = jnp.max(sts[hd, i, j], axis=0, keepdims=True)
        mxs[hd, i] = m if j == 0 else jnp.maximum(mxs[hd, i], m)

    def probs(hd, i, j):
        pts[hd, i, j] = jnp.exp2(sts.pop((hd, i, j)) - mxs[hd, i]).astype(BF16)

    def values(hd, i, j):
        part = _dot(vt_ref[hd, :, j * tq:(j + 1) * tq], pts.pop((hd, i, j)))
        accs[hd, i] = part if j == 0 else accs[hd, i] + part
        if j == i:
            ot = accs.pop((hd, i))
            o = ot[0:dv, :] * (1.0 / ot[dv:dv + 1, :])
            o_ref[i * tq:(i + 1) * tq, hd * dv:(hd + 1) * dv] = o.T.astype(BF16)

    stages = (scores, col_max, probs, values)
    units = [(hd, i) for hd in range(heads) for i in range(n_tiles)]
    for t in range(len(units) + len(stages) - 1):
        for j in range(n_tiles):
            for lag, stage in enumerate(stages):
                if 0 <= t - lag < len(units):
                    hd, i = units[t - lag]
                    if j <= i:
                        stage(hd, i, j)


def _mla(qm, km, vm, *, tq=256):
    batch, heads, seq, qk = qm.shape
    dv = vm.shape[-1]
    return pl.pallas_call(
        functools.partial(_mla_kernel, tq=tq),
        grid=(batch,),
        in_specs=[pl.BlockSpec((1, heads, seq, qk), lambda b: (b, 0, 0, 0)),
                  pl.BlockSpec((1, heads, seq, qk), lambda b: (b, 0, 0, 0)),
                  pl.BlockSpec((1, heads, seq, dv), lambda b: (b, 0, 0, 0))],
        out_specs=pl.BlockSpec((seq, heads * dv), lambda b: (b, 0)),
        out_shape=jax.ShapeDtypeStruct((batch * seq, heads * dv), BF16),
        scratch_shapes=[pltpu.VMEM((heads, dv + MLA_ONES_ROWS, seq), BF16)],
        compiler_params=_params(1),
        name="mla",
    )(qm, km, vm)


def kernel(x, c, w_mod, b_mod, norm_ffn1, ffn1_gate, ffn1_up, ffn1_down, norm_mix, w_in, q_norm, kv_norm, w_uq, w_ukv, sinks, w_o, norm_ffn2, ffn2_gate, ffn2_up, ffn2_down, rel_bias, norm_final):
    batch, seq, d = x.shape
    depth = w_mod.shape[0]
    x2d = x.reshape(batch * seq, d)
    bias = _swa_bias(rel_bias)
    for l in range(depth):
        mod3 = _mod(c, w_mod[l], b_mod[l])
        x2d = _ffn(x2d, mod3, norm_ffn1[l], ffn1_gate[l], ffn1_up[l], ffn1_down[l], mod_row=0, seq=seq,
                   stream_first_tile=True)
        qa, kva, qm, km, vm = _mix_proj(x2d, mod3, norm_mix[l], w_in[l], q_norm[l], kv_norm[l],
                                        w_uq[l], w_ukv[l], batch=batch, seq=seq)
        out_a = _swa(qa, kva, bias, sinks[l], batch=batch, seq=seq)
        out_b = _mla(qm, km, vm)
        last = l == depth - 1
        x2d = _ffn(x2d, mod3, norm_ffn2[l], ffn2_gate[l], ffn2_up[l], ffn2_down[l], mod_row=6, seq=seq,
                   stream_first_tile=False, mix=(out_a, out_b, w_o[l].astype(BF16)),
                   final_gain=norm_final if last else None)
    return x2d.reshape(batch, seq, d)
```
